```python
import math
import jax
import jax.numpy as jnp
from jax import lax
import numpy as np

D_MODEL = 2048
BATCH = 4
SEQ = 4096
DEPTH = 4

GRID_W = 64
CTX_LEN = 256
N_EVEN = (DEPTH + 1) // 2
N_ODD = DEPTH // 2
D_FF = 4 * D_MODEL
N_MOD = 6
NORM_EPS = 1e-6

A_WIDTH = D_MODEL // 2
A_HEAD_DIM = 64
A_HEADS = A_WIDTH // (2 * A_HEAD_DIM)
ROPE_FREQS = A_HEAD_DIM // 4
ROPE_BASE = 10000.0
Q_BLOCK = 128

B_WIDTH = D_MODEL - A_WIDTH
B_BLOCKS = 8
B_BLOCK_DIM = B_WIDTH // B_BLOCKS
CONV_W = 4
CONV_LEFT = 2
LRU_C = 8.0

C_HEADS = 4
C_KEY_DIM = D_MODEL // 2 // C_HEADS
C_VAL_DIM = D_MODEL // C_HEADS
C_KEY_W = C_HEADS * C_KEY_DIM
C_GATE_RANK = 16
C_GATE_TAU = 16.0
C_CHUNK = 64

EVEN_IN = 3 * A_WIDTH + 2 * B_WIDTH
ODD_IN = 2 * C_KEY_W + 2 * D_MODEL + 2 * C_GATE_RANK

kernel_name = 'hybrid_diffattn_rglru_gla_dit'


def rmsnorm(x, g):
    xf = x.astype(jnp.float32)
    y = xf * lax.rsqrt(jnp.mean(xf * xf, axis=-1, keepdims=True) + NORM_EPS)
    return (y * g.astype(jnp.float32)).astype(x.dtype)


def modulate(h, shift, scale):
    return h * (1.0 + scale) + shift


def sq_relu_mlp(u, w1, w2):
    return jnp.square(jax.nn.relu(u @ w1)) @ w2


def grid_angles(rows):
    inv = ROPE_BASE ** (-jnp.arange(ROPE_FREQS, dtype=jnp.float32) / ROPE_FREQS)
    r = jnp.repeat(jnp.arange(rows, dtype=jnp.float32), GRID_W)
    col = jnp.tile(jnp.arange(GRID_W, dtype=jnp.float32), rows)
    return r[:, None] * inv, col[:, None] * inv


def rope_axis(x, ang):
    f = ang.shape[-1]
    x1, x2 = x[..., :f], x[..., f:]
    cs, sn = jnp.cos(ang)[:, None, :], jnp.sin(ang)[:, None, :]
    return jnp.concatenate([x1 * cs - x2 * sn, x2 * cs + x1 * sn], axis=-1)


def rope_2d(x, ang_r, ang_c):
    half = x.shape[-1] // 2
    return jnp.concatenate([rope_axis(x[..., :half], ang_r), rope_axis(x[..., half:], ang_c)], axis=-1)


def qk_pair(t):
    t = t.astype(jnp.float32).reshape(t.shape[0], t.shape[1], A_HEADS, 2, A_HEAD_DIM)
    return t[..., 0, :], t[..., 1, :]


def v_heads(t):
    return t.astype(jnp.float32).reshape(t.shape[0], t.shape[1], A_HEADS, 2 * A_HEAD_DIM)


def diff_softmax_core(q1, q2, k1, k2, v, lam):
    p1 = jax.nn.softmax(jnp.einsum('bqhd,bkhd->bhqk', q1, k1), axis=-1)
    p2 = jax.nn.softmax(jnp.einsum('bqhd,bkhd->bhqk', q2, k2), axis=-1)
    return jnp.einsum('bhqk,bkhe->bqhe', p1 - lam * p2, v)


def centred_dwconv(x, w, b):
    L = x.shape[1]
    xp = jnp.pad(x, ((0, 0), (CONV_LEFT, CONV_W - 1 - CONV_LEFT), (0, 0)))
    w = w.astype(jnp.float32)
    return sum(xp[:, j:j + L] * w[j] for j in range(CONV_W)) + b.astype(jnp.float32)


def block_diag(x, w, b):
    xb = x.reshape(x.shape[0], x.shape[1], B_BLOCKS, B_BLOCK_DIM)
    y = jnp.einsum('blni,nij->blnj', xb, w.astype(jnp.float32))
    return y.reshape(x.shape) + b.astype(jnp.float32)


def lru_coeffs(xc, wa, ba, wx, bx, lam):
    r = jax.nn.sigmoid(block_diag(xc, wa, ba))
    gi = jax.nn.sigmoid(block_diag(xc, wx, bx))
    log_a = -LRU_C * r * jax.nn.softplus(-lam.astype(jnp.float32))
    return jnp.exp(log_a), jnp.sqrt(-jnp.expm1(2.0 * log_a)) * gi * xc


def linear_scan(a, b, h0, reverse):
    idx = -1 if reverse else 0
    b = b.at[:, idx].add(a[:, idx] * h0)

    def combine(e1, e2):
        a1, b1 = e1
        a2, b2 = e2
        return a1 * a2, a2 * b1 + b2

    _, h = lax.associative_scan(combine, (a, b), axis=1, reverse=reverse)
    return h


def gla_chunked(q, k, v, log_a, s0):
    B, L, H, _ = q.shape
    dv = v.shape[-1]
    n = L // C_CHUNK

    def to_chunks(t):
        return t.reshape(B, n, C_CHUNK, H, t.shape[-1]).transpose(1, 0, 3, 2, 4)

    mask = jnp.tril(jnp.ones((C_CHUNK, C_CHUNK), dtype=bool))

    def step(S, inp):
        qc, kc, vc, gc = inp
        bcum = jnp.cumsum(gc, axis=2)
        qe = qc * jnp.exp(bcum)
        ke = kc * jnp.exp(-bcum)
        att = jnp.where(mask, jnp.einsum('bhtk,bhsk->bhts', qe, ke), 0.0)
        o = jnp.einsum('bhtk,bhkv->bhtv', qe, S) + jnp.einsum('bhts,bhsv->bhtv', att, vc)
        b_end = bcum[:, :, -1:]
        S = jnp.exp(b_end[:, :, 0])[..., None] * S + jnp.einsum('bhsk,bhsv->bhkv', kc * jnp.exp(b_end - bcum), vc)
        return S, o

    S, o = lax.scan(step, s0, (to_chunks(q), to_chunks(k), to_chunks(v), to_chunks(log_a)))
    return o.transpose(1, 0, 3, 2, 4).reshape(B, L, H, dv), S


def even_mixer(u, uc, ang_r, ang_c, w_in, w_out, lq1, lk1, lq2, lk2, subln_g, lam_init,
               conv_w, conv_b, wa, ba, wx, bx, lru_lam, need_ctx):
    f32 = jnp.float32
    B, L, _ = u.shape
    cuts = [A_WIDTH, 2 * A_WIDTH, 3 * A_WIDTH, 3 * A_WIDTH + B_WIDTH]
    q, k, v, gt, rec = jnp.split(u @ w_in, cuts, axis=-1)
    qc, kc, vc, gtc, recc = jnp.split(uc @ w_in, cuts, axis=-1)
    lam = (jnp.exp(jnp.sum(lq1.astype(f32) * lk1.astype(f32)))
           - jnp.exp(jnp.sum(lq2.astype(f32) * lk2.astype(f32))) + lam_init)
    scale = A_HEAD_DIM ** -0.5
    q1, q2 = (rope_2d(t, ang_r, ang_c) * scale for t in qk_pair(q))
    k1, k2 = (rope_2d(t, ang_r, ang_c) for t in qk_pair(k))
    k1c, k2c = qk_pair(kc)
    v_ctx = v_heads(vc)
    keys1 = jnp.concatenate([k1c, k1], axis=1)
    keys2 = jnp.concatenate([k2c, k2], axis=1)
    vals = jnp.concatenate([v_ctx, v_heads(v)], axis=1)
    nb = L // Q_BLOCK

    def to_blocks(t):
        return t.reshape(B, nb, Q_BLOCK, A_HEADS, A_HEAD_DIM).swapaxes(0, 1)

    o = lax.map(lambda qb: diff_softmax_core(qb[0], qb[1], keys1, keys2, vals, lam),
                (to_blocks(q1), to_blocks(q2)))
    o = o.swapaxes(0, 1).reshape(B, L, A_HEADS, 2 * A_HEAD_DIM)

    def a_post(t):
        return (rmsnorm(t, subln_g) * (1.0 - lam_init)).reshape(t.shape[0], t.shape[1], A_WIDTH)

    x_lat = centred_dwconv(rec.astype(f32), conv_w, conv_b)
    x_ctx = centred_dwconv(recc.astype(f32), conv_w, conv_b)
    h_lat, h_ctx = [], []
    for d in range(2):
        rev = d == 1
        a_c, b_c = lru_coeffs(x_ctx, wa[d], ba[d], wx[d], bx[d], lru_lam[d])
        hc_d = linear_scan(a_c, b_c, jnp.zeros((B, B_WIDTH), f32), rev)
        a_l, b_l = lru_coeffs(x_lat, wa[d], ba[d], wx[d], bx[d], lru_lam[d])
        h_lat.append(linear_scan(a_l, b_l, hc_d[:, 0] if rev else hc_d[:, -1], rev))
        h_ctx.append(hc_d)

    def merge(ya, gate, h):
        yb = jax.nn.gelu(gate.astype(f32)) * (h[0] + h[1])
        return jnp.concatenate([ya, yb], axis=-1).astype(u.dtype) @ w_out

    y = merge(a_post(o), gt, h_lat)
    if not need_ctx:
        return y, None
    q1c, q2c = (t * scale for t in qk_pair(qc))
    oc = diff_softmax_core(q1c, q2c, k1c, k2c, v_ctx, lam)
    return y, merge(a_post(oc), gtc, h_ctx)


def odd_mixer(u, uc, w_in, w_out, gate_w2, gate_b, norm_g, need_ctx):
    f32 = jnp.float32
    cuts = [C_KEY_W, 2 * C_KEY_W, 2 * C_KEY_W + D_MODEL, 2 * C_KEY_W + 2 * D_MODEL]

    def prep(p):
        B, L, _ = p.shape
        q, k, v, r, z = jnp.split(p.astype(f32), cuts, axis=-1)
        q = q.reshape(B, L, C_HEADS, C_KEY_DIM) * C_KEY_DIM ** -0.5
        k = k.reshape(B, L, C_HEADS, C_KEY_DIM)
        v = v.reshape(B, L, C_HEADS, C_VAL_DIM)
        z = z.reshape(B, L, 2, C_GATE_RANK)
        log_a = [(jax.nn.log_sigmoid(z[:, :, d] @ gate_w2[d].astype(f32) + gate_b[d].astype(f32))
                  / C_GATE_TAU).reshape(B, L, C_HEADS, C_KEY_DIM) for d in range(2)]
        return q, k, v, r, log_a

    q, k, v, r, la = prep(u @ w_in)
    qc, kc, vc, rc, lac = prep(uc @ w_in)
    B = u.shape[0]
    o_lat, o_ctx = [], []
    for d in range(2):
        f = (lambda t: t[:, ::-1]) if d == 1 else (lambda t: t)
        s0 = jnp.zeros((B, C_HEADS, C_KEY_DIM, C_VAL_DIM), f32)
        oc_d, s_ctx = gla_chunked(f(qc), f(kc), f(vc), f(lac[d]), s0)
        ol_d, _ = gla_chunked(f(q), f(k), f(v), f(la[d]), s_ctx)
        o_lat.append(f(ol_d))
        o_ctx.append(f(oc_d))

    def out(o, gate):
        y = rmsnorm(o, norm_g).reshape(o.shape[0], o.shape[1], D_MODEL) * jax.nn.silu(gate)
        return y.astype(u.dtype) @ w_out

    y = out(o_lat[0] + o_lat[1], r)
    if not need_ctx:
        return y, None
    return y, out(o_ctx[0] + o_ctx[1], rc)


def setup_inputs(seed: int = 0) -> dict:
    key = jax.random.key(seed)
    keys = iter(list(jax.random.split(key, 32)))
    f32 = jnp.float32

    def nrm(shape, scale=1.0):
        return scale * jax.random.normal(next(keys), shape, f32)

    def gain(shape):
        return 1.0 + nrm(shape, 0.05)

    x = nrm((BATCH, SEQ, D_MODEL))
    c = nrm((BATCH, D_MODEL))
    ctx = nrm((BATCH, CTX_LEN, D_MODEL))
    c_ctx = nrm((D_MODEL,))
    ada_w = nrm((DEPTH, D_MODEL, N_MOD * D_MODEL), 0.5 * D_MODEL ** -0.5)
    ada_b = nrm((DEPTH, N_MOD * D_MODEL), 0.02)
    norm1_g = gain((DEPTH, D_MODEL))
    norm2_g = gain((DEPTH, D_MODEL))
    mlp_w1 = nrm((DEPTH, D_MODEL, D_FF), D_MODEL ** -0.5)
    mlp_w2 = nrm((DEPTH, D_FF, D_MODEL), D_FF ** -0.5)
    ev_w_in = nrm((N_EVEN, D_MODEL, EVEN_IN), D_MODEL ** -0.5)
    ev_w_out = nrm((N_EVEN, D_MODEL, D_MODEL), D_MODEL ** -0.5)
    ev_lambda_q1 = nrm((N_EVEN, A_HEAD_DIM), 0.1)
    ev_lambda_k1 = nrm((N_EVEN, A_HEAD_DIM), 0.1)
    ev_lambda_q2 = nrm((N_EVEN, A_HEAD_DIM), 0.1)
    ev_lambda_k2 = nrm((N_EVEN, A_HEAD_DIM), 0.1)
    ev_subln_g = gain((N_EVEN, 2 * A_HEAD_DIM))
    ev_conv_w = nrm((N_EVEN, CONV_W, B_WIDTH), CONV_W ** -0.5)
    ev_conv_b = nrm((N_EVEN, B_WIDTH), 0.02)
    ev_lru_wa = nrm((N_EVEN, 2, B_BLOCKS, B_BLOCK_DIM, B_BLOCK_DIM), B_BLOCK_DIM ** -0.5)
    ev_lru_ba = nrm((N_EVEN, 2, B_WIDTH), 0.1)
    ev_lru_wx = nrm((N_EVEN, 2, B_BLOCKS, B_BLOCK_DIM, B_BLOCK_DIM), B_BLOCK_DIM ** -0.5)
    ev_lru_bx = nrm((N_EVEN, 2, B_WIDTH), 0.1)
    a_pow = jax.random.uniform(next(keys), (N_EVEN, 2, B_WIDTH), f32, 0.9, 0.999)
    a_base = a_pow ** (1.0 / LRU_C)
    ev_lru_lam = jnp.log(a_base) - jnp.log1p(-a_base)
    od_w_in = nrm((N_ODD, D_MODEL, ODD_IN), D_MODEL ** -0.5)
    od_w_out = nrm((N_ODD, D_MODEL, D_MODEL), D_MODEL ** -0.5)
    od_gate_w2 = nrm((N_ODD, 2, C_GATE_RANK, C_KEY_W), C_GATE_RANK ** -0.5)
    od_gate_b = nrm((N_ODD, 2, C_KEY_W), 0.1)
    od_norm_g = gain((N_ODD, C_VAL_DIM))
    final_g = gain((D_MODEL,))
    return {'x': x, 'c': c, 'ctx': ctx, 'c_ctx': c_ctx, 'ada_w': ada_w, 'ada_b': ada_b,
            'norm1_g': norm1_g, 'norm2_g': norm2_g, 'mlp_w1': mlp_w1, 'mlp_w2': mlp_w2,
            'ev_w_in': ev_w_in, 'ev_w_out': ev_w_out, 'ev_lambda_q1': ev_lambda_q1,
            'ev_lambda_k1': ev_lambda_k1, 'ev_lambda_q2': ev_lambda_q2, 'ev_lambda_k2': ev_lambda_k2,
            'ev_subln_g': ev_subln_g, 'ev_conv_w': ev_conv_w, 'ev_conv_b': ev_conv_b,
            'ev_lru_wa': ev_lru_wa, 'ev_lru_ba': ev_lru_ba, 'ev_lru_wx': ev_lru_wx,
            'ev_lru_bx': ev_lru_bx, 'ev_lru_lam': ev_lru_lam, 'od_w_in': od_w_in,
            'od_w_out': od_w_out, 'od_gate_w2': od_gate_w2, 'od_gate_b': od_gate_b,
            'od_norm_g': od_norm_g, 'final_g': final_g}


def reference(x, c, ctx, c_ctx, ada_w, ada_b, norm1_g, norm2_g, mlp_w1, mlp_w2,
              ev_w_in, ev_w_out, ev_lambda_q1, ev_lambda_k1, ev_lambda_q2, ev_lambda_k2,
              ev_subln_g, ev_conv_w, ev_conv_b, ev_lru_wa, ev_lru_ba, ev_lru_wx, ev_lru_bx,
              ev_lru_lam, od_w_in, od_w_out, od_gate_w2, od_gate_b, od_norm_g, final_g):
    ROWS = x.shape[1] // GRID_W
    ang_r, ang_c = grid_angles(ROWS)
    s_lat = jax.nn.silu(c)
    s_ctx = jax.nn.silu(c_ctx)
    h, hc = x, ctx
    for i in range(DEPTH):
        need_ctx = i < DEPTH - 1
        m = jnp.split((s_lat @ ada_w[i] + ada_b[i])[:, None, :], N_MOD, axis=-1)
        mc = jnp.split(s_ctx @ ada_w[i] + ada_b[i], N_MOD, axis=-1)
        u = modulate(rmsnorm(h, norm1_g[i]), m[0], m[1])
        uc = modulate(rmsnorm(hc, norm1_g[i]), mc[0], mc[1])
        j = i // 2
        if i % 2 == 0:
            lam_init = 0.8 - 0.6 * math.exp(-0.3 * i)
            y, yc = even_mixer(u, uc, ang_r, ang_c, ev_w_in[j], ev_w_out[j],
                               ev_lambda_q1[j], ev_lambda_k1[j], ev_lambda_q2[j], ev_lambda_k2[j],
                               ev_subln_g[j], lam_init, ev_conv_w[j], ev_conv_b[j],
                               ev_lru_wa[j], ev_lru_ba[j], ev_lru_wx[j], ev_lru_bx[j],
                               ev_lru_lam[j], need_ctx)
        else:
            y, yc = odd_mixer(u, uc, od_w_in[j], od_w_out[j], od_gate_w2[j], od_gate_b[j],
                              od_norm_g[j], need_ctx)
        h = h + m[2] * y
        h = h + m[5] * sq_relu_mlp(modulate(rmsnorm(h, norm2_g[i]), m[3], m[4]), mlp_w1[i], mlp_w2[i])
        if need_ctx:
            hc = hc + mc[2] * yc
            hc = hc + mc[5] * sq_relu_mlp(modulate(rmsnorm(hc, norm2_g[i]), mc[3], mc[4]),
                                          mlp_w1[i], mlp_w2[i])
    return rmsnorm(h, final_g)
```

```python
import functools
import math

import jax
import jax.numpy as jnp
from jax import lax
from jax.experimental import pallas as pl
from jax.experimental.pallas import tpu as pltpu

F32 = jnp.float32
BF16 = jnp.bfloat16

NORM_EPS = 1e-6
N_MOD = 6
GRID_W = 64
A_HEAD_DIM = 64
ROPE_FREQS = A_HEAD_DIM // 4
ROPE_BASE = 10000.0
B_BLOCK_DIM = 128
CONV_W = 4
CONV_LEFT = 2
LRU_C = 8.0
C_KEY_DIM = 256
C_VAL_DIM = 512
C_GATE_RANK = 16
C_GATE_TAU = 16.0
C_CHUNK = 64

LANES = 128
SUBLANES = 8
VMEM_LIMIT = 56 * 1024 * 1024


def _params(sem, vmem=VMEM_LIMIT):
    return pltpu.CompilerParams(dimension_semantics=sem, vmem_limit_bytes=vmem)


def _dot(a, b):
    return jnp.dot(a, b, preferred_element_type=F32)


def _dot_nt(a, b):
    return lax.dot_general(a, b, (((1,), (1,)), ((), ())), preferred_element_type=F32)


def _dot_tn(a, b):
    return lax.dot_general(a, b, (((0,), (0,)), ((), ())), preferred_element_type=F32)


def _sigmoid(x):
    return 1.0 / (1.0 + jnp.exp(-x))


def _silu(x):
    return x * _sigmoid(x)


def _softplus(x):
    return jnp.maximum(x, 0.0) + jnp.log(1.0 + jnp.exp(-jnp.abs(x)))


def _log_sigmoid(x):
    return -_softplus(-x)


def _gelu_tanh(x):
    return 0.5 * x * (1.0 + jnp.tanh(math.sqrt(2.0 / math.pi) * (x + 0.044715 * (x * x * x))))


def _rms(x):
    return x * lax.rsqrt(jnp.mean(x * x, axis=-1, keepdims=True) + NORM_EPS)


def _norm_mod(x, g, shift, scale):
    return (_rms(x) * g) * (1.0 + scale) + shift


def _pick_tile(limit, *dims):
    t = limit
    while t > SUBLANES and any(d % t for d in dims):
        t //= 2
    assert all(d % t == 0 for d in dims), (limit, dims)
    return t


def _ada_kernel(c_ref, w_ref, b_ref, o_ref):
    s = _silu(c_ref[...]).astype(BF16)
    o_ref[...] = _dot(s, w_ref[...].astype(BF16)) + b_ref[...]


def _ada_mod(cc, ada_w, ada_b):
    depth, d, n = ada_w.shape
    rows = cc.shape[0]
    tn = _pick_tile(1024, n)
    return pl.pallas_call(
        _ada_kernel,
        grid=(depth, n // tn),
        in_specs=[
            pl.BlockSpec((rows, d), lambda l, j: (0, 0)),
            pl.BlockSpec((None, d, tn), lambda l, j: (l, 0, j)),
            pl.BlockSpec((None, 1, tn), lambda l, j: (l, 0, j)),
        ],
        out_specs=pl.BlockSpec((None, rows, tn), lambda l, j: (l, 0, j)),
        out_shape=jax.ShapeDtypeStruct((depth, rows, n), F32),
        name="ada_mod",
        compiler_params=_params(("parallel", "parallel")),
    )(cc, ada_w, ada_b.reshape(depth, 1, n))


class _Rows:
    def __init__(self, batch, seq, ctx_len, tm):
        self.batch, self.seq, self.ctx = batch, seq, ctx_len
        self.tm = tm
        self.n_lat = batch * seq // tm
        self.n_ctx = batch * ctx_len // tm
        self.per_batch = seq // tm
        self.rows_lat = batch * seq
        self.rows_all = batch * (seq + ctx_len)

    def mod_row(self, i):
        return jnp.where(i < self.n_lat, i // self.per_batch, self.batch)

    def rope_row(self, i):
        return jnp.where(i < self.n_lat, i % self.per_batch, self.per_batch)


def _rope_store(o_ref, x, cos, sa, sb):
    for g in range(x.shape[1] // LANES):
        xg = x[:, g * LANES:(g + 1) * LANES]
        y = xg * cos + pltpu.roll(xg, LANES - ROPE_FREQS, 1) * sa + pltpu.roll(xg, ROPE_FREQS, 1) * sb
        o_ref[:, g * LANES:(g + 1) * LANES] = y.astype(o_ref.dtype)


def _inproj_even_kernel(h_ref, mod_ref, g_ref, w_ref, cos_ref, sa_ref, sb_ref, o_ref, u_ref, *, nq, nqk):
    j = pl.program_id(1)

    @pl.when(j == 0)
    def _():
        u_ref[...] = _norm_mod(h_ref[...], g_ref[...], mod_ref[0:1, :], mod_ref[1:2, :]).astype(BF16)

    acc = _dot(u_ref[...], w_ref[...])

    @pl.when(j < nq)
    def _():
        _rope_store(o_ref, acc * (A_HEAD_DIM ** -0.5), cos_ref[...], sa_ref[...], sb_ref[...])

    @pl.when(jnp.logical_and(j >= nq, j < nqk))
    def _():
        _rope_store(o_ref, acc, cos_ref[...], sa_ref[...], sb_ref[...])

    @pl.when(j >= nqk)
    def _():
        o_ref[...] = acc.astype(o_ref.dtype)


def _inproj_even(h, mods, norm_g, w, rope, rows, a_width):
    d, n = w.shape
    tm = rows.tm
    tn = _pick_tile(512, a_width, n)
    cos, sa, sb = rope
    kern = functools.partial(_inproj_even_kernel, nq=a_width // tn, nqk=2 * a_width // tn)
    tab = pl.BlockSpec((tm, LANES), lambda i, j: (rows.rope_row(i), 0))
    return pl.pallas_call(
        kern,
        grid=(rows.n_lat + rows.n_ctx, n // tn),
        in_specs=[
            pl.BlockSpec((tm, d), lambda i, j: (i, 0)),
            pl.BlockSpec((None, N_MOD, d), lambda i, j: (rows.mod_row(i), 0, 0)),
            pl.BlockSpec((1, d), lambda i, j: (0, 0)),
            pl.BlockSpec((d, tn), lambda i, j: (0, j)),
            tab, tab, tab,
        ],
        out_specs=pl.BlockSpec((tm, tn), lambda i, j: (i, j)),
        out_shape=jax.ShapeDtypeStruct((rows.rows_all, n), BF16),
        scratch_shapes=[pltpu.VMEM((tm, d), BF16)],
        name="inproj_even",
        compiler_params=_params(("parallel", "arbitrary")),
    )(h, mods, norm_g.reshape(1, d), w, cos, sa, sb)


def _inproj_odd_kernel(h_ref, mod_ref, g_ref, w_ref, wz_ref, o_ref, z_ref, u_ref, *, nq):
    j = pl.program_id(1)

    @pl.when(j == 0)
    def _():
        u = _norm_mod(h_ref[...], g_ref[...], mod_ref[0:1, :], mod_ref[1:2, :]).astype(BF16)
        u_ref[...] = u
        z_ref[...] = _dot(u, wz_ref[...])

    acc = _dot(u_ref[...], w_ref[...])

    @pl.when(j < nq)
    def _():
        o_ref[...] = (acc * (C_KEY_DIM ** -0.5)).astype(o_ref.dtype)

    @pl.when(j >= nq)
    def _():
        o_ref[...] = acc.astype(o_ref.dtype)


def _inproj_odd(h, mods, norm_g, w, wz, rows, key_w):
    d, n = w.shape
    tm = rows.tm
    tn = _pick_tile(512, key_w, n)
    kern = functools.partial(_inproj_odd_kernel, nq=key_w // tn)
    return pl.pallas_call(
        kern,
        grid=(rows.n_lat + rows.n_ctx, n // tn),
        in_specs=[
            pl.BlockSpec((tm, d), lambda i, j: (i, 0)),
            pl.BlockSpec((None, N_MOD, d), lambda i, j: (rows.mod_row(i), 0, 0)),
            pl.BlockSpec((1, d), lambda i, j: (0, 0)),
            pl.BlockSpec((d, tn), lambda i, j: (0, j)),
            pl.BlockSpec((d, LANES), lambda i, j: (0, 0)),
        ],
        out_specs=[
            pl.BlockSpec((tm, tn), lambda i, j: (i, j)),
            pl.BlockSpec((tm, LANES), lambda i, j: (i, 0)),
        ],
        out_shape=[
            jax.ShapeDtypeStruct((rows.rows_all, n), BF16),
            jax.ShapeDtypeStruct((rows.rows_all, LANES), F32),
        ],
        scratch_shapes=[pltpu.VMEM((tm, d), BF16)],
        name="inproj_odd",
        compiler_params=_params(("parallel", "arbitrary")),
    )(h, mods, norm_g.reshape(1, d), w, wz)


def _attn_kernel(lam_ref, q_ref, kl_ref, vl_ref, kc_ref, vc_ref, g_ref, o_ref, *, nq, post_scale):
    qi = pl.program_id(2)
    lam = lam_ref[0]
    q = q_ref[...]
    lane = lax.broadcasted_iota(jnp.int32, q.shape, 1)
    zero = jnp.zeros_like(q)
    qmaps = (jnp.where(lane < A_HEAD_DIM, q, zero), jnp.where(lane >= A_HEAD_DIM, q, zero))

    def finish(o1, o2):
        o = o1 - lam * o2
        o_ref[...] = (_rms(o) * g_ref[...] * post_scale).astype(o_ref.dtype)

    @pl.when(qi < nq)
    def _():
        outs = []
        for qm in qmaps:
            sc = _dot_nt(qm, kc_ref[...])
            sl = _dot_nt(qm, kl_ref[...])
            m = jnp.maximum(jnp.max(sc, axis=-1, keepdims=True), jnp.max(sl, axis=-1, keepdims=True))
            pc = jnp.exp(sc - m)
            pl_ = jnp.exp(sl - m)
            den = jnp.sum(pc, axis=-1, keepdims=True) + jnp.sum(pl_, axis=-1, keepdims=True)
            num = _dot(pc.astype(BF16), vc_ref[...]) + _dot(pl_.astype(BF16), vl_ref[...])
            outs.append(num / den)
        finish(*outs)

    @pl.when(qi == nq)
    def _():
        outs = []
        for qm in qmaps:
            sc = _dot_nt(qm, kc_ref[...])
            m = jnp.max(sc, axis=-1, keepdims=True)
            pc = jnp.exp(sc - m)
            den = jnp.sum(pc, axis=-1, keepdims=True)
            outs.append(_dot(pc.astype(BF16), vc_ref[...]) / den)
        finish(*outs)


def _diff_attn(p, lam, subln_g, post_scale, batch, seq, ctx_len, heads):
    tq = ctx_len
    nq = seq // tq
    n_ctx_blk0 = batch * seq // ctx_len

    def qrow(b, h, qi):
        return jnp.where(qi < nq, b * nq + qi, n_ctx_blk0 + b)

    kern = functools.partial(_attn_kernel, nq=nq, post_scale=post_scale)
    return pl.pallas_call(
        kern,
        grid=(batch, heads, nq + 1),
        in_specs=[
            pl.BlockSpec(memory_space=pltpu.SMEM),
            pl.BlockSpec((tq, LANES), lambda b, h, qi: (qrow(b, h, qi), h)),
            pl.BlockSpec((seq, LANES), lambda b, h, qi: (b, heads + h)),
            pl.BlockSpec((seq, LANES), lambda b, h, qi: (b, 2 * heads + h)),
            pl.BlockSpec((ctx_len, LANES), lambda b, h, qi: (n_ctx_blk0 + b, heads + h)),
            pl.BlockSpec((ctx_len, LANES), lambda b, h, qi: (n_ctx_blk0 + b, 2 * heads + h)),
            pl.BlockSpec((1, LANES), lambda b, h, qi: (0, 0)),
        ],
        out_specs=pl.BlockSpec((tq, LANES), lambda b, h, qi: (qrow(b, h, qi), h)),
        out_shape=jax.ShapeDtypeStruct((batch * (seq + ctx_len), heads * LANES), BF16),
        name="diff_attn",
        compiler_params=_params(("parallel", "parallel", "arbitrary")),
    )(lam, p, p, p, p, p, subln_g.reshape(1, LANES))


def _dwconv(x, w, b):
    n = x.shape[0]
    row = lax.broadcasted_iota(jnp.int32, x.shape, 0)
    y = b + x * w[CONV_LEFT:CONV_LEFT + 1, :]
    for j in range(CONV_W):
        off = j - CONV_LEFT
        if off == 0:
            continue
        xs = pltpu.roll(x, (-off) % n, 0)
        ok = jnp.logical_and(row + off >= 0, row + off < n)
        y = y + jnp.where(ok, xs, 0.0) * w[j:j + 1, :]
    return y


def _lru_kernel(recl_ref, recc_ref, gtl_ref, gtc_ref, cw_ref, cb_ref, wa_ref, ba_ref, wx_ref, bx_ref,
                lam_ref, ol_ref, oc_ref, xc_s, a_s, b_s, h_s, *, ctx_len, seq):
    total = ctx_len + seq
    nc = ctx_len // SUBLANES
    nt = total // SUBLANES
    cw = cw_ref[...]
    cb = cb_ref[...]
    xc_s[0:ctx_len, :] = _dwconv(recc_ref[...].astype(F32), cw, cb)
    xc_s[ctx_len:total, :] = _dwconv(recl_ref[...].astype(F32), cw, cb)

    xc = xc_s[...]
    xcb = xc.astype(BF16)
    for d in range(2):
        r = _sigmoid(_dot(xcb, wa_ref[d].astype(BF16)) + ba_ref[d])
        gi = _sigmoid(_dot(xcb, wx_ref[d].astype(BF16)) + bx_ref[d])
        a = jnp.exp((-LRU_C) * r * _softplus(-lam_ref[d]))
        a_s[d] = a
        b_s[d] = jnp.sqrt(1.0 - a * a) * gi * xc

    row = lax.broadcasted_iota(jnp.int32, (SUBLANES, LANES), 0)

    def scan_tile(a, b, reverse):
        s = 1
        while s < SUBLANES:
            sh = (SUBLANES - s) if reverse else s
            a_sh = pltpu.roll(a, sh, 0)
            b_sh = pltpu.roll(b, sh, 0)
            ok = (row < SUBLANES - s) if reverse else (row >= s)
            b = jnp.where(ok, a * b_sh + b, b)
            a = jnp.where(ok, a * a_sh, a)
            s *= 2
        return a, b

    def body(p, carry):
        cf, cbk = carry
        tf = pl.multiple_of(p * SUBLANES, SUBLANES)
        a, b = scan_tile(a_s[0, pl.ds(tf, SUBLANES), :], b_s[0, pl.ds(tf, SUBLANES), :], False)
        hf = a * cf + b
        h_s[0, pl.ds(tf, SUBLANES), :] = hf
        cf = hf[SUBLANES - 1:SUBLANES, :]
        tb = pl.multiple_of(jnp.where(p < nc, nc - 1 - p, nt + nc - 1 - p) * SUBLANES, SUBLANES)
        a, b = scan_tile(a_s[1, pl.ds(tb, SUBLANES), :], b_s[1, pl.ds(tb, SUBLANES), :], True)
        hb = a * cbk + b
        h_s[1, pl.ds(tb, SUBLANES), :] = hb
        cbk = hb[0:1, :]
        return cf, cbk

    zero = jnp.zeros((1, LANES), F32)
    lax.fori_loop(0, nt, body, (zero, zero))

    h = h_s[0] + h_s[1]
    oc_ref[...] = (_gelu_tanh(gtc_ref[...].astype(F32)) * h[0:ctx_len]).astype(oc_ref.dtype)
    ol_ref[...] = (_gelu_tanh(gtl_ref[...].astype(F32)) * h[ctx_len:total]).astype(ol_ref.dtype)


def _rglru(p, conv_w, conv_b, wa, ba, wx, bx, lru_lam, batch, seq, ctx_len, a_width, b_width):
    nb = b_width // B_BLOCK_DIM
    gt_off = 3 * a_width // LANES
    rec_off = (3 * a_width + b_width) // LANES
    n_ctx_blk0 = batch * seq // ctx_len
    total = seq + ctx_len
    kern = functools.partial(_lru_kernel, ctx_len=ctx_len, seq=seq)
    vec = pl.BlockSpec((2, 1, LANES), lambda b, n: (0, 0, n))
    mat = pl.BlockSpec((2, None, B_BLOCK_DIM, B_BLOCK_DIM), lambda b, n: (0, n, 0, 0))
    return pl.pallas_call(
        kern,
        grid=(batch, nb),
        in_specs=[
            pl.BlockSpec((seq, LANES), lambda b, n: (b, rec_off + n)),
            pl.BlockSpec((ctx_len, LANES), lambda b, n: (n_ctx_blk0 + b, rec_off + n)),
            pl.BlockSpec((seq, LANES), lambda b, n: (b, gt_off + n)),
            pl.BlockSpec((ctx_len, LANES), lambda b, n: (n_ctx_blk0 + b, gt_off + n)),
            pl.BlockSpec((CONV_W, LANES), lambda b, n: (0, n)),
            pl.BlockSpec((1, LANES), lambda b, n: (0, n)),
            mat, vec, mat, vec, vec,
        ],
        out_specs=[
            pl.BlockSpec((seq, LANES), lambda b, n: (b, n)),
            pl.BlockSpec((ctx_len, LANES), lambda b, n: (b, n)),
        ],
        out_shape=[
            jax.ShapeDtypeStruct((batch * seq, b_width), BF16),
            jax.ShapeDtypeStruct((batch * ctx_len, b_width), BF16),
        ],
        scratch_shapes=[
            pltpu.VMEM((total, LANES), F32),
            pltpu.VMEM((2, total, LANES), F32),
            pltpu.VMEM((2, total, LANES), F32),
            pltpu.VMEM((2, total, LANES), F32),
        ],
        name="rglru",
        compiler_params=_params(("parallel", "parallel")),
    )(p, p, p, p, conv_w, conv_b.reshape(1, b_width), wa, ba.reshape(2, 1, b_width),
      wx, bx.reshape(2, 1, b_width), lru_lam.reshape(2, 1, b_width))


def _gla_kernel(ql_ref, kl_ref, vl_ref, rl_ref, zl_ref, qc_ref, kc_ref, vc_ref, rc_ref, zc_ref,
                w2_ref, gb_ref, g_ref, al_ref, ac_ref, st_s, ol_s, oc_s, *, ctx_len, seq):
    c = C_CHUNK
    ri = lax.broadcasted_iota(jnp.int32, (c, c), 0)
    ci = lax.broadcasted_iota(jnp.int32, (c, c), 1)
    keep = (ri >= ci, ri <= ci)
    tri = tuple(jnp.where(k, 1.0, 0.0).astype(BF16) for k in keep)
    end_row = (c - 1, 0)

    def chunk(q_ref, k_ref, v_ref, z_ref, o_ref, t0, d, accumulate):
        rows = pl.ds(t0, c)
        q = q_ref[rows, :].astype(F32)
        k = k_ref[rows, :].astype(F32)
        v = v_ref[rows, :]
        x = _dot(z_ref[rows, :].astype(BF16), w2_ref[d]) + gb_ref[d]
        g = _log_sigmoid(x) * (1.0 / C_GATE_TAU)
        g_hi = g.astype(BF16)
        g_lo = (g - g_hi.astype(F32)).astype(BF16)
        bcum = _dot(tri[d], g_hi) + _dot(tri[d], g_lo)
        b_end = bcum[end_row[d]:end_row[d] + 1, :]
        qe = (q * jnp.exp(bcum)).astype(BF16)
        ke = (k * jnp.exp(-bcum)).astype(BF16)
        att = jnp.where(keep[d], _dot_nt(qe, ke), 0.0).astype(BF16)
        st = st_s[d]
        o = _dot_nt(qe, st.astype(BF16)) + _dot(att, v)
        kd = (k * jnp.exp(b_end - bcum)).astype(BF16)
        st_s[d] = st * jnp.exp(b_end) + _dot_tn(v, kd)
        if accumulate:
            o_ref[rows, :] += o
        else:
            o_ref[rows, :] = o

    def sweep(q_ref, k_ref, v_ref, z_ref, o_ref, n):
        half = n // 2

        def run(lo, accumulate):
            def body(i, carry):
                chunk(q_ref, k_ref, v_ref, z_ref, o_ref, pl.multiple_of(i * c, c), 0, accumulate)
                chunk(q_ref, k_ref, v_ref, z_ref, o_ref, pl.multiple_of((n - 1 - i) * c, c), 1, accumulate)
                return carry
            lax.fori_loop(lo, lo + half, body, 0)

        run(0, False)
        run(half, True)

    st_s[...] = jnp.zeros_like(st_s)
    sweep(qc_ref, kc_ref, vc_ref, zc_ref, oc_s, ctx_len // c)
    sweep(ql_ref, kl_ref, vl_ref, zl_ref, ol_s, seq // c)

    g = g_ref[...]
    ac_ref[...] = (_rms(oc_s[...]) * g * _silu(rc_ref[...].astype(F32))).astype(ac_ref.dtype)
    al_ref[...] = (_rms(ol_s[...]) * g * _silu(rl_ref[...].astype(F32))).astype(al_ref.dtype)


def _gla(p, z, w2p, gate_b, norm_g, batch, seq, ctx_len, heads):
    d_model = heads * C_VAL_DIM
    key_w = heads * C_KEY_DIM
    n_ctx_blk0 = batch * seq // ctx_len
    kern = functools.partial(_gla_kernel, ctx_len=ctx_len, seq=seq)

    def lat(width, off):
        return pl.BlockSpec((seq, width), lambda b, h: (b, off + h))

    def ctx(width, off):
        return pl.BlockSpec((ctx_len, width), lambda b, h: (n_ctx_blk0 + b, off + h))

    specs = [
        lat(C_KEY_DIM, 0), lat(C_KEY_DIM, heads), lat(C_VAL_DIM, heads), lat(C_VAL_DIM, 2 * heads),
        pl.BlockSpec((seq, LANES), lambda b, h: (b, 0)),
        ctx(C_KEY_DIM, 0), ctx(C_KEY_DIM, heads), ctx(C_VAL_DIM, heads), ctx(C_VAL_DIM, 2 * heads),
        pl.BlockSpec((ctx_len, LANES), lambda b, h: (n_ctx_blk0 + b, 0)),
    ]
    specs += [
        pl.BlockSpec((2, LANES, C_KEY_DIM), lambda b, h: (0, 0, h)),
        pl.BlockSpec((2, 1, C_KEY_DIM), lambda b, h: (0, 0, h)),
        pl.BlockSpec((1, C_VAL_DIM), lambda b, h: (0, 0)),
    ]
    return pl.pallas_call(
        kern,
        grid=(batch, heads),
        in_specs=specs,
        out_specs=[
            pl.BlockSpec((seq, C_VAL_DIM), lambda b, h: (b, h)),
            pl.BlockSpec((ctx_len, C_VAL_DIM), lambda b, h: (b, h)),
        ],
        out_shape=[
            jax.ShapeDtypeStruct((batch * seq, d_model), BF16),
            jax.ShapeDtypeStruct((batch * ctx_len, d_model), BF16),
        ],
        scratch_shapes=[
            pltpu.VMEM((2, C_VAL_DIM, C_KEY_DIM), F32),
            pltpu.VMEM((seq, C_VAL_DIM), F32),
            pltpu.VMEM((ctx_len, C_VAL_DIM), F32),
        ],
        name="gla",
        compiler_params=_params(("parallel", "parallel")),
    )(p, p, p, p, z, p, p, p, p, z, w2p, gate_b.reshape(2, 1, key_w), norm_g.reshape(1, C_VAL_DIM))


def _outproj_kernel(a0_ref, a1_ref, w0_ref, w1_ref, h_ref, mod_ref, o_ref):
    acc = _dot(a0_ref[...], w0_ref[...]) + _dot(a1_ref[...], w1_ref[...])
    o_ref[...] = h_ref[...] + mod_ref[2:3, :] * acc


def _outproj(a0, a1, col0, col1, w, h, mods, rows, n_tiles):
    d = w.shape[1]
    kh = w.shape[0] // 2
    tm = rows.tm
    tn = _pick_tile(1024, d)
    return pl.pallas_call(
        _outproj_kernel,
        grid=(n_tiles, d // tn),
        in_specs=[
            pl.BlockSpec((tm, kh), lambda i, j: (i, col0)),
            pl.BlockSpec((tm, kh), lambda i, j: (i, col1)),
            pl.BlockSpec((kh, tn), lambda i, j: (0, j)),
            pl.BlockSpec((kh, tn), lambda i, j: (1, j)),
            pl.BlockSpec((tm, tn), lambda i, j: (i, j)),
            pl.BlockSpec((None, N_MOD, tn), lambda i, j: (rows.mod_row(i), 0, j)),
        ],
        out_specs=pl.BlockSpec((tm, tn), lambda i, j: (i, j)),
        out_shape=jax.ShapeDtypeStruct((n_tiles * tm, d), F32),
        name="outproj",
        compiler_params=_params(("parallel", "parallel")),
    )(a0, a1, w, w, h, mods)


def _mlp_kernel(h_ref, mod_ref, g_ref, w1_ref, w2_ref, fg_ref, o_ref, u_ref, acc_ref, *, nf, final_norm):
    j = pl.program_id(1)

    @pl.when(j == 0)
    def _():
        u_ref[...] = _norm_mod(h_ref[...], g_ref[...], mod_ref[3:4, :], mod_ref[4:5, :]).astype(BF16)
        acc_ref[...] = jnp.zeros_like(acc_ref)

    t = jnp.maximum(_dot(u_ref[...], w1_ref[...]), 0.0)
    acc_ref[...] += _dot((t * t).astype(BF16), w2_ref[...])

    @pl.when(j == nf - 1)
    def _():
        y = h_ref[...] + mod_ref[5:6, :] * acc_ref[...]
        if final_norm:
            y = _rms(y) * fg_ref[...]
        o_ref[...] = y


def _mlp(h, mods, norm_g, w1, w2, final_g, rows, n_tiles, final_norm):
    d, ff = w1.shape
    tm = rows.tm
    tf = _pick_tile(512, ff)
    kern = functools.partial(_mlp_kernel, nf=ff // tf, final_norm=final_norm)
    return pl.pallas_call(
        kern,
        grid=(n_tiles, ff // tf),
        in_specs=[
            pl.BlockSpec((tm, d), lambda i, j: (i, 0)),
            pl.BlockSpec((None, N_MOD, d), lambda i, j: (rows.mod_row(i), 0, 0)),
            pl.BlockSpec((1, d), lambda i, j: (0, 0)),
            pl.BlockSpec((d, tf), lambda i, j: (0, j)),
            pl.BlockSpec((tf, d), lambda i, j: (j, 0)),
            pl.BlockSpec((1, d), lambda i, j: (0, 0)),
        ],
        out_specs=pl.BlockSpec((tm, d), lambda i, j: (i, 0)),
        out_shape=jax.ShapeDtypeStruct((n_tiles * tm, d), F32),
        scratch_shapes=[pltpu.VMEM((tm, d), BF16), pltpu.VMEM((tm, d), F32)],
        name="mlp",
        compiler_params=_params(("parallel", "arbitrary")),
    )(h, mods, norm_g.reshape(1, d), w1, w2, final_g.reshape(1, d))


def _rope_tables(seq, pad_rows):
    inv = ROPE_BASE ** (-jnp.arange(ROPE_FREQS, dtype=F32) / ROPE_FREQS)
    t = jnp.arange(seq, dtype=jnp.int32)
    pos_r = (t // GRID_W).astype(F32)[:, None]
    pos_c = (t % GRID_W).astype(F32)[:, None]
    lane = jnp.arange(LANES, dtype=jnp.int32)[None, :]
    m = lane % A_HEAD_DIM
    is_col = (m // (2 * ROPE_FREQS)) == 1
    second = ((m % (2 * ROPE_FREQS)) // ROPE_FREQS) == 1
    ang = jnp.where(is_col, pos_c, pos_r) * inv[lane % ROPE_FREQS]
    cos, sin = jnp.cos(ang), jnp.sin(ang)
    sa = jnp.where(second, 0.0, -sin)
    sb = jnp.where(second, sin, 0.0)
    ones = jnp.ones((pad_rows, LANES), F32)
    zeros = jnp.zeros((pad_rows, LANES), F32)
    return (jnp.concatenate([cos, ones]), jnp.concatenate([sa, zeros]), jnp.concatenate([sb, zeros]))


def kernel(x, c, ctx, c_ctx, ada_w, ada_b, norm1_g, norm2_g, mlp_w1, mlp_w2, ev_w_in, ev_w_out, ev_lambda_q1, ev_lambda_k1, ev_lambda_q2, ev_lambda_k2, ev_subln_g, ev_conv_w, ev_conv_b, ev_lru_wa, ev_lru_ba, ev_lru_wx, ev_lru_bx, ev_lru_lam, od_w_in, od_w_out, od_gate_w2, od_gate_b, od_norm_g, final_g):
    batch, seq, d = x.shape
    ctx_len = ctx.shape[1]
    depth = ada_w.shape[0]
    a_width = d // 2
    b_width = d - a_width
    a_heads = a_width // (2 * A_HEAD_DIM)
    c_heads = d // C_VAL_DIM
    key_w = c_heads * C_KEY_DIM

    tm = _pick_tile(1024, seq, batch * ctx_len)
    rows = _Rows(batch, seq, ctx_len, tm)
    tm_mlp = _pick_tile(512, seq, batch * ctx_len)
    rows_mlp = _Rows(batch, seq, ctx_len, tm_mlp)

    pad = (-(batch + 1)) % SUBLANES
    cc = jnp.concatenate([c, c_ctx[None, :], jnp.zeros((pad, d), F32)], axis=0)
    mods_all = _ada_mod(cc, ada_w, ada_b).reshape(depth, batch + 1 + pad, N_MOD, d)

    rope = _rope_tables(seq, tm)
    h = jnp.concatenate([x.reshape(batch * seq, d), ctx.reshape(batch * ctx_len, d)], axis=0)

    for i in range(depth):
        last = i == depth - 1
        mods = mods_all[i]
        j = i // 2
        if i % 2 == 0:
            lam_init = 0.8 - 0.6 * math.exp(-0.3 * i)
            lam = (jnp.exp(jnp.sum(ev_lambda_q1[j] * ev_lambda_k1[j]))
                   - jnp.exp(jnp.sum(ev_lambda_q2[j] * ev_lambda_k2[j])) + lam_init).reshape(1).astype(F32)
            p = _inproj_even(h, mods, norm1_g[i], ev_w_in[j].astype(BF16), rope, rows, a_width)
            ya = _diff_attn(p, lam, ev_subln_g[j], 1.0 - lam_init, batch, seq, ctx_len, a_heads)
            yb_lat, yb_ctx = _rglru(p, ev_conv_w[j], ev_conv_b[j], ev_lru_wa[j], ev_lru_ba[j],
                                    ev_lru_wx[j], ev_lru_bx[j], ev_lru_lam[j],
                                    batch, seq, ctx_len, a_width, b_width)
            yb = jnp.concatenate([yb_lat, yb_ctx], axis=0)
            a0, a1, col0, col1 = ya, yb, 0, 0
            w_out = ev_w_out[j]
        else:
            w_in = od_w_in[j]
            n_main = 2 * key_w + 2 * d
            wz = jnp.pad(w_in[:, n_main:], ((0, 0), (0, LANES - 2 * C_GATE_RANK))).astype(BF16)
            p, z = _inproj_odd(h, mods, norm1_g[i], w_in[:, :n_main].astype(BF16), wz, rows, key_w)
            w2p = jnp.zeros((2, LANES, key_w), F32)
            for dd in range(2):
                w2p = w2p.at[dd, dd * C_GATE_RANK:(dd + 1) * C_GATE_RANK, :].set(od_gate_w2[j, dd])
            a_lat, a_ctx = _gla(p, z, w2p.astype(BF16), od_gate_b[j], od_norm_g[j], batch, seq, ctx_len, c_heads)
            a_all = jnp.concatenate([a_lat, a_ctx], axis=0)
            a0, a1, col0, col1 = a_all, a_all, 0, 1
            w_out = od_w_out[j]

        n_out = rows.n_lat if last else rows.n_lat + rows.n_ctx
        h = _outproj(a0, a1, col0, col1, w_out.astype(BF16), h, mods, rows, n_out)
        n_mlp = rows_mlp.n_lat if last else rows_mlp.n_lat + rows_mlp.n_ctx
        h = _mlp(h, mods, norm2_g[i], mlp_w1[i].astype(BF16), mlp_w2[i].astype(BF16), final_g,
                 rows_mlp, n_mlp, last)

    return h.reshape(batch, seq, d)
```

```python
import functools
import math

import jax
import jax.numpy as jnp
from jax import lax
from jax.experimental import pallas as pl
from jax.experimental.pallas import tpu as pltpu

F32 = jnp.float32
BF16 = jnp.bfloat16

NORM_EPS = 1e-6
N_MOD = 6
GRID_W = 64
A_HEAD_DIM = 64
ROPE_FREQS = A_HEAD_DIM // 4
ROPE_BASE = 10000.0
B_BLOCK_DIM = 128
CONV_W = 4
CONV_LEFT = 2
LRU_C = 8.0
C_KEY_DIM = 256
C_VAL_DIM = 512
C_GATE_RANK = 16
C_GATE_TAU = 16.0
C_CHUNK = 64

LANES = 128
SUBLANES = 8
VMEM_LIMIT = 56 * 1024 * 1024


def _params(sem, vmem=VMEM_LIMIT):
    return pltpu.CompilerParams(dimension_semantics=sem, vmem_limit_bytes=vmem)


def _dot(a, b):
    return jnp.dot(a, b, preferred_element_type=F32)


def _dot_nt(a, b):
    return lax.dot_general(a, b, (((1,), (1,)), ((), ())), preferred_element_type=F32)


def _dot_tn(a, b):
    return lax.dot_general(a, b, (((0,), (0,)), ((), ())), preferred_element_type=F32)


def _sigmoid(x):
    return 1.0 / (1.0 + jnp.exp(-x))


def _silu(x):
    return x * _sigmoid(x)


def _softplus(x):
    return jnp.maximum(x, 0.0) + jnp.log(1.0 + jnp.exp(-jnp.abs(x)))


def _log_sigmoid(x):
    return -_softplus(-x)


def _gelu_tanh(x):
    return 0.5 * x * (1.0 + jnp.tanh(math.sqrt(2.0 / math.pi) * (x + 0.044715 * (x * x * x))))


def _rms(x):
    return x * lax.rsqrt(jnp.mean(x * x, axis=-1, keepdims=True) + NORM_EPS)


def _norm_mod(x, g, shift, scale):
    return (_rms(x) * g) * (1.0 + scale) + shift


def _pick_tile(limit, *dims):
    t = limit
    while t > SUBLANES and any(d % t for d in dims):
        t //= 2
    assert all(d % t == 0 for d in dims), (limit, dims)
    return t


def _ada_kernel(c_ref, w_ref, b_ref, o_ref):
    s = _silu(c_ref[...]).astype(BF16)
    o_ref[...] = _dot(s, w_ref[...].astype(BF16)) + b_ref[...]


def _ada_mod(cc, ada_w, ada_b):
    depth, d, n = ada_w.shape
    rows = cc.shape[0]
    tn = _pick_tile(1024, n)
    return pl.pallas_call(
        _ada_kernel,
        grid=(depth, n // tn),
        in_specs=[
            pl.BlockSpec((rows, d), lambda l, j: (0, 0)),
            pl.BlockSpec((None, d, tn), lambda l, j: (l, 0, j)),
            pl.BlockSpec((None, 1, tn), lambda l, j: (l, 0, j)),
        ],
        out_specs=pl.BlockSpec((None, rows, tn), lambda l, j: (l, 0, j)),
        out_shape=jax.ShapeDtypeStruct((depth, rows, n), F32),
        name="ada_mod",
        compiler_params=_params(("parallel", "parallel")),
    )(cc, ada_w, ada_b.reshape(depth, 1, n))


class _Rows:
    def __init__(self, batch, seq, ctx_len, tm):
        self.batch, self.seq, self.ctx = batch, seq, ctx_len
        self.tm = tm
        self.n_lat = batch * seq // tm
        self.n_ctx = batch * ctx_len // tm
        self.per_batch = seq // tm
        self.rows_lat = batch * seq
        self.rows_all = batch * (seq + ctx_len)

    def mod_row(self, i):
        return jnp.where(i < self.n_lat, i // self.per_batch, self.batch)

    def rope_row(self, i):
        return jnp.where(i < self.n_lat, i % self.per_batch, self.per_batch)


def _rope_store(o_ref, x, cos, sa, sb):
    for g in range(x.shape[1] // LANES):
        xg = x[:, g * LANES:(g + 1) * LANES]
        y = xg * cos + pltpu.roll(xg, LANES - ROPE_FREQS, 1) * sa + pltpu.roll(xg, ROPE_FREQS, 1) * sb
        o_ref[:, g * LANES:(g + 1) * LANES] = y.astype(o_ref.dtype)


def _inproj_even_kernel(h_ref, mod_ref, g_ref, w_ref, cos_ref, sa_ref, sb_ref, o_ref, u_ref):
    j = pl.program_id(1)

    @pl.when(j == 0)
    def _():
        u_ref[...] = _norm_mod(h_ref[...], g_ref[...], mod_ref[0:1, :], mod_ref[1:2, :]).astype(BF16)

    _rope_store(o_ref, _dot(u_ref[...], w_ref[...]), cos_ref[...], sa_ref[...], sb_ref[...])


def _inproj_even(h, mods, norm_g, w, rope, rows, a_width):
    d, n = w.shape
    tm = rows.tm
    tn = _pick_tile(512, a_width, n)
    cos, sa, sb = rope
    nq, nqk = a_width // tn, 2 * a_width // tn

    def col_type(j):
        return (j >= nq).astype(jnp.int32) + (j >= nqk).astype(jnp.int32)

    tab = pl.BlockSpec((tm, LANES), lambda i, j: (rows.rope_row(i), col_type(j)))
    return pl.pallas_call(
        _inproj_even_kernel,
        grid=(rows.n_lat + rows.n_ctx, n // tn),
        in_specs=[
            pl.BlockSpec((tm, d), lambda i, j: (i, 0)),
            pl.BlockSpec((None, N_MOD, d), lambda i, j: (rows.mod_row(i), 0, 0)),
            pl.BlockSpec((1, d), lambda i, j: (0, 0)),
            pl.BlockSpec((d, tn), lambda i, j: (0, j)),
            tab, tab, tab,
        ],
        out_specs=pl.BlockSpec((tm, tn), lambda i, j: (i, j)),
        out_shape=jax.ShapeDtypeStruct((rows.rows_all, n), BF16),
        scratch_shapes=[pltpu.VMEM((tm, d), BF16)],
        name="inproj_even",
        compiler_params=_params(("parallel", "arbitrary")),
    )(h, mods, norm_g.reshape(1, d), w, cos, sa, sb)


def _inproj_odd_kernel(h_ref, mod_ref, g_ref, w_ref, wz_ref, o_ref, z_ref, u_ref):
    j = pl.program_id(1)

    @pl.when(j == 0)
    def _():
        u = _norm_mod(h_ref[...], g_ref[...], mod_ref[0:1, :], mod_ref[1:2, :]).astype(BF16)
        u_ref[...] = u
        z_ref[...] = _dot(u, wz_ref[...])

    o_ref[...] = _dot(u_ref[...], w_ref[...]).astype(o_ref.dtype)


def _inproj_odd(h, mods, norm_g, w, wz, rows):
    d, n = w.shape
    tm = rows.tm
    tn = _pick_tile(512, n)
    return pl.pallas_call(
        _inproj_odd_kernel,
        grid=(rows.n_lat + rows.n_ctx, n // tn),
        in_specs=[
            pl.BlockSpec((tm, d), lambda i, j: (i, 0)),
            pl.BlockSpec((None, N_MOD, d), lambda i, j: (rows.mod_row(i), 0, 0)),
            pl.BlockSpec((1, d), lambda i, j: (0, 0)),
            pl.BlockSpec((d, tn), lambda i, j: (0, j)),
            pl.BlockSpec((d, LANES), lambda i, j: (0, 0)),
        ],
        out_specs=[
            pl.BlockSpec((tm, tn), lambda i, j: (i, j)),
            pl.BlockSpec((tm, LANES), lambda i, j: (i, 0)),
        ],
        out_shape=[
            jax.ShapeDtypeStruct((rows.rows_all, n), BF16),
            jax.ShapeDtypeStruct((rows.rows_all, LANES), F32),
        ],
        scratch_shapes=[pltpu.VMEM((tm, d), BF16)],
        name="inproj_odd",
        compiler_params=_params(("parallel", "arbitrary")),
    )(h, mods, norm_g.reshape(1, d), w, wz)


def _attn_kernel(lam_ref, q_ref, kl_ref, vl_ref, kc_ref, vc_ref, g_ref, o_ref, *, nq, post_scale):
    qi = pl.program_id(2)
    lam = lam_ref[0]
    q = q_ref[...]
    lane = lax.broadcasted_iota(jnp.int32, q.shape, 1)
    zero = jnp.zeros_like(q)
    qmaps = (jnp.where(lane < A_HEAD_DIM, q, zero), jnp.where(lane >= A_HEAD_DIM, q, zero))

    def finish(o1, o2):
        o = o1 - lam * o2
        o_ref[...] = (_rms(o) * g_ref[...] * post_scale).astype(o_ref.dtype)

    @pl.when(qi < nq)
    def _():
        outs = []
        for qm in qmaps:
            sc = _dot_nt(qm, kc_ref[...])
            sl = _dot_nt(qm, kl_ref[...])
            m = jnp.maximum(jnp.max(sc, axis=-1, keepdims=True), jnp.max(sl, axis=-1, keepdims=True))
            pc = jnp.exp(sc - m)
            pl_ = jnp.exp(sl - m)
            den = jnp.sum(pc, axis=-1, keepdims=True) + jnp.sum(pl_, axis=-1, keepdims=True)
            num = _dot(pc.astype(BF16), vc_ref[...]) + _dot(pl_.astype(BF16), vl_ref[...])
            outs.append(num / den)
        finish(*outs)

    @pl.when(qi == nq)
    def _():
        outs = []
        for qm in qmaps:
            sc = _dot_nt(qm, kc_ref[...])
            m = jnp.max(sc, axis=-1, keepdims=True)
            pc = jnp.exp(sc - m)
            den = jnp.sum(pc, axis=-1, keepdims=True)
            outs.append(_dot(pc.astype(BF16), vc_ref[...]) / den)
        finish(*outs)


def _diff_attn(p, lam, subln_g, post_scale, batch, seq, ctx_len, heads):
    tq = ctx_len
    nq = seq // tq
    n_ctx_blk0 = batch * seq // ctx_len

    def qrow(b, h, qi):
        return jnp.where(qi < nq, b * nq + qi, n_ctx_blk0 + b)

    kern = functools.partial(_attn_kernel, nq=nq, post_scale=post_scale)
    return pl.pallas_call(
        kern,
        grid=(batch, heads, nq + 1),
        in_specs=[
            pl.BlockSpec(memory_space=pltpu.SMEM),
            pl.BlockSpec((tq, LANES), lambda b, h, qi: (qrow(b, h, qi), h)),
            pl.BlockSpec((seq, LANES), lambda b, h, qi: (b, heads + h)),
            pl.BlockSpec((seq, LANES), lambda b, h, qi: (b, 2 * heads + h)),
            pl.BlockSpec((ctx_len, LANES), lambda b, h, qi: (n_ctx_blk0 + b, heads + h)),
            pl.BlockSpec((ctx_len, LANES), lambda b, h, qi: (n_ctx_blk0 + b, 2 * heads + h)),
            pl.BlockSpec((1, LANES), lambda b, h, qi: (0, 0)),
        ],
        out_specs=pl.BlockSpec((tq, LANES), lambda b, h, qi: (qrow(b, h, qi), h)),
        out_shape=jax.ShapeDtypeStruct((batch * (seq + ctx_len), heads * LANES), BF16),
        name="diff_attn",
        compiler_params=_params(("parallel", "parallel", "arbitrary")),
    )(lam, p, p, p, p, p, subln_g.reshape(1, LANES))


def _dwconv(x, w, b):
    n = x.shape[0]
    row = lax.broadcasted_iota(jnp.int32, x.shape, 0)
    y = b + x * w[CONV_LEFT:CONV_LEFT + 1, :]
    for j in range(CONV_W):
        off = j - CONV_LEFT
        if off == 0:
            continue
        xs = pltpu.roll(x, (-off) % n, 0)
        ok = jnp.logical_and(row + off >= 0, row + off < n)
        y = y + jnp.where(ok, xs, 0.0) * w[j:j + 1, :]
    return y


def _lru_kernel(recl_ref, recc_ref, gtl_ref, gtc_ref, cw_ref, cb_ref, wa_ref, ba_ref, wx_ref, bx_ref,
                lam_ref, ol_ref, oc_ref, xc_s, a_s, b_s, h_s, *, ctx_len, seq):
    total = ctx_len + seq
    nc = ctx_len // SUBLANES
    nt = total // SUBLANES
    cw = cw_ref[...]
    cb = cb_ref[...]
    xc_s[0:ctx_len, :] = _dwconv(recc_ref[...].astype(F32), cw, cb)
    xc_s[ctx_len:total, :] = _dwconv(recl_ref[...].astype(F32), cw, cb)

    xc = xc_s[...]
    xcb = xc.astype(BF16)
    for d in range(2):
        r = _sigmoid(_dot(xcb, wa_ref[d].astype(BF16)) + ba_ref[d])
        gi = _sigmoid(_dot(xcb, wx_ref[d].astype(BF16)) + bx_ref[d])
        a = jnp.exp((-LRU_C) * r * _softplus(-lam_ref[d]))
        a_s[d] = a
        b_s[d] = jnp.sqrt(1.0 - a * a) * gi * xc

    row = lax.broadcasted_iota(jnp.int32, (SUBLANES, LANES), 0)

    def scan_tile(a, b, reverse):
        s = 1
        while s < SUBLANES:
            sh = (SUBLANES - s) if reverse else s
            a_sh = pltpu.roll(a, sh, 0)
            b_sh = pltpu.roll(b, sh, 0)
            ok = (row < SUBLANES - s) if reverse else (row >= s)
            b = jnp.where(ok, a * b_sh + b, b)
            a = jnp.where(ok, a * a_sh, a)
            s *= 2
        return a, b

    def body(p, carry):
        cf, cbk = carry
        tf = pl.multiple_of(p * SUBLANES, SUBLANES)
        a, b = scan_tile(a_s[0, pl.ds(tf, SUBLANES), :], b_s[0, pl.ds(tf, SUBLANES), :], False)
        hf = a * cf + b
        h_s[0, pl.ds(tf, SUBLANES), :] = hf
        cf = hf[SUBLANES - 1:SUBLANES, :]
        tb = pl.multiple_of(jnp.where(p < nc, nc - 1 - p, nt + nc - 1 - p) * SUBLANES, SUBLANES)
        a, b = scan_tile(a_s[1, pl.ds(tb, SUBLANES), :], b_s[1, pl.ds(tb, SUBLANES), :], True)
        hb = a * cbk + b
        h_s[1, pl.ds(tb, SUBLANES), :] = hb
        cbk = hb[0:1, :]
        return cf, cbk

    zero = jnp.zeros((1, LANES), F32)
    lax.fori_loop(0, nt, body, (zero, zero))

    h = h_s[0] + h_s[1]
    oc_ref[...] = (_gelu_tanh(gtc_ref[...].astype(F32)) * h[0:ctx_len]).astype(oc_ref.dtype)
    ol_ref[...] = (_gelu_tanh(gtl_ref[...].astype(F32)) * h[ctx_len:total]).astype(ol_ref.dtype)


def _rglru(p, conv_w, conv_b, wa, ba, wx, bx, lru_lam, batch, seq, ctx_len, a_width, b_width):
    nb = b_width // B_BLOCK_DIM
    gt_off = 3 * a_width // LANES
    rec_off = (3 * a_width + b_width) // LANES
    n_ctx_blk0 = batch * seq // ctx_len
    total = seq + ctx_len
    kern = functools.partial(_lru_kernel, ctx_len=ctx_len, seq=seq)
    vec = pl.BlockSpec((2, 1, LANES), lambda b, n: (0, 0, n))
    mat = pl.BlockSpec((2, None, B_BLOCK_DIM, B_BLOCK_DIM), lambda b, n: (0, n, 0, 0))
    return pl.pallas_call(
        kern,
        grid=(batch, nb),
        in_specs=[
            pl.BlockSpec((seq, LANES), lambda b, n: (b, rec_off + n)),
            pl.BlockSpec((ctx_len, LANES), lambda b, n: (n_ctx_blk0 + b, rec_off + n)),
            pl.BlockSpec((seq, LANES), lambda b, n: (b, gt_off + n)),
            pl.BlockSpec((ctx_len, LANES), lambda b, n: (n_ctx_blk0 + b, gt_off + n)),
            pl.BlockSpec((CONV_W, LANES), lambda b, n: (0, n)),
            pl.BlockSpec((1, LANES), lambda b, n: (0, n)),
            mat, vec, mat, vec, vec,
        ],
        out_specs=[
            pl.BlockSpec((seq, LANES), lambda b, n: (b, n)),
            pl.BlockSpec((ctx_len, LANES), lambda b, n: (b, n)),
        ],
        out_shape=[
            jax.ShapeDtypeStruct((batch * seq, b_width), BF16),
            jax.ShapeDtypeStruct((batch * ctx_len, b_width), BF16),
        ],
        scratch_shapes=[
            pltpu.VMEM((total, LANES), F32),
            pltpu.VMEM((2, total, LANES), F32),
            pltpu.VMEM((2, total, LANES), F32),
            pltpu.VMEM((2, total, LANES), F32),
        ],
        name="rglru",
        compiler_params=_params(("parallel", "parallel")),
    )(p, p, p, p, conv_w, conv_b.reshape(1, b_width), wa, ba.reshape(2, 1, b_width),
      wx, bx.reshape(2, 1, b_width), lru_lam.reshape(2, 1, b_width))


def _gla_kernel(ql_ref, kl_ref, vl_ref, rl_ref, zl_ref, qc_ref, kc_ref, vc_ref, rc_ref, zc_ref,
                w2_ref, gb_ref, g_ref, al_ref, ac_ref, st_s, ol_s, oc_s, *, ctx_len, seq):
    c = C_CHUNK
    ri = lax.broadcasted_iota(jnp.int32, (c, c), 0)
    ci = lax.broadcasted_iota(jnp.int32, (c, c), 1)
    keep = (ri >= ci, ri <= ci)
    tri = tuple(jnp.where(k, 1.0, 0.0).astype(BF16) for k in keep)
    end_row = (c - 1, 0)

    def chunk(q_ref, k_ref, v_ref, z_ref, o_ref, t0, d, accumulate):
        rows = pl.ds(t0, c)
        q = q_ref[rows, :].astype(F32)
        k = k_ref[rows, :].astype(F32)
        v = v_ref[rows, :]
        x = _dot(z_ref[rows, :].astype(BF16), w2_ref[d]) + gb_ref[d]
        g = _log_sigmoid(x) * (1.0 / C_GATE_TAU)
        g_hi = g.astype(BF16)
        g_lo = (g - g_hi.astype(F32)).astype(BF16)
        bcum = _dot(tri[d], g_hi) + _dot(tri[d], g_lo)
        b_end = bcum[end_row[d]:end_row[d] + 1, :]
        qe = (q * jnp.exp(bcum)).astype(BF16)
        ke = (k * jnp.exp(-bcum)).astype(BF16)
        att = jnp.where(keep[d], _dot_nt(qe, ke), 0.0).astype(BF16)
        st = st_s[d]
        o = _dot_nt(qe, st.astype(BF16)) + _dot(att, v)
        kd = (k * jnp.exp(b_end - bcum)).astype(BF16)
        st_s[d] = st * jnp.exp(b_end) + _dot_tn(v, kd)
        if accumulate:
            o_ref[rows, :] += o
        else:
            o_ref[rows, :] = o

    def sweep(q_ref, k_ref, v_ref, z_ref, o_ref, n):
        half = n // 2

        def run(lo, accumulate):
            def body(i, carry):
                chunk(q_ref, k_ref, v_ref, z_ref, o_ref, pl.multiple_of(i * c, c), 0, accumulate)
                chunk(q_ref, k_ref, v_ref, z_ref, o_ref, pl.multiple_of((n - 1 - i) * c, c), 1, accumulate)
                return carry
            lax.fori_loop(lo, lo + half, body, 0)

        run(0, False)
        run(half, True)

    st_s[...] = jnp.zeros_like(st_s)
    sweep(qc_ref, kc_ref, vc_ref, zc_ref, oc_s, ctx_len // c)
    sweep(ql_ref, kl_ref, vl_ref, zl_ref, ol_s, seq // c)

    g = g_ref[...]
    ac_ref[...] = (_rms(oc_s[...]) * g * _silu(rc_ref[...].astype(F32))).astype(ac_ref.dtype)
    al_ref[...] = (_rms(ol_s[...]) * g * _silu(rl_ref[...].astype(F32))).astype(al_ref.dtype)


def _gla(p, z, w2p, gate_b, norm_g, batch, seq, ctx_len, heads):
    d_model = heads * C_VAL_DIM
    key_w = heads * C_KEY_DIM
    n_ctx_blk0 = batch * seq // ctx_len
    kern = functools.partial(_gla_kernel, ctx_len=ctx_len, seq=seq)

    def lat(width, off):
        return pl.BlockSpec((seq, width), lambda b, h: (b, off + h))

    def ctx(width, off):
        return pl.BlockSpec((ctx_len, width), lambda b, h: (n_ctx_blk0 + b, off + h))

    specs = [
        lat(C_KEY_DIM, 0), lat(C_KEY_DIM, heads), lat(C_VAL_DIM, heads), lat(C_VAL_DIM, 2 * heads),
        pl.BlockSpec((seq, LANES), lambda b, h: (b, 0)),
        ctx(C_KEY_DIM, 0), ctx(C_KEY_DIM, heads), ctx(C_VAL_DIM, heads), ctx(C_VAL_DIM, 2 * heads),
        pl.BlockSpec((ctx_len, LANES), lambda b, h: (n_ctx_blk0 + b, 0)),
    ]
    specs += [
        pl.BlockSpec((2, LANES, C_KEY_DIM), lambda b, h: (0, 0, h)),
        pl.BlockSpec((2, 1, C_KEY_DIM), lambda b, h: (0, 0, h)),
        pl.BlockSpec((1, C_VAL_DIM), lambda b, h: (0, 0)),
    ]
    return pl.pallas_call(
        kern,
        grid=(batch, heads),
        in_specs=specs,
        out_specs=[
            pl.BlockSpec((seq, C_VAL_DIM), lambda b, h: (b, h)),
            pl.BlockSpec((ctx_len, C_VAL_DIM), lambda b, h: (b, h)),
        ],
        out_shape=[
            jax.ShapeDtypeStruct((batch * seq, d_model), BF16),
            jax.ShapeDtypeStruct((batch * ctx_len, d_model), BF16),
        ],
        scratch_shapes=[
            pltpu.VMEM((2, C_VAL_DIM, C_KEY_DIM), F32),
            pltpu.VMEM((seq, C_VAL_DIM), F32),
            pltpu.VMEM((ctx_len, C_VAL_DIM), F32),
        ],
        name="gla",
        compiler_params=_params(("parallel", "parallel")),
    )(p, p, p, p, z, p, p, p, p, z, w2p, gate_b.reshape(2, 1, key_w), norm_g.reshape(1, C_VAL_DIM))


def _outproj_kernel(al0_ref, ac0_ref, al1_ref, ac1_ref, w0_ref, w1_ref, h_ref, mod_ref, o_ref, *, n_lat):
    is_lat = pl.program_id(0) < n_lat
    a0 = jnp.where(is_lat, al0_ref[...], ac0_ref[...])
    a1 = jnp.where(is_lat, al1_ref[...], ac1_ref[...])
    acc = _dot(a0, w0_ref[...]) + _dot(a1, w1_ref[...])
    o_ref[...] = h_ref[...] + mod_ref[2:3, :] * acc


def _outproj(srcs, w, h, mods, rows, n_tiles):
    d = w.shape[1]
    kh = w.shape[0] // 2
    tm = rows.tm
    n_lat = rows.n_lat
    specs, args = [], []
    for lat_arr, lat_col, ctx_arr, ctx_row0, ctx_col in srcs:
        specs.append(pl.BlockSpec((tm, kh), lambda i, c=lat_col: (jnp.minimum(i, n_lat - 1), c)))
        specs.append(pl.BlockSpec((tm, kh), lambda i, r=ctx_row0, c=ctx_col: (r + jnp.maximum(i - n_lat, 0), c)))
        args += [lat_arr, ctx_arr]
    kern = functools.partial(_outproj_kernel, n_lat=n_lat)
    return pl.pallas_call(
        kern,
        grid=(n_tiles,),
        in_specs=specs + [
            pl.BlockSpec((kh, d), lambda i: (0, 0)),
            pl.BlockSpec((kh, d), lambda i: (1, 0)),
            pl.BlockSpec((tm, d), lambda i: (i, 0)),
            pl.BlockSpec((None, N_MOD, d), lambda i: (rows.mod_row(i), 0, 0)),
        ],
        out_specs=pl.BlockSpec((tm, d), lambda i: (i, 0)),
        out_shape=jax.ShapeDtypeStruct((n_tiles * tm, d), F32),
        name="outproj",
        compiler_params=_params(("parallel",)),
    )(*args, w, w, h, mods)


def _mlp_kernel(h_ref, mod_ref, g_ref, w1_ref, w2_ref, fg_ref, o_ref, u_ref, *, nf, final_norm):
    j = pl.program_id(1)

    @pl.when(j == 0)
    def _():
        u_ref[...] = _norm_mod(h_ref[...], g_ref[...], mod_ref[3:4, :], mod_ref[4:5, :]).astype(BF16)
        o_ref[...] = jnp.zeros_like(o_ref)

    t = jnp.maximum(_dot(u_ref[...], w1_ref[...]), 0.0)
    o_ref[...] += _dot((t * t).astype(BF16), w2_ref[...])

    @pl.when(j == nf - 1)
    def _():
        y = h_ref[...] + mod_ref[5:6, :] * o_ref[...]
        if final_norm:
            y = _rms(y) * fg_ref[...]
        o_ref[...] = y


def _mlp(h, mods, norm_g, w1_tiles, w2, final_g, rows, n_tiles, final_norm):
    nf, d, tf = w1_tiles.shape
    tm = rows.tm
    kern = functools.partial(_mlp_kernel, nf=nf, final_norm=final_norm)
    return pl.pallas_call(
        kern,
        grid=(n_tiles, nf),
        in_specs=[
            pl.BlockSpec((tm, d), lambda i, j: (i, 0)),
            pl.BlockSpec((None, N_MOD, d), lambda i, j: (rows.mod_row(i), 0, 0)),
            pl.BlockSpec((1, d), lambda i, j: (0, 0)),
            pl.BlockSpec((None, d, tf), lambda i, j: (j, 0, 0)),
            pl.BlockSpec((tf, d), lambda i, j: (j, 0)),
            pl.BlockSpec((1, d), lambda i, j: (0, 0)),
        ],
        out_specs=pl.BlockSpec((tm, d), lambda i, j: (i, 0)),
        out_shape=jax.ShapeDtypeStruct((n_tiles * tm, d), F32),
        scratch_shapes=[pltpu.VMEM((tm, d), BF16)],
        name="mlp",
        compiler_params=_params(("parallel", "arbitrary")),
    )(h, mods, norm_g.reshape(1, d), w1_tiles, w2, final_g.reshape(1, d))


def _rope_tables(seq, pad_rows):
    inv = ROPE_BASE ** (-jnp.arange(ROPE_FREQS, dtype=F32) / ROPE_FREQS)
    t = jnp.arange(seq, dtype=jnp.int32)
    pos_r = (t // GRID_W).astype(F32)[:, None]
    pos_c = (t % GRID_W).astype(F32)[:, None]
    lane = jnp.arange(LANES, dtype=jnp.int32)[None, :]
    m = lane % A_HEAD_DIM
    is_col = (m // (2 * ROPE_FREQS)) == 1
    second = ((m % (2 * ROPE_FREQS)) // ROPE_FREQS) == 1
    ang = jnp.where(is_col, pos_c, pos_r) * inv[lane % ROPE_FREQS]
    cos, sin = jnp.cos(ang), jnp.sin(ang)
    sa = jnp.where(second, 0.0, -sin)
    sb = jnp.where(second, sin, 0.0)
    qs = A_HEAD_DIM ** -0.5
    ones = jnp.ones((pad_rows, LANES), F32)
    zeros = jnp.zeros((pad_rows, LANES), F32)
    one_all = jnp.ones((seq + pad_rows, LANES), F32)
    zero_all = jnp.zeros((seq + pad_rows, LANES), F32)

    def table(rot, pad):
        body = jnp.concatenate([rot, pad])
        return jnp.concatenate([qs * body, body, one_all if pad is ones else zero_all], axis=1)

    return table(cos, ones), table(sa, zeros), table(sb, zeros)


def kernel(x, c, ctx, c_ctx, ada_w, ada_b, norm1_g, norm2_g, mlp_w1, mlp_w2, ev_w_in, ev_w_out, ev_lambda_q1, ev_lambda_k1, ev_lambda_q2, ev_lambda_k2, ev_subln_g, ev_conv_w, ev_conv_b, ev_lru_wa, ev_lru_ba, ev_lru_wx, ev_lru_bx, ev_lru_lam, od_w_in, od_w_out, od_gate_w2, od_gate_b, od_norm_g, final_g):
    batch, seq, d = x.shape
    ctx_len = ctx.shape[1]
    depth = ada_w.shape[0]
    a_width = d // 2
    b_width = d - a_width
    a_heads = a_width // (2 * A_HEAD_DIM)
    c_heads = d // C_VAL_DIM
    key_w = c_heads * C_KEY_DIM

    tm = _pick_tile(1024, seq, batch * ctx_len)
    rows = _Rows(batch, seq, ctx_len, tm)
    rows_out = _Rows(batch, seq, ctx_len, _pick_tile(512, seq, batch * ctx_len))
    tf = _pick_tile(512, mlp_w1.shape[2])

    pad = (-(batch + 1)) % SUBLANES
    cc = jnp.concatenate([c, c_ctx[None, :], jnp.zeros((pad, d), F32)], axis=0)
    mods_all = _ada_mod(cc, ada_w, ada_b).reshape(depth, batch + 1 + pad, N_MOD, d)

    rope = _rope_tables(seq, tm)
    h = jnp.concatenate([x.reshape(batch * seq, d), ctx.reshape(batch * ctx_len, d)], axis=0)

    for i in range(depth):
        last = i == depth - 1
        mods = mods_all[i]
        j = i // 2
        if i % 2 == 0:
            lam_init = 0.8 - 0.6 * math.exp(-0.3 * i)
            lam = (jnp.exp(jnp.sum(ev_lambda_q1[j] * ev_lambda_k1[j]))
                   - jnp.exp(jnp.sum(ev_lambda_q2[j] * ev_lambda_k2[j])) + lam_init).reshape(1).astype(F32)
            p = _inproj_even(h, mods, norm1_g[i], ev_w_in[j].astype(BF16), rope, rows, a_width)
            ya = _diff_attn(p, lam, ev_subln_g[j], 1.0 - lam_init, batch, seq, ctx_len, a_heads)
            yb_lat, yb_ctx = _rglru(p, ev_conv_w[j], ev_conv_b[j], ev_lru_wa[j], ev_lru_ba[j],
                                    ev_lru_wx[j], ev_lru_bx[j], ev_lru_lam[j],
                                    batch, seq, ctx_len, a_width, b_width)
            srcs = [(ya, 0, ya, rows_out.n_lat, 0), (yb_lat, 0, yb_ctx, 0, 0)]
            w_out = ev_w_out[j]
        else:
            w_in = od_w_in[j]
            n_main = 2 * key_w + 2 * d
            col_scale = jnp.where(jnp.arange(n_main) < key_w, C_KEY_DIM ** -0.5, 1.0).astype(F32)
            w_main = (w_in[:, :n_main] * col_scale[None, :]).astype(BF16)
            wz = jnp.pad(w_in[:, n_main:], ((0, 0), (0, LANES - 2 * C_GATE_RANK))).astype(BF16)
            p, z = _inproj_odd(h, mods, norm1_g[i], w_main, wz, rows)
            w2p = jnp.zeros((2, LANES, key_w), F32)
            for dd in range(2):
                w2p = w2p.at[dd, dd * C_GATE_RANK:(dd + 1) * C_GATE_RANK, :].set(od_gate_w2[j, dd])
            a_lat, a_ctx = _gla(p, z, w2p.astype(BF16), od_gate_b[j], od_norm_g[j], batch, seq, ctx_len, c_heads)
            srcs = [(a_lat, 0, a_ctx, 0, 0), (a_lat, 1, a_ctx, 0, 1)]
            w_out = od_w_out[j]

        n_out = rows_out.n_lat if last else rows_out.n_lat + rows_out.n_ctx
        h = _outproj(srcs, w_out.astype(BF16), h, mods, rows_out, n_out)
        n_mlp = rows.n_lat if last else rows.n_lat + rows.n_ctx
        w1_tiles = mlp_w1[i].astype(BF16).reshape(d, -1, tf).transpose(1, 0, 2)
        h = _mlp(h, mods, norm2_g[i], w1_tiles, mlp_w2[i].astype(BF16), final_g, rows, n_mlp, last)

    return h.reshape(batch, seq, d)
```

```python
import functools
import math

import jax
import jax.numpy as jnp
from jax import lax
from jax.experimental import pallas as pl
from jax.experimental.pallas import tpu as pltpu

F32 = jnp.float32
BF16 = jnp.bfloat16

NORM_EPS = 1e-6
N_MOD = 6
GRID_W = 64
A_HEAD_DIM = 64
ROPE_FREQS = A_HEAD_DIM // 4
ROPE_BASE = 10000.0
B_BLOCK_DIM = 128
CONV_W = 4
CONV_LEFT = 2
LRU_C = 8.0
C_KEY_DIM = 256
C_VAL_DIM = 512
C_GATE_RANK = 16
C_GATE_TAU = 16.0
C_CHUNK = 64

LANES = 128
SUBLANES = 8
VMEM_LIMIT = 56 * 1024 * 1024


def _params(sem, vmem=VMEM_LIMIT):
    return pltpu.CompilerParams(dimension_semantics=sem, vmem_limit_bytes=vmem)


def _dot(a, b):
    return jnp.dot(a, b, preferred_element_type=F32)


def _dot_nt(a, b):
    return lax.dot_general(a, b, (((1,), (1,)), ((), ())), preferred_element_type=F32)


def _dot_tn(a, b):
    return lax.dot_general(a, b, (((0,), (0,)), ((), ())), preferred_element_type=F32)


def _sigmoid(x):
    return 1.0 / (1.0 + jnp.exp(-x))


def _silu(x):
    return x * _sigmoid(x)


def _softplus(x):
    return jnp.maximum(x, 0.0) + jnp.log(1.0 + jnp.exp(-jnp.abs(x)))


def _log_sigmoid(x):
    return -_softplus(-x)


def _gelu_tanh(x):
    return 0.5 * x * (1.0 + jnp.tanh(math.sqrt(2.0 / math.pi) * (x + 0.044715 * (x * x * x))))


def _rms(x):
    return x * lax.rsqrt(jnp.mean(x * x, axis=-1, keepdims=True) + NORM_EPS)


def _norm_mod(x, g, shift, scale):
    return (_rms(x) * g) * (1.0 + scale) + shift


def _pick_tile(limit, *dims):
    t = limit
    while t > SUBLANES and any(d % t for d in dims):
        t //= 2
    assert all(d % t == 0 for d in dims), (limit, dims)
    return t


def _ada_kernel(c_ref, w_ref, b_ref, o_ref):
    s = _silu(c_ref[...]).astype(BF16)
    o_ref[...] = _dot(s, w_ref[...].astype(BF16)) + b_ref[...]


def _ada_mod(cc, ada_w, ada_b):
    depth, d, n = ada_w.shape
    rows = cc.shape[0]
    tn = _pick_tile(1024, n)
    return pl.pallas_call(
        _ada_kernel,
        grid=(depth, n // tn),
        in_specs=[
            pl.BlockSpec((rows, d), lambda l, j: (0, 0)),
            pl.BlockSpec((None, d, tn), lambda l, j: (l, 0, j)),
            pl.BlockSpec((None, 1, tn), lambda l, j: (l, 0, j)),
        ],
        out_specs=pl.BlockSpec((None, rows, tn), lambda l, j: (l, 0, j)),
        out_shape=jax.ShapeDtypeStruct((depth, rows, n), F32),
        name="ada_mod",
        compiler_params=_params(("parallel", "parallel")),
    )(cc, ada_w, ada_b.reshape(depth, 1, n))


class _Rows:
    def __init__(self, batch, seq, ctx_len, tm):
        self.batch, self.seq, self.ctx = batch, seq, ctx_len
        self.tm = tm
        self.n_lat = batch * seq // tm
        self.n_ctx = batch * ctx_len // tm
        self.per_batch = seq // tm
        self.rows_lat = batch * seq
        self.rows_all = batch * (seq + ctx_len)

    def mod_row(self, i):
        return jnp.where(i < self.n_lat, i // self.per_batch, self.batch)

    def rope_row(self, i):
        return jnp.where(i < self.n_lat, i % self.per_batch, self.per_batch)


INPROJ_ROW_CHUNK = 256


def _inproj_even_kernel(h_ref, mod_ref, g_ref, w_ref, cos_ref, sin_ref, o_ref, u_ref):
    j = pl.program_id(1)

    @pl.when(j == 0)
    def _():
        u_ref[...] = _norm_mod(h_ref[...], g_ref[...], mod_ref[0:1, :], mod_ref[1:2, :]).astype(BF16)

    w = w_ref[...]
    tm, tn = o_ref.shape
    rc = min(INPROJ_ROW_CHUNK, tm)
    for r in range(tm // rc):
        rs = slice(r * rc, (r + 1) * rc)
        x = _dot(u_ref[rs, :], w)
        cos, sin = cos_ref[rs, :], sin_ref[rs, :]
        for g in range(tn // LANES):
            xg = x[:, g * LANES:(g + 1) * LANES]
            y = xg * cos + pltpu.roll(xg, LANES // 2, 1) * sin
            o_ref[rs, g * LANES:(g + 1) * LANES] = y.astype(o_ref.dtype)


def _inproj_even(h, mods, norm_g, w, rope, rows, a_width):
    d, n = w.shape
    tm = rows.tm
    tn = _pick_tile(512, a_width, n)
    cos, sin = rope
    nq, nqk = a_width // tn, 2 * a_width // tn

    def col_type(j):
        return (j >= nq).astype(jnp.int32) + (j >= nqk).astype(jnp.int32)

    tab = pl.BlockSpec((tm, LANES), lambda i, j: (rows.rope_row(i), col_type(j)))
    return pl.pallas_call(
        _inproj_even_kernel,
        grid=(rows.n_lat + rows.n_ctx, n // tn),
        in_specs=[
            pl.BlockSpec((tm, d), lambda i, j: (i, 0)),
            pl.BlockSpec((None, N_MOD, d), lambda i, j: (rows.mod_row(i), 0, 0)),
            pl.BlockSpec((1, d), lambda i, j: (0, 0)),
            pl.BlockSpec((d, tn), lambda i, j: (0, j)),
            tab, tab,
        ],
        out_specs=pl.BlockSpec((tm, tn), lambda i, j: (i, j)),
        out_shape=jax.ShapeDtypeStruct((rows.rows_all, n), BF16),
        scratch_shapes=[pltpu.VMEM((tm, d), BF16)],
        name="inproj_even",
        compiler_params=_params(("parallel", "arbitrary")),
    )(h, mods, norm_g.reshape(1, d), w, cos, sin)


def _inproj_odd_kernel(h_ref, mod_ref, g_ref, w_ref, wz_ref, o_ref, z_ref, u_ref):
    j = pl.program_id(1)

    @pl.when(j == 0)
    def _():
        u = _norm_mod(h_ref[...], g_ref[...], mod_ref[0:1, :], mod_ref[1:2, :]).astype(BF16)
        u_ref[...] = u
        z_ref[...] = _dot(u, wz_ref[...])

    w = w_ref[...]
    tm = o_ref.shape[0]
    rc = min(INPROJ_ROW_CHUNK, tm)
    for r in range(tm // rc):
        rs = slice(r * rc, (r + 1) * rc)
        o_ref[rs, :] = _dot(u_ref[rs, :], w).astype(o_ref.dtype)


def _inproj_odd(h, mods, norm_g, w, wz, rows):
    d, n = w.shape
    tm = rows.tm
    tn = _pick_tile(512, n)
    return pl.pallas_call(
        _inproj_odd_kernel,
        grid=(rows.n_lat + rows.n_ctx, n // tn),
        in_specs=[
            pl.BlockSpec((tm, d), lambda i, j: (i, 0)),
            pl.BlockSpec((None, N_MOD, d), lambda i, j: (rows.mod_row(i), 0, 0)),
            pl.BlockSpec((1, d), lambda i, j: (0, 0)),
            pl.BlockSpec((d, tn), lambda i, j: (0, j)),
            pl.BlockSpec((d, LANES), lambda i, j: (0, 0)),
        ],
        out_specs=[
            pl.BlockSpec((tm, tn), lambda i, j: (i, j)),
            pl.BlockSpec((tm, LANES), lambda i, j: (i, 0)),
        ],
        out_shape=[
            jax.ShapeDtypeStruct((rows.rows_all, n), BF16),
            jax.ShapeDtypeStruct((rows.rows_all, LANES), F32),
        ],
        scratch_shapes=[pltpu.VMEM((tm, d), BF16)],
        name="inproj_odd",
        compiler_params=_params(("parallel", "arbitrary")),
    )(h, mods, norm_g.reshape(1, d), w, wz)


def _attn_kernel(lam_ref, ql_ref, qc_ref, kl_ref, vl_ref, kc_ref, vc_ref, g_ref, ol_ref, oc_ref,
                 k_s, v_s, s0_s, s1_s, p0_s, p1_s, *, tq, kt, post_scale):
    ctx_len, seq = kc_ref.shape[0], kl_ref.shape[0]
    total = ctx_len + seq
    nkt, nq, ng = total // kt, seq // tq, kt // LANES
    s_bufs, p_bufs = (s0_s, s1_s), (p0_s, p1_s)
    lam = lam_ref[0]
    gain = g_ref[...] * post_scale

    k_s[0:ctx_len, :] = kc_ref[...]
    k_s[ctx_len:total, :] = kl_ref[...]
    v_s[0:ctx_len, 0:LANES] = vc_ref[...]
    v_s[ctx_len:total, 0:LANES] = vl_ref[...]
    v_s[:, LANES:2 * LANES] = jnp.ones((total, LANES), BF16)

    def map_mask(shape, mi):
        lane = lax.broadcasted_iota(jnp.int32, shape, 1)
        return ((lane // (A_HEAD_DIM // 2)) % 2) == mi

    def q_map(t, mi):
        q = ql_ref[pl.ds(pl.multiple_of(t * tq, tq), tq), :]
        return jnp.where(map_mask(q.shape, mi), q, jnp.zeros_like(q))

    def scores(qa, buf, j, m):
        s = _dot_nt(qa, k_s[j * kt:(j + 1) * kt, :])
        s_bufs[buf][:, j * kt:(j + 1) * kt] = s
        for g in range(ng):
            m = jnp.maximum(m, s[:, g * LANES:(g + 1) * LANES])
        return m

    def exps(buf, j, mb):
        x = s_bufs[buf][:, j * kt:(j + 1) * kt]
        ps = [jnp.exp2(x[:, g * LANES:(g + 1) * LANES] - mb).astype(BF16) for g in range(ng)]
        p_bufs[buf][:, j * kt:(j + 1) * kt] = ps[0] if ng == 1 else jnp.concatenate(ps, axis=1)

    def values(buf):
        r = _dot(p_bufs[buf][...], v_s[...])
        return r[:, 0:LANES] / r[:, LANES:2 * LANES]

    def row_max(m):
        return jnp.broadcast_to(jnp.max(m, axis=-1, keepdims=True), m.shape)

    neg = jnp.full((tq, LANES), -jnp.inf, F32)

    qa = q_map(0, 0)
    m = neg
    for j in range(nkt):
        m = scores(qa, 0, j, m)
    mx0 = row_max(m)
    qb = q_map(0, 1)
    m = neg
    for j in range(nkt):
        m = scores(qb, 1, j, m)
        exps(0, j, mx0)

    def tile_body(t, mx1):
        tn = jnp.minimum(t + 1, nq - 1)
        o0 = values(0)
        qa = q_map(tn, 0)
        m = neg
        for j in range(nkt):
            m = scores(qa, 0, j, m)
            exps(1, j, mx1)
        mx0n = row_max(m)
        o1 = values(1)
        qb = q_map(tn, 1)
        m = neg
        for j in range(nkt):
            m = scores(qb, 1, j, m)
            exps(0, j, mx0n)
        o = o0 - lam * o1
        ol_ref[pl.ds(pl.multiple_of(t * tq, tq), tq), :] = (_rms(o) * gain).astype(ol_ref.dtype)
        return row_max(m)

    lax.fori_loop(0, nq, tile_body, row_max(m))

    qc = qc_ref[...]
    outs = []
    for mi in range(2):
        qm = jnp.where(map_mask(qc.shape, mi), qc, jnp.zeros_like(qc))
        s = _dot_nt(qm, kc_ref[...])
        p = jnp.exp2(s - jnp.max(s, axis=-1, keepdims=True))
        outs.append(_dot(p.astype(BF16), vc_ref[...]) / jnp.sum(p, axis=-1, keepdims=True))
    oc_ref[...] = (_rms(outs[0] - lam * outs[1]) * gain).astype(oc_ref.dtype)


def _diff_attn(p, lam, subln_g, post_scale, batch, seq, ctx_len, heads):
    total = seq + ctx_len
    tq = _pick_tile(256, seq)
    kt = _pick_tile(256, total)
    nkt = total // kt
    n_ctx_blk0 = batch * seq // ctx_len
    kern = functools.partial(_attn_kernel, tq=tq, kt=kt, post_scale=post_scale)

    def lat(off):
        return pl.BlockSpec((seq, LANES), lambda b, h: (b, off + h))

    def ctx(off):
        return pl.BlockSpec((ctx_len, LANES), lambda b, h: (n_ctx_blk0 + b, off + h))

    return pl.pallas_call(
        kern,
        grid=(batch, heads),
        in_specs=[
            pl.BlockSpec(memory_space=pltpu.SMEM),
            lat(0), ctx(0), lat(heads), lat(2 * heads), ctx(heads), ctx(2 * heads),
            pl.BlockSpec((1, LANES), lambda b, h: (0, 0)),
        ],
        out_specs=[
            pl.BlockSpec((seq, LANES), lambda b, h: (b, h)),
            pl.BlockSpec((ctx_len, LANES), lambda b, h: (b, h)),
        ],
        out_shape=[
            jax.ShapeDtypeStruct((batch * seq, heads * LANES), BF16),
            jax.ShapeDtypeStruct((batch * ctx_len, heads * LANES), BF16),
        ],
        scratch_shapes=[
            pltpu.VMEM((total, LANES), BF16),
            pltpu.VMEM((total, 2 * LANES), BF16),
            pltpu.VMEM((tq, total), F32),
            pltpu.VMEM((tq, total), F32),
            pltpu.VMEM((tq, total), BF16),
            pltpu.VMEM((tq, total), BF16),
        ],
        name="diff_attn",
        compiler_params=_params(("parallel", "parallel")),
    )(lam, p, p, p, p, p, p, subln_g.reshape(1, LANES))


def _dwconv(x, w, b):
    n = x.shape[0]
    row = lax.broadcasted_iota(jnp.int32, x.shape, 0)
    y = b + x * w[CONV_LEFT:CONV_LEFT + 1, :]
    for j in range(CONV_W):
        off = j - CONV_LEFT
        if off == 0:
            continue
        xs = pltpu.roll(x, (-off) % n, 0)
        ok = jnp.logical_and(row + off >= 0, row + off < n)
        y = y + jnp.where(ok, xs, 0.0) * w[j:j + 1, :]
    return y


def _lru_kernel(recl_ref, recc_ref, gtl_ref, gtc_ref, cw_ref, cb_ref, wa_ref, ba_ref, wx_ref, bx_ref,
                lam_ref, ol_ref, oc_ref, xc_s, a_s, b_s, h_s, *, ctx_len, seq):
    total = ctx_len + seq
    nc = ctx_len // SUBLANES
    nt = total // SUBLANES
    cw = cw_ref[...]
    cb = cb_ref[...]
    xc_s[0:ctx_len, :] = _dwconv(recc_ref[...].astype(F32), cw, cb)
    xc_s[ctx_len:total, :] = _dwconv(recl_ref[...].astype(F32), cw, cb)

    xc = xc_s[...]
    xcb = xc.astype(BF16)
    for d in range(2):
        r = _sigmoid(_dot(xcb, wa_ref[d].astype(BF16)) + ba_ref[d])
        gi = _sigmoid(_dot(xcb, wx_ref[d].astype(BF16)) + bx_ref[d])
        a = jnp.exp((-LRU_C) * r * _softplus(-lam_ref[d]))
        a_s[d] = a
        b_s[d] = jnp.sqrt(1.0 - a * a) * gi * xc

    row = lax.broadcasted_iota(jnp.int32, (SUBLANES, LANES), 0)

    def scan_tile(a, b, reverse):
        s = 1
        while s < SUBLANES:
            sh = (SUBLANES - s) if reverse else s
            a_sh = pltpu.roll(a, sh, 0)
            b_sh = pltpu.roll(b, sh, 0)
            ok = (row < SUBLANES - s) if reverse else (row >= s)
            b = jnp.where(ok, a * b_sh + b, b)
            a = jnp.where(ok, a * a_sh, a)
            s *= 2
        return a, b

    def body(p, carry):
        cf, cbk = carry
        tf = pl.multiple_of(p * SUBLANES, SUBLANES)
        a, b = scan_tile(a_s[0, pl.ds(tf, SUBLANES), :], b_s[0, pl.ds(tf, SUBLANES), :], False)
        hf = a * cf + b
        h_s[0, pl.ds(tf, SUBLANES), :] = hf
        cf = hf[SUBLANES - 1:SUBLANES, :]
        tb = pl.multiple_of(jnp.where(p < nc, nc - 1 - p, nt + nc - 1 - p) * SUBLANES, SUBLANES)
        a, b = scan_tile(a_s[1, pl.ds(tb, SUBLANES), :], b_s[1, pl.ds(tb, SUBLANES), :], True)
        hb = a * cbk + b
        h_s[1, pl.ds(tb, SUBLANES), :] = hb
        cbk = hb[0:1, :]
        return cf, cbk

    zero = jnp.zeros((1, LANES), F32)
    lax.fori_loop(0, nt, body, (zero, zero))

    h = h_s[0] + h_s[1]
    oc_ref[...] = (_gelu_tanh(gtc_ref[...].astype(F32)) * h[0:ctx_len]).astype(oc_ref.dtype)
    ol_ref[...] = (_gelu_tanh(gtl_ref[...].astype(F32)) * h[ctx_len:total]).astype(ol_ref.dtype)


def _rglru(p, conv_w, conv_b, wa, ba, wx, bx, lru_lam, batch, seq, ctx_len, a_width, b_width):
    nb = b_width // B_BLOCK_DIM
    gt_off = 3 * a_width // LANES
    rec_off = (3 * a_width + b_width) // LANES
    n_ctx_blk0 = batch * seq // ctx_len
    total = seq + ctx_len
    kern = functools.partial(_lru_kernel, ctx_len=ctx_len, seq=seq)
    vec = pl.BlockSpec((2, 1, LANES), lambda b, n: (0, 0, n))
    mat = pl.BlockSpec((2, None, B_BLOCK_DIM, B_BLOCK_DIM), lambda b, n: (0, n, 0, 0))
    return pl.pallas_call(
        kern,
        grid=(batch, nb),
        in_specs=[
            pl.BlockSpec((seq, LANES), lambda b, n: (b, rec_off + n)),
            pl.BlockSpec((ctx_len, LANES), lambda b, n: (n_ctx_blk0 + b, rec_off + n)),
            pl.BlockSpec((seq, LANES), lambda b, n: (b, gt_off + n)),
            pl.BlockSpec((ctx_len, LANES), lambda b, n: (n_ctx_blk0 + b, gt_off + n)),
            pl.BlockSpec((CONV_W, LANES), lambda b, n: (0, n)),
            pl.BlockSpec((1, LANES), lambda b, n: (0, n)),
            mat, vec, mat, vec, vec,
        ],
        out_specs=[
            pl.BlockSpec((seq, LANES), lambda b, n: (b, n)),
            pl.BlockSpec((ctx_len, LANES), lambda b, n: (b, n)),
        ],
        out_shape=[
            jax.ShapeDtypeStruct((batch * seq, b_width), BF16),
            jax.ShapeDtypeStruct((batch * ctx_len, b_width), BF16),
        ],
        scratch_shapes=[
            pltpu.VMEM((total, LANES), F32),
            pltpu.VMEM((2, total, LANES), F32),
            pltpu.VMEM((2, total, LANES), F32),
            pltpu.VMEM((2, total, LANES), F32),
        ],
        name="rglru",
        compiler_params=_params(("parallel", "parallel")),
    )(p, p, p, p, conv_w, conv_b.reshape(1, b_width), wa, ba.reshape(2, 1, b_width),
      wx, bx.reshape(2, 1, b_width), lru_lam.reshape(2, 1, b_width))


def _gla_kernel(ql_ref, kl_ref, vl_ref, rl_ref, zl_ref, qc_ref, kc_ref, vc_ref, rc_ref, zc_ref,
                w2_ref, gb_ref, g_ref, al_ref, ac_ref, st_s, ol_s, oc_s, *, ctx_len, seq):
    c = C_CHUNK
    ri = lax.broadcasted_iota(jnp.int32, (c, c), 0)
    ci = lax.broadcasted_iota(jnp.int32, (c, c), 1)
    keep = (ri >= ci, ri <= ci)
    tri = tuple(jnp.where(k, 1.0, 0.0).astype(BF16) for k in keep)
    end_row = (c - 1, 0)

    def chunk(q_ref, k_ref, v_ref, z_ref, o_ref, t0, d, accumulate):
        rows = pl.ds(t0, c)
        q = q_ref[rows, :].astype(F32)
        k = k_ref[rows, :].astype(F32)
        v = v_ref[rows, :]
        x = _dot(z_ref[rows, :].astype(BF16), w2_ref[d]) + gb_ref[d]
        g = _log_sigmoid(x) * (1.0 / C_GATE_TAU)
        g_hi = g.astype(BF16)
        g_lo = (g - g_hi.astype(F32)).astype(BF16)
        bcum = _dot(tri[d], g_hi) + _dot(tri[d], g_lo)
        b_end = bcum[end_row[d]:end_row[d] + 1, :]
        qe = (q * jnp.exp(bcum)).astype(BF16)
        ke = (k * jnp.exp(-bcum)).astype(BF16)
        att = jnp.where(keep[d], _dot_nt(qe, ke), 0.0).astype(BF16)
        st = st_s[d]
        o = _dot_nt(qe, st.astype(BF16)) + _dot(att, v)
        kd = (k * jnp.exp(b_end - bcum)).astype(BF16)
        st_s[d] = st * jnp.exp(b_end) + _dot_tn(v, kd)
        if accumulate:
            o_ref[rows, :] += o
        else:
            o_ref[rows, :] = o

    def sweep(q_ref, k_ref, v_ref, z_ref, o_ref, n):
        half = n // 2

        def run(lo, accumulate):
            def body(i, carry):
                chunk(q_ref, k_ref, v_ref, z_ref, o_ref, pl.multiple_of(i * c, c), 0, accumulate)
                chunk(q_ref, k_ref, v_ref, z_ref, o_ref, pl.multiple_of((n - 1 - i) * c, c), 1, accumulate)
                return carry
            lax.fori_loop(lo, lo + half, body, 0)

        run(0, False)
        run(half, True)

    st_s[...] = jnp.zeros_like(st_s)
    sweep(qc_ref, kc_ref, vc_ref, zc_ref, oc_s, ctx_len // c)
    sweep(ql_ref, kl_ref, vl_ref, zl_ref, ol_s, seq // c)

    g = g_ref[...]
    ac_ref[...] = (_rms(oc_s[...]) * g * _silu(rc_ref[...].astype(F32))).astype(ac_ref.dtype)
    al_ref[...] = (_rms(ol_s[...]) * g * _silu(rl_ref[...].astype(F32))).astype(al_ref.dtype)


def _gla(p, z, w2p, gate_b, norm_g, batch, seq, ctx_len, heads):
    d_model = heads * C_VAL_DIM
    key_w = heads * C_KEY_DIM
    n_ctx_blk0 = batch * seq // ctx_len
    kern = functools.partial(_gla_kernel, ctx_len=ctx_len, seq=seq)

    def lat(width, off):
        return pl.BlockSpec((seq, width), lambda b, h: (b, off + h))

    def ctx(width, off):
        return pl.BlockSpec((ctx_len, width), lambda b, h: (n_ctx_blk0 + b, off + h))

    specs = [
        lat(C_KEY_DIM, 0), lat(C_KEY_DIM, heads), lat(C_VAL_DIM, heads), lat(C_VAL_DIM, 2 * heads),
        pl.BlockSpec((seq, LANES), lambda b, h: (b, 0)),
        ctx(C_KEY_DIM, 0), ctx(C_KEY_DIM, heads), ctx(C_VAL_DIM, heads), ctx(C_VAL_DIM, 2 * heads),
        pl.BlockSpec((ctx_len, LANES), lambda b, h: (n_ctx_blk0 + b, 0)),
    ]
    specs += [
        pl.BlockSpec((2, LANES, C_KEY_DIM), lambda b, h: (0, 0, h)),
        pl.BlockSpec((2, 1, C_KEY_DIM), lambda b, h: (0, 0, h)),
        pl.BlockSpec((1, C_VAL_DIM), lambda b, h: (0, 0)),
    ]
    return pl.pallas_call(
        kern,
        grid=(batch, heads),
        in_specs=specs,
        out_specs=[
            pl.BlockSpec((seq, C_VAL_DIM), lambda b, h: (b, h)),
            pl.BlockSpec((ctx_len, C_VAL_DIM), lambda b, h: (b, h)),
        ],
        out_shape=[
            jax.ShapeDtypeStruct((batch * seq, d_model), BF16),
            jax.ShapeDtypeStruct((batch * ctx_len, d_model), BF16),
        ],
        scratch_shapes=[
            pltpu.VMEM((2, C_VAL_DIM, C_KEY_DIM), F32),
            pltpu.VMEM((seq, C_VAL_DIM), F32),
            pltpu.VMEM((ctx_len, C_VAL_DIM), F32),
        ],
        name="gla",
        compiler_params=_params(("parallel", "parallel")),
    )(p, p, p, p, z, p, p, p, p, z, w2p, gate_b.reshape(2, 1, key_w), norm_g.reshape(1, C_VAL_DIM))


def _outproj_kernel(al0_ref, ac0_ref, al1_ref, ac1_ref, w0_ref, w1_ref, h_ref, mod_ref, o_ref, *, n_lat):
    is_lat = pl.program_id(0) < n_lat
    a0 = jnp.where(is_lat, al0_ref[...], ac0_ref[...])
    a1 = jnp.where(is_lat, al1_ref[...], ac1_ref[...])
    acc = _dot(a0, w0_ref[...]) + _dot(a1, w1_ref[...])
    o_ref[...] = h_ref[...] + mod_ref[2:3, :] * acc


def _outproj(srcs, w, h, mods, rows, n_tiles):
    d = w.shape[1]
    kh = w.shape[0] // 2
    tm = rows.tm
    n_lat = rows.n_lat
    specs, args = [], []
    for lat_arr, lat_col, ctx_arr, ctx_row0, ctx_col in srcs:
        specs.append(pl.BlockSpec((tm, kh), lambda i, c=lat_col: (jnp.minimum(i, n_lat - 1), c)))
        specs.append(pl.BlockSpec((tm, kh), lambda i, r=ctx_row0, c=ctx_col: (r + jnp.maximum(i - n_lat, 0), c)))
        args += [lat_arr, ctx_arr]
    kern = functools.partial(_outproj_kernel, n_lat=n_lat)
    return pl.pallas_call(
        kern,
        grid=(n_tiles,),
        in_specs=specs + [
            pl.BlockSpec((kh, d), lambda i: (0, 0)),
            pl.BlockSpec((kh, d), lambda i: (1, 0)),
            pl.BlockSpec((tm, d), lambda i: (i, 0)),
            pl.BlockSpec((None, N_MOD, d), lambda i: (rows.mod_row(i), 0, 0)),
        ],
        out_specs=pl.BlockSpec((tm, d), lambda i: (i, 0)),
        out_shape=jax.ShapeDtypeStruct((n_tiles * tm, d), F32),
        name="outproj",
        compiler_params=_params(("parallel",)),
    )(*args, w, w, h, mods)


def _mlp_kernel(h_ref, mod_ref, g_ref, w1_ref, w2_ref, fg_ref, o_ref, u_ref, *, nf, final_norm):
    j = pl.program_id(1)

    @pl.when(j == 0)
    def _():
        u_ref[...] = _norm_mod(h_ref[...], g_ref[...], mod_ref[3:4, :], mod_ref[4:5, :]).astype(BF16)
        o_ref[...] = jnp.zeros_like(o_ref)

    t = jnp.maximum(_dot(u_ref[...], w1_ref[...]), 0.0)
    o_ref[...] += _dot((t * t).astype(BF16), w2_ref[...])

    @pl.when(j == nf - 1)
    def _():
        y = h_ref[...] + mod_ref[5:6, :] * o_ref[...]
        if final_norm:
            y = _rms(y) * fg_ref[...]
        o_ref[...] = y


def _mlp(h, mods, norm_g, w1, w2, final_g, rows, n_tiles, tf, final_norm):
    d, ff = w1.shape
    nf = ff // tf
    tm = rows.tm
    kern = functools.partial(_mlp_kernel, nf=nf, final_norm=final_norm)
    return pl.pallas_call(
        kern,
        grid=(n_tiles, nf),
        in_specs=[
            pl.BlockSpec((tm, d), lambda i, j: (i, 0)),
            pl.BlockSpec((None, N_MOD, d), lambda i, j: (rows.mod_row(i), 0, 0)),
            pl.BlockSpec((1, d), lambda i, j: (0, 0)),
            pl.BlockSpec((d, tf), lambda i, j: (0, j)),
            pl.BlockSpec((tf, d), lambda i, j: (j, 0)),
            pl.BlockSpec((1, d), lambda i, j: (0, 0)),
        ],
        out_specs=pl.BlockSpec((tm, d), lambda i, j: (i, 0)),
        out_shape=jax.ShapeDtypeStruct((n_tiles * tm, d), F32),
        scratch_shapes=[pltpu.VMEM((tm, d), BF16)],
        name="mlp",
        compiler_params=_params(("parallel", "arbitrary")),
    )(h, mods, norm_g.reshape(1, d), w1, w2, final_g.reshape(1, d))


def _rope_tables(seq, pad_rows):
    inv = ROPE_BASE ** (-jnp.arange(ROPE_FREQS, dtype=F32) / ROPE_FREQS)
    t = jnp.arange(seq, dtype=jnp.int32)
    pos_r = (t // GRID_W).astype(F32)[:, None]
    pos_c = (t % GRID_W).astype(F32)[:, None]
    lane = jnp.arange(LANES, dtype=jnp.int32)[None, :]
    second = (lane // A_HEAD_DIM) == 1
    is_col = ((lane % (2 * ROPE_FREQS)) // ROPE_FREQS) == 1
    ang = jnp.where(is_col, pos_c, pos_r) * inv[lane % ROPE_FREQS]
    cos = jnp.cos(ang)
    sin = jnp.where(second, jnp.sin(ang), -jnp.sin(ang))
    qs = A_HEAD_DIM ** -0.5 * math.log2(math.e)
    ones = jnp.ones((pad_rows, LANES), F32)
    zeros = jnp.zeros((pad_rows, LANES), F32)
    one_all = jnp.ones((seq + pad_rows, LANES), F32)
    zero_all = jnp.zeros((seq + pad_rows, LANES), F32)

    def table(rot, pad):
        body = jnp.concatenate([rot, pad])
        return jnp.concatenate([qs * body, body, one_all if pad is ones else zero_all], axis=1)

    return table(cos, ones), table(sin, zeros)


def _permute_qk_columns(w, a_width):
    d = w.shape[0]
    qk = w[:, :2 * a_width].reshape(d, -1, 2, 2, 2, ROPE_FREQS).transpose(0, 1, 4, 2, 3, 5)
    return jnp.concatenate([qk.reshape(d, 2 * a_width), w[:, 2 * a_width:]], axis=1)


def kernel(x, c, ctx, c_ctx, ada_w, ada_b, norm1_g, norm2_g, mlp_w1, mlp_w2, ev_w_in, ev_w_out, ev_lambda_q1, ev_lambda_k1, ev_lambda_q2, ev_lambda_k2, ev_subln_g, ev_conv_w, ev_conv_b, ev_lru_wa, ev_lru_ba, ev_lru_wx, ev_lru_bx, ev_lru_lam, od_w_in, od_w_out, od_gate_w2, od_gate_b, od_norm_g, final_g):
    batch, seq, d = x.shape
    ctx_len = ctx.shape[1]
    depth = ada_w.shape[0]
    a_width = d // 2
    b_width = d - a_width
    a_heads = a_width // (2 * A_HEAD_DIM)
    c_heads = d // C_VAL_DIM
    key_w = c_heads * C_KEY_DIM

    tm = _pick_tile(1024, seq, batch * ctx_len)
    rows = _Rows(batch, seq, ctx_len, tm)
    rows_out = _Rows(batch, seq, ctx_len, _pick_tile(512, seq, batch * ctx_len))
    tf = _pick_tile(512, mlp_w1.shape[2])

    pad = (-(batch + 1)) % SUBLANES
    cc = jnp.concatenate([c, c_ctx[None, :], jnp.zeros((pad, d), F32)], axis=0)
    mods_all = _ada_mod(cc, ada_w, ada_b).reshape(depth, batch + 1 + pad, N_MOD, d)

    rope = _rope_tables(seq, tm)
    h = jnp.concatenate([x.reshape(batch * seq, d), ctx.reshape(batch * ctx_len, d)], axis=0)

    for i in range(depth):
        last = i == depth - 1
        mods = mods_all[i]
        j = i // 2
        if i % 2 == 0:
            lam_init = 0.8 - 0.6 * math.exp(-0.3 * i)
            lam = (jnp.exp(jnp.sum(ev_lambda_q1[j] * ev_lambda_k1[j]))
                   - jnp.exp(jnp.sum(ev_lambda_q2[j] * ev_lambda_k2[j])) + lam_init).reshape(1).astype(F32)
            w_in = _permute_qk_columns(ev_w_in[j], a_width).astype(BF16)
            p = _inproj_even(h, mods, norm1_g[i], w_in, rope, rows, a_width)
            ya_lat, ya_ctx = _diff_attn(p, lam, ev_subln_g[j], 1.0 - lam_init, batch, seq, ctx_len, a_heads)
            yb_lat, yb_ctx = _rglru(p, ev_conv_w[j], ev_conv_b[j], ev_lru_wa[j], ev_lru_ba[j],
                                    ev_lru_wx[j], ev_lru_bx[j], ev_lru_lam[j],
                                    batch, seq, ctx_len, a_width, b_width)
            srcs = [(ya_lat, 0, ya_ctx, 0, 0), (yb_lat, 0, yb_ctx, 0, 0)]
            w_out = ev_w_out[j]
        else:
            w_in = od_w_in[j]
            n_main = 2 * key_w + 2 * d
            col_scale = jnp.where(jnp.arange(n_main) < key_w, C_KEY_DIM ** -0.5, 1.0).astype(F32)
            w_main = (w_in[:, :n_main] * col_scale[None, :]).astype(BF16)
            wz = jnp.pad(w_in[:, n_main:], ((0, 0), (0, LANES - 2 * C_GATE_RANK))).astype(BF16)
            p, z = _inproj_odd(h, mods, norm1_g[i], w_main, wz, rows)
            w2p = jnp.zeros((2, LANES, key_w), F32)
            for dd in range(2):
                w2p = w2p.at[dd, dd * C_GATE_RANK:(dd + 1) * C_GATE_RANK, :].set(od_gate_w2[j, dd])
            a_lat, a_ctx = _gla(p, z, w2p.astype(BF16), od_gate_b[j], od_norm_g[j], batch, seq, ctx_len, c_heads)
            srcs = [(a_lat, 0, a_ctx, 0, 0), (a_lat, 1, a_ctx, 0, 1)]
            w_out = od_w_out[j]

        n_out = rows_out.n_lat if last else rows_out.n_lat + rows_out.n_ctx
        h = _outproj(srcs, w_out.astype(BF16), h, mods, rows_out, n_out)
        n_mlp = rows.n_lat if last else rows.n_lat + rows.n_ctx
        h = _mlp(h, mods, norm2_g[i], mlp_w1[i].astype(BF16), mlp_w2[i].astype(BF16), final_g,
                 rows, n_mlp, tf, last)

    return h.reshape(batch, seq, d)
```

```python
import functools
import math

import jax
import jax.numpy as jnp
from jax import lax
from jax.experimental import pallas as pl
from jax.experimental.pallas import tpu as pltpu

F32 = jnp.float32
BF16 = jnp.bfloat16

NORM_EPS = 1e-6
N_MOD = 6
GRID_W = 64
A_HEAD_DIM = 64
ROPE_FREQS = A_HEAD_DIM // 4
ROPE_BASE = 10000.0
B_BLOCK_DIM = 128
CONV_W = 4
CONV_LEFT = 2
LRU_C = 8.0
C_KEY_DIM = 256
C_VAL_DIM = 512
C_GATE_RANK = 16
C_GATE_TAU = 16.0
C_CHUNK = 64

LANES = 128
SUBLANES = 8
VMEM_LIMIT = 56 * 1024 * 1024


def _params(sem, vmem=VMEM_LIMIT):
    return pltpu.CompilerParams(dimension_semantics=sem, vmem_limit_bytes=vmem)


def _dot(a, b):
    return jnp.dot(a, b, preferred_element_type=F32)


def _dot_nt(a, b):
    return lax.dot_general(a, b, (((1,), (1,)), ((), ())), preferred_element_type=F32)


def _dot_tn(a, b):
    return lax.dot_general(a, b, (((0,), (0,)), ((), ())), preferred_element_type=F32)


def _sigmoid(x):
    return 1.0 / (1.0 + jnp.exp(-x))


def _silu(x):
    return x * _sigmoid(x)


def _softplus(x):
    return jnp.maximum(x, 0.0) + jnp.log(1.0 + jnp.exp(-jnp.abs(x)))


def _log_sigmoid(x):
    return -_softplus(-x)


def _gelu_tanh(x):
    return 0.5 * x * (1.0 + jnp.tanh(math.sqrt(2.0 / math.pi) * (x + 0.044715 * (x * x * x))))


def _rms(x):
    return x * lax.rsqrt(jnp.mean(x * x, axis=-1, keepdims=True) + NORM_EPS)


def _norm_mod(x, g, shift, scale):
    return (_rms(x) * g) * (1.0 + scale) + shift


def _pick_tile(limit, *dims):
    t = limit
    while t > SUBLANES and any(d % t for d in dims):
        t //= 2
    assert all(d % t == 0 for d in dims), (limit, dims)
    return t


def _ada_kernel(c_ref, w_ref, b_ref, o_ref):
    s = _silu(c_ref[...]).astype(BF16)
    o_ref[...] = _dot(s, w_ref[...].astype(BF16)) + b_ref[...]


def _ada_mod(cc, ada_w, ada_b):
    depth, d, n = ada_w.shape
    rows = cc.shape[0]
    tn = _pick_tile(1024, n)
    return pl.pallas_call(
        _ada_kernel,
        grid=(depth, n // tn),
        in_specs=[
            pl.BlockSpec((rows, d), lambda l, j: (0, 0)),
            pl.BlockSpec((None, d, tn), lambda l, j: (l, 0, j)),
            pl.BlockSpec((None, 1, tn), lambda l, j: (l, 0, j)),
        ],
        out_specs=pl.BlockSpec((None, rows, tn), lambda l, j: (l, 0, j)),
        out_shape=jax.ShapeDtypeStruct((depth, rows, n), F32),
        name="ada_mod",
        compiler_params=_params(("parallel", "parallel")),
    )(cc, ada_w, ada_b.reshape(depth, 1, n))


class _Rows:
    def __init__(self, batch, seq, ctx_len, tm):
        self.batch, self.seq, self.ctx = batch, seq, ctx_len
        self.tm = tm
        self.n_lat = batch * seq // tm
        self.n_ctx = batch * ctx_len // tm
        self.per_batch = seq // tm
        self.rows_lat = batch * seq
        self.rows_all = batch * (seq + ctx_len)

    def mod_row(self, i):
        return jnp.where(i < self.n_lat, i // self.per_batch, self.batch)

    def rope_row(self, i):
        return jnp.where(i < self.n_lat, i % self.per_batch, self.per_batch)


INPROJ_ROW_CHUNK = 256


def _inproj_even_kernel(h_ref, mod_ref, g_ref, w_ref, cos_ref, sin_ref, o_ref, u_ref):
    j = pl.program_id(1)

    @pl.when(j == 0)
    def _():
        u_ref[...] = _norm_mod(h_ref[...], g_ref[...], mod_ref[0:1, :], mod_ref[1:2, :]).astype(BF16)

    w = w_ref[...]
    tm, tn = o_ref.shape
    rc = min(INPROJ_ROW_CHUNK, tm)
    for r in range(tm // rc):
        rs = slice(r * rc, (r + 1) * rc)
        x = _dot(u_ref[rs, :], w)
        cos, sin = cos_ref[rs, :], sin_ref[rs, :]
        for g in range(tn // LANES):
            xg = x[:, g * LANES:(g + 1) * LANES]
            y = xg * cos + pltpu.roll(xg, LANES // 2, 1) * sin
            o_ref[rs, g * LANES:(g + 1) * LANES] = y.astype(o_ref.dtype)


def _inproj_even(h, mods, norm_g, w, rope, rows, a_width):
    d, n = w.shape
    tm = rows.tm
    tn = _pick_tile(1024, a_width, n)
    cos, sin = rope
    nq, nqk = a_width // tn, 2 * a_width // tn

    def col_type(j):
        return (j >= nq).astype(jnp.int32) + (j >= nqk).astype(jnp.int32)

    tab = pl.BlockSpec((tm, LANES), lambda i, j: (rows.rope_row(i), col_type(j)))
    return pl.pallas_call(
        _inproj_even_kernel,
        grid=(rows.n_lat + rows.n_ctx, n // tn),
        in_specs=[
            pl.BlockSpec((tm, d), lambda i, j: (i, 0)),
            pl.BlockSpec((None, N_MOD, d), lambda i, j: (rows.mod_row(i), 0, 0)),
            pl.BlockSpec((1, d), lambda i, j: (0, 0)),
            pl.BlockSpec((d, tn), lambda i, j: (0, j)),
            tab, tab,
        ],
        out_specs=pl.BlockSpec((tm, tn), lambda i, j: (i, j)),
        out_shape=jax.ShapeDtypeStruct((rows.rows_all, n), BF16),
        scratch_shapes=[pltpu.VMEM((tm, d), BF16)],
        name="inproj_even",
        compiler_params=_params(("parallel", "arbitrary")),
    )(h, mods, norm_g.reshape(1, d), w, cos, sin)


def _inproj_odd_kernel(h_ref, mod_ref, g_ref, w_ref, wz_ref, o_ref, z_ref, u_ref):
    j = pl.program_id(1)

    @pl.when(j == 0)
    def _():
        u = _norm_mod(h_ref[...], g_ref[...], mod_ref[0:1, :], mod_ref[1:2, :]).astype(BF16)
        u_ref[...] = u
        z_ref[...] = _dot(u, wz_ref[...])

    w = w_ref[...]
    tm = o_ref.shape[0]
    rc = min(INPROJ_ROW_CHUNK, tm)
    for r in range(tm // rc):
        rs = slice(r * rc, (r + 1) * rc)
        o_ref[rs, :] = _dot(u_ref[rs, :], w).astype(o_ref.dtype)


def _inproj_odd(h, mods, norm_g, w, wz, rows):
    d, n = w.shape
    tm = rows.tm
    tn = _pick_tile(1024, n)
    return pl.pallas_call(
        _inproj_odd_kernel,
        grid=(rows.n_lat + rows.n_ctx, n // tn),
        in_specs=[
            pl.BlockSpec((tm, d), lambda i, j: (i, 0)),
            pl.BlockSpec((None, N_MOD, d), lambda i, j: (rows.mod_row(i), 0, 0)),
            pl.BlockSpec((1, d), lambda i, j: (0, 0)),
            pl.BlockSpec((d, tn), lambda i, j: (0, j)),
            pl.BlockSpec((d, LANES), lambda i, j: (0, 0)),
        ],
        out_specs=[
            pl.BlockSpec((tm, tn), lambda i, j: (i, j)),
            pl.BlockSpec((tm, LANES), lambda i, j: (i, 0)),
        ],
        out_shape=[
            jax.ShapeDtypeStruct((rows.rows_all, n), BF16),
            jax.ShapeDtypeStruct((rows.rows_all, LANES), F32),
        ],
        scratch_shapes=[pltpu.VMEM((tm, d), BF16)],
        name="inproj_odd",
        compiler_params=_params(("parallel", "arbitrary")),
    )(h, mods, norm_g.reshape(1, d), w, wz)


def _attn_kernel(lam_ref, ql_ref, qc_ref, kl_ref, vl_ref, kc_ref, vc_ref, g_ref, ol_ref, oc_ref,
                 k_s, v_s, s0_s, s1_s, p0_s, p1_s, *, tq, kt, post_scale):
    ctx_len, seq = kc_ref.shape[0], kl_ref.shape[0]
    total = ctx_len + seq
    nkt, nq, ng = total // kt, seq // tq, kt // LANES
    s_bufs, p_bufs = (s0_s, s1_s), (p0_s, p1_s)
    lam = lam_ref[0]
    gain = g_ref[...] * post_scale

    k_s[0:ctx_len, :] = kc_ref[...]
    k_s[ctx_len:total, :] = kl_ref[...]
    v_s[0:ctx_len, 0:LANES] = vc_ref[...]
    v_s[ctx_len:total, 0:LANES] = vl_ref[...]
    v_s[:, LANES:2 * LANES] = jnp.ones((total, LANES), BF16)

    def map_mask(shape, mi):
        lane = lax.broadcasted_iota(jnp.int32, shape, 1)
        return ((lane // (A_HEAD_DIM // 2)) % 2) == mi

    def q_map(t, mi):
        q = ql_ref[pl.ds(pl.multiple_of(t * tq, tq), tq), :]
        return jnp.where(map_mask(q.shape, mi), q, jnp.zeros_like(q))

    def scores(qa, buf, j, m):
        s = _dot_nt(qa, k_s[j * kt:(j + 1) * kt, :])
        s_bufs[buf][:, j * kt:(j + 1) * kt] = s
        for g in range(ng):
            m = jnp.maximum(m, s[:, g * LANES:(g + 1) * LANES])
        return m

    def exps(buf, j, mb):
        x = s_bufs[buf][:, j * kt:(j + 1) * kt]
        ps = [jnp.exp2(x[:, g * LANES:(g + 1) * LANES] - mb).astype(BF16) for g in range(ng)]
        p_bufs[buf][:, j * kt:(j + 1) * kt] = ps[0] if ng == 1 else jnp.concatenate(ps, axis=1)

    def values(buf):
        r = _dot(p_bufs[buf][...], v_s[...])
        return r[:, 0:LANES] / r[:, LANES:2 * LANES]

    def row_max(m):
        return jnp.broadcast_to(jnp.max(m, axis=-1, keepdims=True), m.shape)

    neg = jnp.full((tq, LANES), -jnp.inf, F32)

    qa = q_map(0, 0)
    m = neg
    for j in range(nkt):
        m = scores(qa, 0, j, m)
    mx0 = row_max(m)
    qb = q_map(0, 1)
    m = neg
    for j in range(nkt):
        m = scores(qb, 1, j, m)
        exps(0, j, mx0)

    def tile_body(t, mx1):
        tn = jnp.minimum(t + 1, nq - 1)
        o0 = values(0)
        qa = q_map(tn, 0)
        m = neg
        for j in range(nkt):
            m = scores(qa, 0, j, m)
            exps(1, j, mx1)
        mx0n = row_max(m)
        o1 = values(1)
        qb = q_map(tn, 1)
        m = neg
        for j in range(nkt):
            m = scores(qb, 1, j, m)
            exps(0, j, mx0n)
        o = o0 - lam * o1
        ol_ref[pl.ds(pl.multiple_of(t * tq, tq), tq), :] = (_rms(o) * gain).astype(ol_ref.dtype)
        return row_max(m)

    lax.fori_loop(0, nq, tile_body, row_max(m))

    qc = qc_ref[...]
    outs = []
    for mi in range(2):
        qm = jnp.where(map_mask(qc.shape, mi), qc, jnp.zeros_like(qc))
        s = _dot_nt(qm, kc_ref[...])
        p = jnp.exp2(s - jnp.max(s, axis=-1, keepdims=True))
        outs.append(_dot(p.astype(BF16), vc_ref[...]) / jnp.sum(p, axis=-1, keepdims=True))
    oc_ref[...] = (_rms(outs[0] - lam * outs[1]) * gain).astype(oc_ref.dtype)


def _diff_attn(p, lam, subln_g, post_scale, batch, seq, ctx_len, heads):
    total = seq + ctx_len
    tq = _pick_tile(256, seq)
    kt = _pick_tile(256, total)
    nkt = total // kt
    n_ctx_blk0 = batch * seq // ctx_len
    kern = functools.partial(_attn_kernel, tq=tq, kt=kt, post_scale=post_scale)

    def lat(off):
        return pl.BlockSpec((seq, LANES), lambda b, h: (b, off + h))

    def ctx(off):
        return pl.BlockSpec((ctx_len, LANES), lambda b, h: (n_ctx_blk0 + b, off + h))

    return pl.pallas_call(
        kern,
        grid=(batch, heads),
        in_specs=[
            pl.BlockSpec(memory_space=pltpu.SMEM),
            lat(0), ctx(0), lat(heads), lat(2 * heads), ctx(heads), ctx(2 * heads),
            pl.BlockSpec((1, LANES), lambda b, h: (0, 0)),
        ],
        out_specs=[
            pl.BlockSpec((seq, LANES), lambda b, h: (b, h)),
            pl.BlockSpec((ctx_len, LANES), lambda b, h: (b, h)),
        ],
        out_shape=[
            jax.ShapeDtypeStruct((batch * seq, heads * LANES), BF16),
            jax.ShapeDtypeStruct((batch * ctx_len, heads * LANES), BF16),
        ],
        scratch_shapes=[
            pltpu.VMEM((total, LANES), BF16),
            pltpu.VMEM((total, 2 * LANES), BF16),
            pltpu.VMEM((tq, total), F32),
            pltpu.VMEM((tq, total), F32),
            pltpu.VMEM((tq, total), BF16),
            pltpu.VMEM((tq, total), BF16),
        ],
        name="diff_attn",
        compiler_params=_params(("parallel", "parallel")),
    )(lam, p, p, p, p, p, p, subln_g.reshape(1, LANES))


def _dwconv(x, w, b):
    n = x.shape[0]
    row = lax.broadcasted_iota(jnp.int32, x.shape, 0)
    y = b + x * w[CONV_LEFT:CONV_LEFT + 1, :]
    for j in range(CONV_W):
        off = j - CONV_LEFT
        if off == 0:
            continue
        xs = pltpu.roll(x, (-off) % n, 0)
        ok = jnp.logical_and(row + off >= 0, row + off < n)
        y = y + jnp.where(ok, xs, 0.0) * w[j:j + 1, :]
    return y


def _lru_kernel(recl_ref, recc_ref, gtl_ref, gtc_ref, cw_ref, cb_ref, wa_ref, ba_ref, wx_ref, bx_ref,
                lam_ref, ol_ref, oc_ref, xc_s, a_s, b_s, h_s, *, ctx_len, seq):
    total = ctx_len + seq
    nc = ctx_len // SUBLANES
    nt = total // SUBLANES
    cw = cw_ref[...]
    cb = cb_ref[...]
    xc_s[0:ctx_len, :] = _dwconv(recc_ref[...].astype(F32), cw, cb)
    xc_s[ctx_len:total, :] = _dwconv(recl_ref[...].astype(F32), cw, cb)

    xc = xc_s[...]
    xcb = xc.astype(BF16)
    for d in range(2):
        r = _sigmoid(_dot(xcb, wa_ref[d].astype(BF16)) + ba_ref[d])
        gi = _sigmoid(_dot(xcb, wx_ref[d].astype(BF16)) + bx_ref[d])
        a = jnp.exp((-LRU_C) * r * _softplus(-lam_ref[d]))
        a_s[d] = a
        b_s[d] = jnp.sqrt(1.0 - a * a) * gi * xc

    row = lax.broadcasted_iota(jnp.int32, (SUBLANES, LANES), 0)

    def scan_tile(a, b, reverse):
        s = 1
        while s < SUBLANES:
            sh = (SUBLANES - s) if reverse else s
            a_sh = pltpu.roll(a, sh, 0)
            b_sh = pltpu.roll(b, sh, 0)
            ok = (row < SUBLANES - s) if reverse else (row >= s)
            b = jnp.where(ok, a * b_sh + b, b)
            a = jnp.where(ok, a * a_sh, a)
            s *= 2
        return a, b

    def body(p, carry):
        cf, cbk = carry
        tf = pl.multiple_of(p * SUBLANES, SUBLANES)
        a, b = scan_tile(a_s[0, pl.ds(tf, SUBLANES), :], b_s[0, pl.ds(tf, SUBLANES), :], False)
        hf = a * cf + b
        h_s[0, pl.ds(tf, SUBLANES), :] = hf
        cf = hf[SUBLANES - 1:SUBLANES, :]
        tb = pl.multiple_of(jnp.where(p < nc, nc - 1 - p, nt + nc - 1 - p) * SUBLANES, SUBLANES)
        a, b = scan_tile(a_s[1, pl.ds(tb, SUBLANES), :], b_s[1, pl.ds(tb, SUBLANES), :], True)
        hb = a * cbk + b
        h_s[1, pl.ds(tb, SUBLANES), :] = hb
        cbk = hb[0:1, :]
        return cf, cbk

    zero = jnp.zeros((1, LANES), F32)
    lax.fori_loop(0, nt, body, (zero, zero))

    h = h_s[0] + h_s[1]
    oc_ref[...] = (_gelu_tanh(gtc_ref[...].astype(F32)) * h[0:ctx_len]).astype(oc_ref.dtype)
    ol_ref[...] = (_gelu_tanh(gtl_ref[...].astype(F32)) * h[ctx_len:total]).astype(ol_ref.dtype)


def _rglru(p, conv_w, conv_b, wa, ba, wx, bx, lru_lam, batch, seq, ctx_len, a_width, b_width):
    nb = b_width // B_BLOCK_DIM
    gt_off = 3 * a_width // LANES
    rec_off = (3 * a_width + b_width) // LANES
    n_ctx_blk0 = batch * seq // ctx_len
    total = seq + ctx_len
    kern = functools.partial(_lru_kernel, ctx_len=ctx_len, seq=seq)
    vec = pl.BlockSpec((2, 1, LANES), lambda b, n: (0, 0, n))
    mat = pl.BlockSpec((2, None, B_BLOCK_DIM, B_BLOCK_DIM), lambda b, n: (0, n, 0, 0))
    return pl.pallas_call(
        kern,
        grid=(batch, nb),
        in_specs=[
            pl.BlockSpec((seq, LANES), lambda b, n: (b, rec_off + n)),
            pl.BlockSpec((ctx_len, LANES), lambda b, n: (n_ctx_blk0 + b, rec_off + n)),
            pl.BlockSpec((seq, LANES), lambda b, n: (b, gt_off + n)),
            pl.BlockSpec((ctx_len, LANES), lambda b, n: (n_ctx_blk0 + b, gt_off + n)),
            pl.BlockSpec((CONV_W, LANES), lambda b, n: (0, n)),
            pl.BlockSpec((1, LANES), lambda b, n: (0, n)),
            mat, vec, mat, vec, vec,
        ],
        out_specs=[
            pl.BlockSpec((seq, LANES), lambda b, n: (b, n)),
            pl.BlockSpec((ctx_len, LANES), lambda b, n: (b, n)),
        ],
        out_shape=[
            jax.ShapeDtypeStruct((batch * seq, b_width), BF16),
            jax.ShapeDtypeStruct((batch * ctx_len, b_width), BF16),
        ],
        scratch_shapes=[
            pltpu.VMEM((total, LANES), F32),
            pltpu.VMEM((2, total, LANES), F32),
            pltpu.VMEM((2, total, LANES), F32),
            pltpu.VMEM((2, total, LANES), F32),
        ],
        name="rglru",
        compiler_params=_params(("parallel", "parallel")),
    )(p, p, p, p, conv_w, conv_b.reshape(1, b_width), wa, ba.reshape(2, 1, b_width),
      wx, bx.reshape(2, 1, b_width), lru_lam.reshape(2, 1, b_width))


GLA_UNROLL = 4


def _gla_kernel(ql_ref, kl_ref, vl_ref, rl_ref, zl_ref, qc_ref, kc_ref, vc_ref, rc_ref, zc_ref,
                w2_ref, gb_ref, g_ref, al_ref, ac_ref, st_s, ol_s, oc_s, *, ctx_len, seq):
    c = C_CHUNK
    ri = lax.broadcasted_iota(jnp.int32, (c, c), 0)
    ci = lax.broadcasted_iota(jnp.int32, (c, c), 1)
    keep = (ri >= ci, ri <= ci)
    tri = tuple(jnp.where(k, 1.0, 0.0).astype(BF16) for k in keep)
    end_row = (c - 1, 0)

    def step(q_ref, k_ref, v_ref, z_ref, o_ref, chains, accumulate):
        rows = [pl.ds(t0, c) for t0, _ in chains]
        dirs = [d for _, d in chains]
        xs = [_dot(z_ref[r, :].astype(BF16), w2_ref[d]) + gb_ref[d] for r, d in zip(rows, dirs)]
        gs = [_log_sigmoid(x) * (1.0 / C_GATE_TAU) for x in xs]
        his = [g.astype(BF16) for g in gs]
        los = [(g - hi.astype(F32)).astype(BF16) for g, hi in zip(gs, his)]
        bcums = [_dot(tri[d], hi) + _dot(tri[d], lo) for d, hi, lo in zip(dirs, his, los)]
        qes, kes, kds, decays = [], [], [], []
        for r, d, bcum in zip(rows, dirs, bcums):
            b_end = bcum[end_row[d]:end_row[d] + 1, :]
            k = k_ref[r, :].astype(F32)
            qes.append((q_ref[r, :].astype(F32) * jnp.exp(bcum)).astype(BF16))
            kes.append((k * jnp.exp(-bcum)).astype(BF16))
            kds.append((k * jnp.exp(b_end - bcum)).astype(BF16))
            decays.append(jnp.exp(b_end))
        raw = [_dot_nt(qe, ke) for qe, ke in zip(qes, kes)]
        ups = [_dot_tn(v_ref[r, :], kd) for r, kd in zip(rows, kds)]
        atts = [jnp.where(keep[d], a, 0.0).astype(BF16) for d, a in zip(dirs, raw)]
        intra = [_dot(att, v_ref[r, :]) for att, r in zip(atts, rows)]
        for r, d, qe, up, decay, oi in zip(rows, dirs, qes, ups, decays, intra):
            st = st_s[d]
            o = oi + _dot_nt(qe, st.astype(BF16))
            st_s[d] = st * decay + up
            if accumulate:
                o_ref[r, :] += o
            else:
                o_ref[r, :] = o

    def sweep(q_ref, k_ref, v_ref, z_ref, o_ref, n):
        half = n // 2
        unroll = math.gcd(GLA_UNROLL, half)

        def run(lo, accumulate):
            def body(i, carry):
                chains = []
                for u in range(unroll):
                    ci_ = i * unroll + u
                    chains.append((pl.multiple_of(ci_ * c, c), 0))
                    chains.append((pl.multiple_of((n - 1 - ci_) * c, c), 1))
                step(q_ref, k_ref, v_ref, z_ref, o_ref, chains, accumulate)
                return carry
            lax.fori_loop(lo // unroll, (lo + half) // unroll, body, 0)

        run(0, False)
        run(half, True)

    st_s[...] = jnp.zeros_like(st_s)
    sweep(qc_ref, kc_ref, vc_ref, zc_ref, oc_s, ctx_len // c)
    sweep(ql_ref, kl_ref, vl_ref, zl_ref, ol_s, seq // c)

    g = g_ref[...]
    ac_ref[...] = (_rms(oc_s[...]) * g * _silu(rc_ref[...].astype(F32))).astype(ac_ref.dtype)
    al_ref[...] = (_rms(ol_s[...]) * g * _silu(rl_ref[...].astype(F32))).astype(al_ref.dtype)


def _gla(p, z, w2p, gate_b, norm_g, batch, seq, ctx_len, heads):
    d_model = heads * C_VAL_DIM
    key_w = heads * C_KEY_DIM
    n_ctx_blk0 = batch * seq // ctx_len
    kern = functools.partial(_gla_kernel, ctx_len=ctx_len, seq=seq)

    def lat(width, off):
        return pl.BlockSpec((seq, width), lambda b, h: (b, off + h))

    def ctx(width, off):
        return pl.BlockSpec((ctx_len, width), lambda b, h: (n_ctx_blk0 + b, off + h))

    specs = [
        lat(C_KEY_DIM, 0), lat(C_KEY_DIM, heads), lat(C_VAL_DIM, heads), lat(C_VAL_DIM, 2 * heads),
        pl.BlockSpec((seq, LANES), lambda b, h: (b, 0)),
        ctx(C_KEY_DIM, 0), ctx(C_KEY_DIM, heads), ctx(C_VAL_DIM, heads), ctx(C_VAL_DIM, 2 * heads),
        pl.BlockSpec((ctx_len, LANES), lambda b, h: (n_ctx_blk0 + b, 0)),
    ]
    specs += [
        pl.BlockSpec((2, LANES, C_KEY_DIM), lambda b, h: (0, 0, h)),
        pl.BlockSpec((2, 1, C_KEY_DIM), lambda b, h: (0, 0, h)),
        pl.BlockSpec((1, C_VAL_DIM), lambda b, h: (0, 0)),
    ]
    return pl.pallas_call(
        kern,
        grid=(batch, heads),
        in_specs=specs,
        out_specs=[
            pl.BlockSpec((seq, C_VAL_DIM), lambda b, h: (b, h)),
            pl.BlockSpec((ctx_len, C_VAL_DIM), lambda b, h: (b, h)),
        ],
        out_shape=[
            jax.ShapeDtypeStruct((batch * seq, d_model), BF16),
            jax.ShapeDtypeStruct((batch * ctx_len, d_model), BF16),
        ],
        scratch_shapes=[
            pltpu.VMEM((2, C_VAL_DIM, C_KEY_DIM), F32),
            pltpu.VMEM((seq, C_VAL_DIM), F32),
            pltpu.VMEM((ctx_len, C_VAL_DIM), F32),
        ],
        name="gla",
        compiler_params=_params(("parallel", "parallel")),
    )(p, p, p, p, z, p, p, p, p, z, w2p, gate_b.reshape(2, 1, key_w), norm_g.reshape(1, C_VAL_DIM))


def _outproj_kernel(al0_ref, ac0_ref, al1_ref, ac1_ref, w0_ref, w1_ref, h_ref, mod_ref, o_ref, *, n_lat):
    is_lat = pl.program_id(0) < n_lat
    a0 = jnp.where(is_lat, al0_ref[...], ac0_ref[...])
    a1 = jnp.where(is_lat, al1_ref[...], ac1_ref[...])
    acc = _dot(a0, w0_ref[...]) + _dot(a1, w1_ref[...])
    o_ref[...] = h_ref[...] + mod_ref[2:3, :] * acc


def _outproj(srcs, w, h, mods, rows, n_tiles):
    d = w.shape[1]
    kh = w.shape[0] // 2
    tm = rows.tm
    n_lat = rows.n_lat
    specs, args = [], []
    for lat_arr, lat_col, ctx_arr, ctx_row0, ctx_col in srcs:
        specs.append(pl.BlockSpec((tm, kh), lambda i, c=lat_col: (jnp.minimum(i, n_lat - 1), c)))
        specs.append(pl.BlockSpec((tm, kh), lambda i, r=ctx_row0, c=ctx_col: (r + jnp.maximum(i - n_lat, 0), c)))
        args += [lat_arr, ctx_arr]
    kern = functools.partial(_outproj_kernel, n_lat=n_lat)
    return pl.pallas_call(
        kern,
        grid=(n_tiles,),
        in_specs=specs + [
            pl.BlockSpec((kh, d), lambda i: (0, 0)),
            pl.BlockSpec((kh, d), lambda i: (1, 0)),
            pl.BlockSpec((tm, d), lambda i: (i, 0)),
            pl.BlockSpec((None, N_MOD, d), lambda i: (rows.mod_row(i), 0, 0)),
        ],
        out_specs=pl.BlockSpec((tm, d), lambda i: (i, 0)),
        out_shape=jax.ShapeDtypeStruct((n_tiles * tm, d), F32),
        name="outproj",
        compiler_params=_params(("parallel",)),
    )(*args, w, w, h, mods)


def _mlp_kernel(h_ref, mod_ref, g_ref, w1_ref, w2_ref, fg_ref, o_ref, u_ref, *, nf, final_norm):
    j = pl.program_id(1)

    @pl.when(j == 0)
    def _():
        u_ref[...] = _norm_mod(h_ref[...], g_ref[...], mod_ref[3:4, :], mod_ref[4:5, :]).astype(BF16)
        o_ref[...] = jnp.zeros_like(o_ref)

    t = jnp.maximum(_dot(u_ref[...], w1_ref[...]), 0.0)
    o_ref[...] += _dot((t * t).astype(BF16), w2_ref[...])

    @pl.when(j == nf - 1)
    def _():
        y = h_ref[...] + mod_ref[5:6, :] * o_ref[...]
        if final_norm:
            y = _rms(y) * fg_ref[...]
        o_ref[...] = y


def _mlp(h, mods, norm_g, w1, w2, final_g, rows, n_tiles, tf, final_norm):
    d, ff = w1.shape
    nf = ff // tf
    tm = rows.tm
    kern = functools.partial(_mlp_kernel, nf=nf, final_norm=final_norm)
    return pl.pallas_call(
        kern,
        grid=(n_tiles, nf),
        in_specs=[
            pl.BlockSpec((tm, d), lambda i, j: (i, 0)),
            pl.BlockSpec((None, N_MOD, d), lambda i, j: (rows.mod_row(i), 0, 0)),
            pl.BlockSpec((1, d), lambda i, j: (0, 0)),
            pl.BlockSpec((d, tf), lambda i, j: (0, j)),
            pl.BlockSpec((tf, d), lambda i, j: (j, 0)),
            pl.BlockSpec((1, d), lambda i, j: (0, 0)),
        ],
        out_specs=pl.BlockSpec((tm, d), lambda i, j: (i, 0)),
        out_shape=jax.ShapeDtypeStruct((n_tiles * tm, d), F32),
        scratch_shapes=[pltpu.VMEM((tm, d), BF16)],
        name="mlp",
        compiler_params=_params(("parallel", "arbitrary")),
    )(h, mods, norm_g.reshape(1, d), w1, w2, final_g.reshape(1, d))


def _rope_tables(seq, pad_rows):
    inv = ROPE_BASE ** (-jnp.arange(ROPE_FREQS, dtype=F32) / ROPE_FREQS)
    t = jnp.arange(seq, dtype=jnp.int32)
    pos_r = (t // GRID_W).astype(F32)[:, None]
    pos_c = (t % GRID_W).astype(F32)[:, None]
    lane = jnp.arange(LANES, dtype=jnp.int32)[None, :]
    second = (lane // A_HEAD_DIM) == 1
    is_col = ((lane % (2 * ROPE_FREQS)) // ROPE_FREQS) == 1
    ang = jnp.where(is_col, pos_c, pos_r) * inv[lane % ROPE_FREQS]
    cos = jnp.cos(ang)
    sin = jnp.where(second, jnp.sin(ang), -jnp.sin(ang))
    qs = A_HEAD_DIM ** -0.5 * math.log2(math.e)
    ones = jnp.ones((pad_rows, LANES), F32)
    zeros = jnp.zeros((pad_rows, LANES), F32)
    one_all = jnp.ones((seq + pad_rows, LANES), F32)
    zero_all = jnp.zeros((seq + pad_rows, LANES), F32)

    def table(rot, pad):
        body = jnp.concatenate([rot, pad])
        return jnp.concatenate([qs * body, body, one_all if pad is ones else zero_all], axis=1)

    return table(cos, ones), table(sin, zeros)


def _permute_qk_columns(w, a_width):
    d = w.shape[0]
    qk = w[:, :2 * a_width].reshape(d, -1, 2, 2, 2, ROPE_FREQS).transpose(0, 1, 4, 2, 3, 5)
    return jnp.concatenate([qk.reshape(d, 2 * a_width), w[:, 2 * a_width:]], axis=1)


def kernel(x, c, ctx, c_ctx, ada_w, ada_b, norm1_g, norm2_g, mlp_w1, mlp_w2, ev_w_in, ev_w_out, ev_lambda_q1, ev_lambda_k1, ev_lambda_q2, ev_lambda_k2, ev_subln_g, ev_conv_w, ev_conv_b, ev_lru_wa, ev_lru_ba, ev_lru_wx, ev_lru_bx, ev_lru_lam, od_w_in, od_w_out, od_gate_w2, od_gate_b, od_norm_g, final_g):
    batch, seq, d = x.shape
    ctx_len = ctx.shape[1]
    depth = ada_w.shape[0]
    a_width = d // 2
    b_width = d - a_width
    a_heads = a_width // (2 * A_HEAD_DIM)
    c_heads = d // C_VAL_DIM
    key_w = c_heads * C_KEY_DIM

    tm = _pick_tile(1024, seq, batch * ctx_len)
    rows = _Rows(batch, seq, ctx_len, tm)
    rows_out = _Rows(batch, seq, ctx_len, _pick_tile(512, seq, batch * ctx_len))
    tf = _pick_tile(512, mlp_w1.shape[2])

    pad = (-(batch + 1)) % SUBLANES
    cc = jnp.concatenate([c, c_ctx[None, :], jnp.zeros((pad, d), F32)], axis=0)
    mods_all = _ada_mod(cc, ada_w, ada_b).reshape(depth, batch + 1 + pad, N_MOD, d)

    rope = _rope_tables(seq, tm)
    h = jnp.concatenate([x.reshape(batch * seq, d), ctx.reshape(batch * ctx_len, d)], axis=0)

    for i in range(depth):
        last = i == depth - 1
        mods = mods_all[i]
        j = i // 2
        if i % 2 == 0:
            lam_init = 0.8 - 0.6 * math.exp(-0.3 * i)
            lam = (jnp.exp(jnp.sum(ev_lambda_q1[j] * ev_lambda_k1[j]))
                   - jnp.exp(jnp.sum(ev_lambda_q2[j] * ev_lambda_k2[j])) + lam_init).reshape(1).astype(F32)
            w_in = _permute_qk_columns(ev_w_in[j], a_width).astype(BF16)
            p = _inproj_even(h, mods, norm1_g[i], w_in, rope, rows, a_width)
            ya_lat, ya_ctx = _diff_attn(p, lam, ev_subln_g[j], 1.0 - lam_init, batch, seq, ctx_len, a_heads)
            yb_lat, yb_ctx = _rglru(p, ev_conv_w[j], ev_conv_b[j], ev_lru_wa[j], ev_lru_ba[j],
                                    ev_lru_wx[j], ev_lru_bx[j], ev_lru_lam[j],
                                    batch, seq, ctx_len, a_width, b_width)
            srcs = [(ya_lat, 0, ya_ctx, 0, 0), (yb_lat, 0, yb_ctx, 0, 0)]
            w_out = ev_w_out[j]
        else:
            w_in = od_w_in[j]
            n_main = 2 * key_w + 2 * d
            col_scale = jnp.where(jnp.arange(n_main) < key_w, C_KEY_DIM ** -0.5, 1.0).astype(F32)
            w_main = (w_in[:, :n_main] * col_scale[None, :]).astype(BF16)
            wz = jnp.pad(w_in[:, n_main:], ((0, 0), (0, LANES - 2 * C_GATE_RANK))).astype(BF16)
            p, z = _inproj_odd(h, mods, norm1_g[i], w_main, wz, rows)
            w2p = jnp.zeros((2, LANES, key_w), F32)
            for dd in range(2):
                w2p = w2p.at[dd, dd * C_GATE_RANK:(dd + 1) * C_GATE_RANK, :].set(od_gate_w2[j, dd])
            a_lat, a_ctx = _gla(p, z, w2p.astype(BF16), od_gate_b[j], od_norm_g[j], batch, seq, ctx_len, c_heads)
            srcs = [(a_lat, 0, a_ctx, 0, 0), (a_lat, 1, a_ctx, 0, 1)]
            w_out = od_w_out[j]

        n_out = rows_out.n_lat if last else rows_out.n_lat + rows_out.n_ctx
        h = _outproj(srcs, w_out.astype(BF16), h, mods, rows_out, n_out)
        n_mlp = rows.n_lat if last else rows.n_lat + rows.n_ctx
        h = _mlp(h, mods, norm2_g[i], mlp_w1[i].astype(BF16), mlp_w2[i].astype(BF16), final_g,
                 rows, n_mlp, tf, last)

    return h.reshape(batch, seq, d)
```

```python
import functools
import math

import jax
import jax.numpy as jnp
from jax import lax
from jax.experimental import pallas as pl
from jax.experimental.pallas import tpu as pltpu

F32 = jnp.float32
BF16 = jnp.bfloat16

NORM_EPS = 1e-6
N_MOD = 6
GRID_W = 64
A_HEAD_DIM = 64
ROPE_FREQS = A_HEAD_DIM // 4
ROPE_BASE = 10000.0
B_BLOCK_DIM = 128
CONV_W = 4
CONV_LEFT = 2
LRU_C = 8.0
C_KEY_DIM = 256
C_VAL_DIM = 512
C_GATE_RANK = 16
C_GATE_TAU = 16.0
C_CHUNK = 64

LANES = 128
SUBLANES = 8
VMEM_LIMIT = 56 * 1024 * 1024


def _params(sem, vmem=VMEM_LIMIT):
    return pltpu.CompilerParams(dimension_semantics=sem, vmem_limit_bytes=vmem)


def _dot(a, b):
    return jnp.dot(a, b, preferred_element_type=F32)


def _dot_nt(a, b):
    return lax.dot_general(a, b, (((1,), (1,)), ((), ())), preferred_element_type=F32)


def _dot_tn(a, b):
    return lax.dot_general(a, b, (((0,), (0,)), ((), ())), preferred_element_type=F32)


def _sigmoid(x):
    return 1.0 / (1.0 + jnp.exp(-x))


def _silu(x):
    return x * _sigmoid(x)


def _softplus(x):
    return jnp.maximum(x, 0.0) + jnp.log(1.0 + jnp.exp(-jnp.abs(x)))


def _log_sigmoid(x):
    return -_softplus(-x)


def _gelu_tanh(x):
    return 0.5 * x * (1.0 + jnp.tanh(math.sqrt(2.0 / math.pi) * (x + 0.044715 * (x * x * x))))


def _rms(x):
    return x * lax.rsqrt(jnp.mean(x * x, axis=-1, keepdims=True) + NORM_EPS)


def _norm_mod(x, g, shift, scale):
    return (_rms(x) * g) * (1.0 + scale) + shift


def _pick_tile(limit, *dims):
    t = limit
    while t > SUBLANES and any(d % t for d in dims):
        t //= 2
    assert all(d % t == 0 for d in dims), (limit, dims)
    return t


def _ada_kernel(c_ref, w_ref, b_ref, o_ref):
    s = _silu(c_ref[...]).astype(BF16)
    o_ref[...] = _dot(s, w_ref[...].astype(BF16)) + b_ref[...]


def _ada_mod(cc, ada_w, ada_b):
    depth, d, n = ada_w.shape
    rows = cc.shape[0]
    tn = _pick_tile(1024, n)
    return pl.pallas_call(
        _ada_kernel,
        grid=(depth, n // tn),
        in_specs=[
            pl.BlockSpec((rows, d), lambda l, j: (0, 0)),
            pl.BlockSpec((None, d, tn), lambda l, j: (l, 0, j)),
            pl.BlockSpec((None, 1, tn), lambda l, j: (l, 0, j)),
        ],
        out_specs=pl.BlockSpec((None, rows, tn), lambda l, j: (l, 0, j)),
        out_shape=jax.ShapeDtypeStruct((depth, rows, n), F32),
        name="ada_mod",
        compiler_params=_params(("parallel", "parallel")),
    )(cc, ada_w, ada_b.reshape(depth, 1, n))


class _Rows:
    def __init__(self, batch, seq, ctx_len, tm):
        self.batch, self.seq, self.ctx = batch, seq, ctx_len
        self.tm = tm
        self.n_lat = batch * seq // tm
        self.n_ctx = batch * ctx_len // tm
        self.per_batch = seq // tm
        self.rows_lat = batch * seq
        self.rows_all = batch * (seq + ctx_len)

    def mod_row(self, i):
        return jnp.where(i < self.n_lat, i // self.per_batch, self.batch)

    def rope_row(self, i):
        return jnp.where(i < self.n_lat, i % self.per_batch, self.per_batch)


INPROJ_ROW_CHUNK = 256


def _inproj_even_kernel(h_ref, mod_ref, g_ref, w_ref, cos_ref, sin_ref, o_ref, u_ref):
    j = pl.program_id(1)
    tm, tn = o_ref.shape
    rc = min(INPROJ_ROW_CHUNK, tm)

    def step(first):
        w = w_ref[...]
        for r in range(tm // rc):
            rs = slice(r * rc, (r + 1) * rc)
            if first:
                u = _norm_mod(h_ref[rs, :], g_ref[...], mod_ref[0:1, :], mod_ref[1:2, :]).astype(BF16)
                u_ref[rs, :] = u
            else:
                u = u_ref[rs, :]
            x = _dot(u, w)
            cos, sin = cos_ref[rs, :], sin_ref[rs, :]
            for g in range(tn // LANES):
                xg = x[:, g * LANES:(g + 1) * LANES]
                y = xg * cos + pltpu.roll(xg, LANES // 2, 1) * sin
                o_ref[rs, g * LANES:(g + 1) * LANES] = y.astype(o_ref.dtype)

    @pl.when(j == 0)
    def _():
        step(True)

    @pl.when(j > 0)
    def _():
        step(False)


def _inproj_even(h, mods, norm_g, w, rope, rows, a_width):
    d, n = w.shape
    tm = rows.tm
    tn = _pick_tile(1024, a_width, n)
    cos, sin = rope
    nq, nqk = a_width // tn, 2 * a_width // tn

    def col_type(j):
        return (j >= nq).astype(jnp.int32) + (j >= nqk).astype(jnp.int32)

    tab = pl.BlockSpec((tm, LANES), lambda i, j: (rows.rope_row(i), col_type(j)))
    return pl.pallas_call(
        _inproj_even_kernel,
        grid=(rows.n_lat + rows.n_ctx, n // tn),
        in_specs=[
            pl.BlockSpec((tm, d), lambda i, j: (i, 0)),
            pl.BlockSpec((None, N_MOD, d), lambda i, j: (rows.mod_row(i), 0, 0)),
            pl.BlockSpec((1, d), lambda i, j: (0, 0)),
            pl.BlockSpec((d, tn), lambda i, j: (0, j)),
            tab, tab,
        ],
        out_specs=pl.BlockSpec((tm, tn), lambda i, j: (i, j)),
        out_shape=jax.ShapeDtypeStruct((rows.rows_all, n), BF16),
        scratch_shapes=[pltpu.VMEM((tm, d), BF16)],
        name="inproj_even",
        compiler_params=_params(("parallel", "arbitrary")),
    )(h, mods, norm_g.reshape(1, d), w, cos, sin)


def _inproj_odd_kernel(h_ref, mod_ref, g_ref, w_ref, wz_ref, o_ref, z_ref, u_ref, *, nq):
    j = pl.program_id(1)
    tm = o_ref.shape[0]
    rc = min(INPROJ_ROW_CHUNK, tm)
    col_scale = jnp.where(j < nq, C_KEY_DIM ** -0.5, 1.0).astype(F32)

    def step(first):
        w = w_ref[...]
        for r in range(tm // rc):
            rs = slice(r * rc, (r + 1) * rc)
            if first:
                u = _norm_mod(h_ref[rs, :], g_ref[...], mod_ref[0:1, :], mod_ref[1:2, :]).astype(BF16)
                u_ref[rs, :] = u
                z_ref[rs, :] = _dot(u, wz_ref[...])
            else:
                u = u_ref[rs, :]
            o_ref[rs, :] = (_dot(u, w) * col_scale).astype(o_ref.dtype)

    @pl.when(j == 0)
    def _():
        step(True)

    @pl.when(j > 0)
    def _():
        step(False)


def _inproj_odd(h, mods, norm_g, w, wz, rows, key_w, n):
    d = w.shape[0]
    tm = rows.tm
    tn = _pick_tile(1024, key_w, n)
    return pl.pallas_call(
        functools.partial(_inproj_odd_kernel, nq=key_w // tn),
        grid=(rows.n_lat + rows.n_ctx, n // tn),
        in_specs=[
            pl.BlockSpec((tm, d), lambda i, j: (i, 0)),
            pl.BlockSpec((None, N_MOD, d), lambda i, j: (rows.mod_row(i), 0, 0)),
            pl.BlockSpec((1, d), lambda i, j: (0, 0)),
            pl.BlockSpec((d, tn), lambda i, j: (0, j)),
            pl.BlockSpec((d, LANES), lambda i, j: (0, 0)),
        ],
        out_specs=[
            pl.BlockSpec((tm, tn), lambda i, j: (i, j)),
            pl.BlockSpec((tm, LANES), lambda i, j: (i, 0)),
        ],
        out_shape=[
            jax.ShapeDtypeStruct((rows.rows_all, n), BF16),
            jax.ShapeDtypeStruct((rows.rows_all, LANES), F32),
        ],
        scratch_shapes=[pltpu.VMEM((tm, d), BF16)],
        name="inproj_odd",
        compiler_params=_params(("parallel", "arbitrary")),
    )(h, mods, norm_g.reshape(1, d), w, wz)


def _attn_kernel(lam_ref, ql_ref, qc_ref, kl_ref, vl_ref, kc_ref, vc_ref, g_ref, ol_ref, oc_ref,
                 k_s, v_s, s0_s, s1_s, p0_s, p1_s, *, tq, kt, post_scale):
    ctx_len, seq = kc_ref.shape[0], kl_ref.shape[0]
    total = ctx_len + seq
    nkt, nq, ng = total // kt, seq // tq, kt // LANES
    s_bufs, p_bufs = (s0_s, s1_s), (p0_s, p1_s)
    lam = lam_ref[0]
    gain = g_ref[...] * post_scale

    for src_ref, lo in ((kc_ref, 0), (kl_ref, ctx_len)):
        for c in range(src_ref.shape[0] // LANES):
            k_s[:, lo + c * LANES:lo + (c + 1) * LANES] = src_ref[c * LANES:(c + 1) * LANES, :].T
    v_s[0:ctx_len, 0:LANES] = vc_ref[...]
    v_s[ctx_len:total, 0:LANES] = vl_ref[...]
    v_s[:, LANES:2 * LANES] = jnp.ones((total, LANES), BF16)

    def map_mask(shape, mi):
        lane = lax.broadcasted_iota(jnp.int32, shape, 1)
        return ((lane // (A_HEAD_DIM // 2)) % 2) == mi

    def q_map(t, mi):
        q = ql_ref[pl.ds(pl.multiple_of(t * tq, tq), tq), :]
        return jnp.where(map_mask(q.shape, mi), q, jnp.zeros_like(q))

    def scores(qa, buf, j, m):
        s = _dot(qa, k_s[:, j * kt:(j + 1) * kt])
        s_bufs[buf][:, j * kt:(j + 1) * kt] = s
        for g in range(ng):
            m = jnp.maximum(m, s[:, g * LANES:(g + 1) * LANES])
        return m

    def exps(buf, j, mb):
        x = s_bufs[buf][:, j * kt:(j + 1) * kt]
        ps = [jnp.exp2(x[:, g * LANES:(g + 1) * LANES] - mb).astype(BF16) for g in range(ng)]
        p_bufs[buf][:, j * kt:(j + 1) * kt] = ps[0] if ng == 1 else jnp.concatenate(ps, axis=1)

    def values(buf):
        r = _dot(p_bufs[buf][...], v_s[...])
        return r[:, 0:LANES] / r[:, LANES:2 * LANES]

    def row_max(m):
        return jnp.broadcast_to(jnp.max(m, axis=-1, keepdims=True), m.shape)

    neg = jnp.full((tq, LANES), -jnp.inf, F32)

    qa = q_map(0, 0)
    m = neg
    for j in range(nkt):
        m = scores(qa, 0, j, m)
    mx0 = row_max(m)
    qb = q_map(0, 1)
    m = neg
    for j in range(nkt):
        m = scores(qb, 1, j, m)
        exps(0, j, mx0)

    def tile_body(t, mx1):
        tn = jnp.minimum(t + 1, nq - 1)
        qa = q_map(tn, 0)
        m = neg
        for j in range(nkt):
            m = scores(qa, 0, j, m)
        mx0n = row_max(m)
        for j in range(nkt):
            exps(1, j, mx1)
        o0 = values(0)
        o1 = values(1)
        qb = q_map(tn, 1)
        m = neg
        for j in range(nkt):
            m = scores(qb, 1, j, m)
            exps(0, j, mx0n)
        o = o0 - lam * o1
        ol_ref[pl.ds(pl.multiple_of(t * tq, tq), tq), :] = (_rms(o) * gain).astype(ol_ref.dtype)
        return row_max(m)

    lax.fori_loop(0, nq, tile_body, row_max(m))

    qc = qc_ref[...]
    outs = []
    for mi in range(2):
        qm = jnp.where(map_mask(qc.shape, mi), qc, jnp.zeros_like(qc))
        s = _dot_nt(qm, kc_ref[...])
        p = jnp.exp2(s - jnp.max(s, axis=-1, keepdims=True))
        outs.append(_dot(p.astype(BF16), vc_ref[...]) / jnp.sum(p, axis=-1, keepdims=True))
    oc_ref[...] = (_rms(outs[0] - lam * outs[1]) * gain).astype(oc_ref.dtype)


def _diff_attn(p, lam, subln_g, post_scale, batch, seq, ctx_len, heads):
    total = seq + ctx_len
    tq = _pick_tile(256, seq)
    kt = _pick_tile(256, total)
    nkt = total // kt
    n_ctx_blk0 = batch * seq // ctx_len
    kern = functools.partial(_attn_kernel, tq=tq, kt=kt, post_scale=post_scale)

    def lat(off):
        return pl.BlockSpec((seq, LANES), lambda b, h: (b, off + h))

    def ctx(off):
        return pl.BlockSpec((ctx_len, LANES), lambda b, h: (n_ctx_blk0 + b, off + h))

    return pl.pallas_call(
        kern,
        grid=(batch, heads),
        in_specs=[
            pl.BlockSpec(memory_space=pltpu.SMEM),
            lat(0), ctx(0), lat(heads), lat(2 * heads), ctx(heads), ctx(2 * heads),
            pl.BlockSpec((1, LANES), lambda b, h: (0, 0)),
        ],
        out_specs=[
            pl.BlockSpec((seq, LANES), lambda b, h: (b, h)),
            pl.BlockSpec((ctx_len, LANES), lambda b, h: (b, h)),
        ],
        out_shape=[
            jax.ShapeDtypeStruct((batch * seq, heads * LANES), BF16),
            jax.ShapeDtypeStruct((batch * ctx_len, heads * LANES), BF16),
        ],
        scratch_shapes=[
            pltpu.VMEM((LANES, total), BF16),
            pltpu.VMEM((total, 2 * LANES), BF16),
            pltpu.VMEM((tq, total), F32),
            pltpu.VMEM((tq, total), F32),
            pltpu.VMEM((tq, total), BF16),
            pltpu.VMEM((tq, total), BF16),
        ],
        name="diff_attn",
        compiler_params=_params(("parallel", "parallel")),
    )(lam, p, p, p, p, p, p, subln_g.reshape(1, LANES))


def _dwconv(x, w, b):
    n = x.shape[0]
    row = lax.broadcasted_iota(jnp.int32, x.shape, 0)
    y = b + x * w[CONV_LEFT:CONV_LEFT + 1, :]
    for j in range(CONV_W):
        off = j - CONV_LEFT
        if off == 0:
            continue
        xs = pltpu.roll(x, (-off) % n, 0)
        ok = jnp.logical_and(row + off >= 0, row + off < n)
        y = y + jnp.where(ok, xs, 0.0) * w[j:j + 1, :]
    return y


def _lru_kernel(recl_ref, recc_ref, gtl_ref, gtc_ref, cw_ref, cb_ref, wa_ref, ba_ref, wx_ref, bx_ref,
                lam_ref, ol_ref, oc_ref, a_s, b_s, h_s, *, ctx_len, seq):
    total = ctx_len + seq
    nt = total // SUBLANES
    cw = cw_ref[...]
    cb = cb_ref[...]
    region_off = ((0, seq), (ctx_len, 0))
    decay_rate = [LRU_C * _softplus(-lam_ref[d]) for d in range(2)]
    for ri, rec_ref in enumerate((recc_ref, recl_ref)):
        n = rec_ref.shape[0]
        xc = _dwconv(rec_ref[...].astype(F32), cw, cb)
        xcb = xc.astype(BF16)
        pre = [(_dot(xcb, wa_ref[d].astype(BF16)), _dot(xcb, wx_ref[d].astype(BF16))) for d in range(2)]
        for d in range(2):
            r = _sigmoid(pre[d][0] + ba_ref[d])
            gi = _sigmoid(pre[d][1] + bx_ref[d])
            a = jnp.exp(-(r * decay_rate[d]))
            lo = region_off[ri][d]
            a_s[d, lo:lo + n, :] = a
            b_s[d, lo:lo + n, :] = jnp.sqrt(1.0 - a * a) * gi * xc

    row = lax.broadcasted_iota(jnp.int32, (SUBLANES, LANES), 0)

    def scan_tile(a, b, reverse):
        s = 1
        while s < SUBLANES:
            sh = (SUBLANES - s) if reverse else s
            a_sh = pltpu.roll(a, sh, 0)
            b_sh = pltpu.roll(b, sh, 0)
            ok = (row < SUBLANES - s) if reverse else (row >= s)
            b = jnp.where(ok, a * b_sh + b, b)
            a = jnp.where(ok, a * a_sh, a)
            s *= 2
        return a, b

    def body(p, carry):
        cf, cbk = carry
        tf = pl.multiple_of(p * SUBLANES, SUBLANES)
        a, b = scan_tile(a_s[0, pl.ds(tf, SUBLANES), :], b_s[0, pl.ds(tf, SUBLANES), :], False)
        hf = a * cf + b
        h_s[0, pl.ds(tf, SUBLANES), :] = hf
        tb = pl.multiple_of((nt - 1 - p) * SUBLANES, SUBLANES)
        a, b = scan_tile(a_s[1, pl.ds(tb, SUBLANES), :], b_s[1, pl.ds(tb, SUBLANES), :], True)
        hb = a * cbk + b
        h_s[1, pl.ds(tb, SUBLANES), :] = hb
        return hf[SUBLANES - 1:SUBLANES, :], hb[0:1, :]

    zero = jnp.zeros((1, LANES), F32)
    lax.fori_loop(0, nt, body, (zero, zero), unroll=4)

    oc_ref[...] = (_gelu_tanh(gtc_ref[...].astype(F32))
                   * (h_s[0, 0:ctx_len, :] + h_s[1, seq:total, :])).astype(oc_ref.dtype)
    ol_ref[...] = (_gelu_tanh(gtl_ref[...].astype(F32))
                   * (h_s[0, ctx_len:total, :] + h_s[1, 0:seq, :])).astype(ol_ref.dtype)


def _rglru(p, conv_w, conv_b, wa, ba, wx, bx, lru_lam, batch, seq, ctx_len, a_width, b_width):
    nb = b_width // B_BLOCK_DIM
    gt_off = 3 * a_width // LANES
    rec_off = (3 * a_width + b_width) // LANES
    n_ctx_blk0 = batch * seq // ctx_len
    total = seq + ctx_len
    kern = functools.partial(_lru_kernel, ctx_len=ctx_len, seq=seq)
    vec = pl.BlockSpec((2, 1, LANES), lambda b, n: (0, 0, n))
    mat = pl.BlockSpec((2, None, B_BLOCK_DIM, B_BLOCK_DIM), lambda b, n: (0, n, 0, 0))
    return pl.pallas_call(
        kern,
        grid=(batch, nb),
        in_specs=[
            pl.BlockSpec((seq, LANES), lambda b, n: (b, rec_off + n)),
            pl.BlockSpec((ctx_len, LANES), lambda b, n: (n_ctx_blk0 + b, rec_off + n)),
            pl.BlockSpec((seq, LANES), lambda b, n: (b, gt_off + n)),
            pl.BlockSpec((ctx_len, LANES), lambda b, n: (n_ctx_blk0 + b, gt_off + n)),
            pl.BlockSpec((CONV_W, LANES), lambda b, n: (0, n)),
            pl.BlockSpec((1, LANES), lambda b, n: (0, n)),
            mat, vec, mat, vec, vec,
        ],
        out_specs=[
            pl.BlockSpec((seq, LANES), lambda b, n: (b, n)),
            pl.BlockSpec((ctx_len, LANES), lambda b, n: (b, n)),
        ],
        out_shape=[
            jax.ShapeDtypeStruct((batch * seq, b_width), BF16),
            jax.ShapeDtypeStruct((batch * ctx_len, b_width), BF16),
        ],
        scratch_shapes=[
            pltpu.VMEM((2, total, LANES), F32),
            pltpu.VMEM((2, total, LANES), F32),
            pltpu.VMEM((2, total, LANES), F32),
        ],
        name="rglru",
        compiler_params=_params(("parallel", "parallel")),
    )(p, p, p, p, conv_w, conv_b.reshape(1, b_width), wa, ba.reshape(2, 1, b_width),
      wx, bx.reshape(2, 1, b_width), lru_lam.reshape(2, 1, b_width))


GLA_UNROLL = 4


def _gla_kernel(ql_ref, kl_ref, vl_ref, rl_ref, zl_ref, qc_ref, kc_ref, vc_ref, rc_ref, zc_ref,
                w2_ref, gb_ref, g_ref, al_ref, ac_ref, st_s, ol_s, oc_s, *, ctx_len, seq):
    c = C_CHUNK
    ri = lax.broadcasted_iota(jnp.int32, (c, c), 0)
    ci = lax.broadcasted_iota(jnp.int32, (c, c), 1)
    keep = (ri >= ci, ri <= ci)
    tri = tuple(jnp.where(k, 1.0, 0.0).astype(BF16) for k in keep)
    end_row = (c - 1, 0)

    def step(q_ref, k_ref, v_ref, z_ref, o_ref, chains, accumulate):
        rows = [pl.ds(t0, c) for t0, _ in chains]
        dirs = [d for _, d in chains]
        xs = [_dot(z_ref[r, :].astype(BF16), w2_ref[d]) + gb_ref[d] for r, d in zip(rows, dirs)]
        gs = [_log_sigmoid(x) * (1.0 / C_GATE_TAU) for x in xs]
        his = [g.astype(BF16) for g in gs]
        los = [(g - hi.astype(F32)).astype(BF16) for g, hi in zip(gs, his)]
        bcums = [_dot(tri[d], hi) + _dot(tri[d], lo) for d, hi, lo in zip(dirs, his, los)]
        qes, kes, kds, decays = [], [], [], []
        for r, d, bcum in zip(rows, dirs, bcums):
            b_end = bcum[end_row[d]:end_row[d] + 1, :]
            k = k_ref[r, :].astype(F32)
            qes.append((q_ref[r, :].astype(F32) * jnp.exp(bcum)).astype(BF16))
            kes.append((k * jnp.exp(-bcum)).astype(BF16))
            kds.append((k * jnp.exp(b_end - bcum)).astype(BF16))
            decays.append(jnp.exp(b_end))
        raw = [_dot_nt(qe, ke) for qe, ke in zip(qes, kes)]
        ups = [_dot_tn(v_ref[r, :], kd) for r, kd in zip(rows, kds)]
        atts = [jnp.where(keep[d], a, 0.0).astype(BF16) for d, a in zip(dirs, raw)]
        intra = [_dot(att, v_ref[r, :]) for att, r in zip(atts, rows)]
        for r, d, qe, up, decay, oi in zip(rows, dirs, qes, ups, decays, intra):
            st = st_s[d]
            o = oi + _dot_nt(qe, st.astype(BF16))
            st_s[d] = st * decay + up
            if accumulate:
                o_ref[r, :] += o
            else:
                o_ref[r, :] = o

    def sweep(q_ref, k_ref, v_ref, z_ref, o_ref, n):
        half = n // 2
        unroll = math.gcd(GLA_UNROLL, half)

        def run(lo, accumulate):
            def body(i, carry):
                chains = []
                for u in range(unroll):
                    ci_ = i * unroll + u
                    chains.append((pl.multiple_of(ci_ * c, c), 0))
                    chains.append((pl.multiple_of((n - 1 - ci_) * c, c), 1))
                step(q_ref, k_ref, v_ref, z_ref, o_ref, chains, accumulate)
                return carry
            lax.fori_loop(lo // unroll, (lo + half) // unroll, body, 0)

        run(0, False)
        run(half, True)

    st_s[...] = jnp.zeros_like(st_s)
    sweep(qc_ref, kc_ref, vc_ref, zc_ref, oc_s, ctx_len // c)
    sweep(ql_ref, kl_ref, vl_ref, zl_ref, ol_s, seq // c)

    g = g_ref[...]
    ac_ref[...] = (_rms(oc_s[...]) * g * _silu(rc_ref[...].astype(F32))).astype(ac_ref.dtype)
    al_ref[...] = (_rms(ol_s[...]) * g * _silu(rl_ref[...].astype(F32))).astype(al_ref.dtype)


def _gla(p, z, w2p, gate_b, norm_g, batch, seq, ctx_len, heads):
    d_model = heads * C_VAL_DIM
    key_w = heads * C_KEY_DIM
    n_ctx_blk0 = batch * seq // ctx_len
    kern = functools.partial(_gla_kernel, ctx_len=ctx_len, seq=seq)

    def lat(width, off):
        return pl.BlockSpec((seq, width), lambda b, h: (b, off + h))

    def ctx(width, off):
        return pl.BlockSpec((ctx_len, width), lambda b, h: (n_ctx_blk0 + b, off + h))

    specs = [
        lat(C_KEY_DIM, 0), lat(C_KEY_DIM, heads), lat(C_VAL_DIM, heads), lat(C_VAL_DIM, 2 * heads),
        pl.BlockSpec((seq, LANES), lambda b, h: (b, 0)),
        ctx(C_KEY_DIM, 0), ctx(C_KEY_DIM, heads), ctx(C_VAL_DIM, heads), ctx(C_VAL_DIM, 2 * heads),
        pl.BlockSpec((ctx_len, LANES), lambda b, h: (n_ctx_blk0 + b, 0)),
    ]
    specs += [
        pl.BlockSpec((2, LANES, C_KEY_DIM), lambda b, h: (0, 0, h)),
        pl.BlockSpec((2, 1, C_KEY_DIM), lambda b, h: (0, 0, h)),
        pl.BlockSpec((1, C_VAL_DIM), lambda b, h: (0, 0)),
    ]
    return pl.pallas_call(
        kern,
        grid=(batch, heads),
        in_specs=specs,
        out_specs=[
            pl.BlockSpec((seq, C_VAL_DIM), lambda b, h: (b, h)),
            pl.BlockSpec((ctx_len, C_VAL_DIM), lambda b, h: (b, h)),
        ],
        out_shape=[
            jax.ShapeDtypeStruct((batch * seq, d_model), BF16),
            jax.ShapeDtypeStruct((batch * ctx_len, d_model), BF16),
        ],
        scratch_shapes=[
            pltpu.VMEM((2, C_VAL_DIM, C_KEY_DIM), F32),
            pltpu.VMEM((seq, C_VAL_DIM), F32),
            pltpu.VMEM((ctx_len, C_VAL_DIM), F32),
        ],
        name="gla",
        compiler_params=_params(("parallel", "parallel")),
    )(p, p, p, p, z, p, p, p, p, z, w2p, gate_b.reshape(2, 1, key_w), norm_g.reshape(1, C_VAL_DIM))


def _outproj_kernel(al0_ref, ac0_ref, al1_ref, ac1_ref, w0_ref, w1_ref, h_ref, mod_ref, o_ref, *, n_lat):
    is_lat = pl.program_id(0) < n_lat
    a0 = jnp.where(is_lat, al0_ref[...], ac0_ref[...])
    a1 = jnp.where(is_lat, al1_ref[...], ac1_ref[...])
    acc = _dot(a0, w0_ref[...]) + _dot(a1, w1_ref[...])
    o_ref[...] = h_ref[...] + mod_ref[2:3, :] * acc


def _outproj(srcs, w, h, mods, rows, n_tiles):
    d = w.shape[1]
    kh = w.shape[0] // 2
    tm = rows.tm
    n_lat = rows.n_lat
    specs, args = [], []
    for lat_arr, lat_col, ctx_arr, ctx_row0, ctx_col in srcs:
        specs.append(pl.BlockSpec((tm, kh), lambda i, c=lat_col: (jnp.minimum(i, n_lat - 1), c)))
        specs.append(pl.BlockSpec((tm, kh), lambda i, r=ctx_row0, c=ctx_col: (r + jnp.maximum(i - n_lat, 0), c)))
        args += [lat_arr, ctx_arr]
    kern = functools.partial(_outproj_kernel, n_lat=n_lat)
    return pl.pallas_call(
        kern,
        grid=(n_tiles,),
        in_specs=specs + [
            pl.BlockSpec((kh, d), lambda i: (0, 0)),
            pl.BlockSpec((kh, d), lambda i: (1, 0)),
            pl.BlockSpec((tm, d), lambda i: (i, 0)),
            pl.BlockSpec((None, N_MOD, d), lambda i: (rows.mod_row(i), 0, 0)),
        ],
        out_specs=pl.BlockSpec((tm, d), lambda i: (i, 0)),
        out_shape=jax.ShapeDtypeStruct((n_tiles * tm, d), F32),
        name="outproj",
        compiler_params=_params(("parallel",)),
    )(*args, w, w, h, mods)


MLP_ROW_CHUNK = 512


def _mlp_kernel(h_ref, mod_ref, g_ref, w1_ref, w2_ref, fg_ref, o_ref, u_ref, *, nf, final_norm):
    j = pl.program_id(1)
    tm = o_ref.shape[0]
    rc = min(MLP_ROW_CHUNK, tm)

    def step(first):
        w1, w2 = w1_ref[...], w2_ref[...]
        for r in range(tm // rc):
            rs = slice(r * rc, (r + 1) * rc)
            if first:
                u = _norm_mod(h_ref[rs, :], g_ref[...], mod_ref[3:4, :], mod_ref[4:5, :]).astype(BF16)
                u_ref[rs, :] = u
            else:
                u = u_ref[rs, :]
            t = jnp.maximum(_dot(u, w1), 0.0)
            part = _dot((t * t).astype(BF16), w2)
            if first:
                o_ref[rs, :] = part
            else:
                o_ref[rs, :] += part

    @pl.when(j == 0)
    def _():
        step(True)

    @pl.when(j > 0)
    def _():
        step(False)

    @pl.when(j == nf - 1)
    def _():
        y = h_ref[...] + mod_ref[5:6, :] * o_ref[...]
        if final_norm:
            y = _rms(y) * fg_ref[...]
        o_ref[...] = y


def _mlp(h, mods, norm_g, w1, w2, final_g, rows, n_tiles, tf, final_norm):
    d, ff = w1.shape
    nf = ff // tf
    tm = rows.tm
    kern = functools.partial(_mlp_kernel, nf=nf, final_norm=final_norm)
    return pl.pallas_call(
        kern,
        grid=(n_tiles, nf),
        in_specs=[
            pl.BlockSpec((tm, d), lambda i, j: (i, 0)),
            pl.BlockSpec((None, N_MOD, d), lambda i, j: (rows.mod_row(i), 0, 0)),
            pl.BlockSpec((1, d), lambda i, j: (0, 0)),
            pl.BlockSpec((d, tf), lambda i, j: (0, j)),
            pl.BlockSpec((tf, d), lambda i, j: (j, 0)),
            pl.BlockSpec((1, d), lambda i, j: (0, 0)),
        ],
        out_specs=pl.BlockSpec((tm, d), lambda i, j: (i, 0)),
        out_shape=jax.ShapeDtypeStruct((n_tiles * tm, d), F32),
        scratch_shapes=[pltpu.VMEM((tm, d), BF16)],
        name="mlp",
        compiler_params=_params(("parallel", "arbitrary")),
    )(h, mods, norm_g.reshape(1, d), w1, w2, final_g.reshape(1, d))


def _rope_tables(seq, pad_rows):
    inv = ROPE_BASE ** (-jnp.arange(ROPE_FREQS, dtype=F32) / ROPE_FREQS)
    t = jnp.arange(seq, dtype=jnp.int32)
    pos_r = (t // GRID_W).astype(F32)[:, None]
    pos_c = (t % GRID_W).astype(F32)[:, None]
    lane = jnp.arange(LANES, dtype=jnp.int32)[None, :]
    second = (lane // A_HEAD_DIM) == 1
    is_col = ((lane % (2 * ROPE_FREQS)) // ROPE_FREQS) == 1
    ang = jnp.where(is_col, pos_c, pos_r) * inv[lane % ROPE_FREQS]
    cos = jnp.cos(ang)
    sin = jnp.where(second, jnp.sin(ang), -jnp.sin(ang))
    qs = A_HEAD_DIM ** -0.5 * math.log2(math.e)
    ones = jnp.ones((pad_rows, LANES), F32)
    zeros = jnp.zeros((pad_rows, LANES), F32)
    one_all = jnp.ones((seq + pad_rows, LANES), F32)
    zero_all = jnp.zeros((seq + pad_rows, LANES), F32)

    def table(rot, pad):
        body = jnp.concatenate([rot, pad])
        return jnp.concatenate([qs * body, body, one_all if pad is ones else zero_all], axis=1)

    return table(cos, ones), table(sin, zeros)


def _permute_qk_columns(w, a_width):
    d = w.shape[0]
    qk = w[:, :2 * a_width].reshape(d, -1, 2, 2, 2, ROPE_FREQS).transpose(0, 1, 4, 2, 3, 5)
    return jnp.concatenate([qk.reshape(d, 2 * a_width), w[:, 2 * a_width:]], axis=1)


def kernel(x, c, ctx, c_ctx, ada_w, ada_b, norm1_g, norm2_g, mlp_w1, mlp_w2, ev_w_in, ev_w_out, ev_lambda_q1, ev_lambda_k1, ev_lambda_q2, ev_lambda_k2, ev_subln_g, ev_conv_w, ev_conv_b, ev_lru_wa, ev_lru_ba, ev_lru_wx, ev_lru_bx, ev_lru_lam, od_w_in, od_w_out, od_gate_w2, od_gate_b, od_norm_g, final_g):
    batch, seq, d = x.shape
    ctx_len = ctx.shape[1]
    depth = ada_w.shape[0]
    a_width = d // 2
    b_width = d - a_width
    a_heads = a_width // (2 * A_HEAD_DIM)
    c_heads = d // C_VAL_DIM
    key_w = c_heads * C_KEY_DIM

    tm = _pick_tile(1024, seq, batch * ctx_len)
    rows = _Rows(batch, seq, ctx_len, tm)
    rows_out = _Rows(batch, seq, ctx_len, _pick_tile(512, seq, batch * ctx_len))
    tf = _pick_tile(512, mlp_w1.shape[2])

    pad = (-(batch + 1)) % SUBLANES
    cc = jnp.concatenate([c, c_ctx[None, :], jnp.zeros((pad, d), F32)], axis=0)
    mods_all = _ada_mod(cc, ada_w, ada_b).reshape(depth, batch + 1 + pad, N_MOD, d)

    rope = _rope_tables(seq, tm)
    h = jnp.concatenate([x.reshape(batch * seq, d), ctx.reshape(batch * ctx_len, d)], axis=0)

    for i in range(depth):
        last = i == depth - 1
        mods = mods_all[i]
        j = i // 2
        if i % 2 == 0:
            lam_init = 0.8 - 0.6 * math.exp(-0.3 * i)
            lam = (jnp.exp(jnp.sum(ev_lambda_q1[j] * ev_lambda_k1[j]))
                   - jnp.exp(jnp.sum(ev_lambda_q2[j] * ev_lambda_k2[j])) + lam_init).reshape(1).astype(F32)
            w_in = _permute_qk_columns(ev_w_in[j], a_width).astype(BF16)
            p = _inproj_even(h, mods, norm1_g[i], w_in, rope, rows, a_width)
            ya_lat, ya_ctx = _diff_attn(p, lam, ev_subln_g[j], 1.0 - lam_init, batch, seq, ctx_len, a_heads)
            yb_lat, yb_ctx = _rglru(p, ev_conv_w[j], ev_conv_b[j], ev_lru_wa[j], ev_lru_ba[j],
                                    ev_lru_wx[j], ev_lru_bx[j], ev_lru_lam[j],
                                    batch, seq, ctx_len, a_width, b_width)
            srcs = [(ya_lat, 0, ya_ctx, 0, 0), (yb_lat, 0, yb_ctx, 0, 0)]
            w_out = ev_w_out[j]
        else:
            w_in = od_w_in[j]
            n_main = 2 * key_w + 2 * d
            wz = jnp.pad(w_in[:, n_main:], ((0, 0), (0, LANES - 2 * C_GATE_RANK))).astype(BF16)
            p, z = _inproj_odd(h, mods, norm1_g[i], w_in.astype(BF16), wz, rows, key_w, n_main)
            w2p = jnp.zeros((2, LANES, key_w), F32)
            for dd in range(2):
                w2p = w2p.at[dd, dd * C_GATE_RANK:(dd + 1) * C_GATE_RANK, :].set(od_gate_w2[j, dd])
            a_lat, a_ctx = _gla(p, z, w2p.astype(BF16), od_gate_b[j], od_norm_g[j], batch, seq, ctx_len, c_heads)
            srcs = [(a_lat, 0, a_ctx, 0, 0), (a_lat, 1, a_ctx, 0, 1)]
            w_out = od_w_out[j]

        n_out = rows_out.n_lat if last else rows_out.n_lat + rows_out.n_ctx
        h = _outproj(srcs, w_out.astype(BF16), h, mods, rows_out, n_out)
        n_mlp = rows.n_lat if last else rows.n_lat + rows.n_ctx
        h = _mlp(h, mods, norm2_g[i], mlp_w1[i].astype(BF16), mlp_w2[i].astype(BF16), final_g,
                 rows, n_mlp, tf, last)

    return h.reshape(batch, seq, d)
```

```python
import functools
import math

import jax
import jax.numpy as jnp
from jax import lax
from jax.experimental import pallas as pl
from jax.experimental.pallas import tpu as pltpu

F32 = jnp.float32
BF16 = jnp.bfloat16

NORM_EPS = 1e-6
N_MOD = 6
GRID_W = 64
A_HEAD_DIM = 64
ROPE_FREQS = A_HEAD_DIM // 4
ROPE_BASE = 10000.0
B_BLOCK_DIM = 128
CONV_W = 4
CONV_LEFT = 2
LRU_C = 8.0
C_KEY_DIM = 256
C_VAL_DIM = 512
C_GATE_RANK = 16
C_GATE_TAU = 16.0
C_CHUNK = 64

LANES = 128
SUBLANES = 8
VMEM_LIMIT = 56 * 1024 * 1024


def _params(sem, vmem=VMEM_LIMIT):
    return pltpu.CompilerParams(dimension_semantics=sem, vmem_limit_bytes=vmem)


def _dot(a, b):
    return jnp.dot(a, b, preferred_element_type=F32)


def _dot_nt(a, b):
    return lax.dot_general(a, b, (((1,), (1,)), ((), ())), preferred_element_type=F32)


def _dot_tn(a, b):
    return lax.dot_general(a, b, (((0,), (0,)), ((), ())), preferred_element_type=F32)


def _sigmoid(x):
    return 1.0 / (1.0 + jnp.exp(-x))


def _silu(x):
    return x * _sigmoid(x)


def _softplus(x):
    return jnp.maximum(x, 0.0) + jnp.log(1.0 + jnp.exp(-jnp.abs(x)))


def _log_sigmoid(x):
    return -_softplus(-x)


def _gelu_tanh(x):
    return 0.5 * x * (1.0 + jnp.tanh(math.sqrt(2.0 / math.pi) * (x + 0.044715 * (x * x * x))))


def _rms(x):
    return x * lax.rsqrt(jnp.mean(x * x, axis=-1, keepdims=True) + NORM_EPS)


def _norm_mod(x, g, shift, scale):
    return (_rms(x) * g) * (1.0 + scale) + shift


def _pick_tile(limit, *dims):
    t = limit
    while t > SUBLANES and any(d % t for d in dims):
        t //= 2
    assert all(d % t == 0 for d in dims), (limit, dims)
    return t


def _ada_kernel(c_ref, w_ref, b_ref, o_ref):
    s = _silu(c_ref[...]).astype(BF16)
    o_ref[...] = _dot(s, w_ref[...].astype(BF16)) + b_ref[...]


def _ada_mod(cc, ada_w, ada_b):
    depth, d, n = ada_w.shape
    rows = cc.shape[0]
    tn = _pick_tile(1024, n)
    return pl.pallas_call(
        _ada_kernel,
        grid=(depth, n // tn),
        in_specs=[
            pl.BlockSpec((rows, d), lambda l, j: (0, 0)),
            pl.BlockSpec((None, d, tn), lambda l, j: (l, 0, j)),
            pl.BlockSpec((None, 1, tn), lambda l, j: (l, 0, j)),
        ],
        out_specs=pl.BlockSpec((None, rows, tn), lambda l, j: (l, 0, j)),
        out_shape=jax.ShapeDtypeStruct((depth, rows, n), F32),
        name="ada_mod",
        compiler_params=_params(("parallel", "parallel")),
    )(cc, ada_w, ada_b.reshape(depth, 1, n))


class _Rows:
    def __init__(self, batch, seq, ctx_len, tm):
        self.batch, self.seq, self.ctx = batch, seq, ctx_len
        self.tm = tm
        self.n_lat = batch * seq // tm
        self.n_ctx = batch * ctx_len // tm
        self.per_batch = seq // tm
        self.rows_lat = batch * seq
        self.rows_all = batch * (seq + ctx_len)

    def mod_row(self, i):
        return jnp.where(i < self.n_lat, i // self.per_batch, self.batch)

    def rope_row(self, i):
        return jnp.where(i < self.n_lat, i % self.per_batch, self.per_batch)


INPROJ_ROW_CHUNK = 256


def _row_source(refs, n_lat):
    if len(refs) == 1:
        return lambda rs: refs[0][rs, :]
    is_lat = pl.program_id(0) < n_lat
    return lambda rs: jnp.where(is_lat, refs[0][rs, :], refs[1][rs, :])


def _row_specs(h, rows, width):
    tm, n_lat = rows.tm, rows.n_lat
    if not isinstance(h, tuple):
        return [pl.BlockSpec((tm, width), lambda i, *_: (i, 0))], [h]
    return [pl.BlockSpec((tm, width), lambda i, *_: (jnp.minimum(i, n_lat - 1), 0)),
            pl.BlockSpec((tm, width), lambda i, *_: (jnp.maximum(i - n_lat, 0), 0),
                         pipeline_mode=pl.Buffered(1))], list(h)


def _inproj_even_kernel(*refs, n_lat):
    *h_refs, mod_ref, g_ref, w_ref, cos_ref, sin_ref, o_ref, u_ref = refs
    h_rows = _row_source(h_refs, n_lat)
    j = pl.program_id(1)
    tm, tn = o_ref.shape
    rc = min(INPROJ_ROW_CHUNK, tm)

    def step(first):
        w = w_ref[...]
        for r in range(tm // rc):
            rs = slice(r * rc, (r + 1) * rc)
            if first:
                u = _norm_mod(h_rows(rs), g_ref[...], mod_ref[0:1, :], mod_ref[1:2, :]).astype(BF16)
                u_ref[rs, :] = u
            else:
                u = u_ref[rs, :]
            x = _dot(u, w)
            cos, sin = cos_ref[rs, :], sin_ref[rs, :]
            for g in range(tn // LANES):
                xg = x[:, g * LANES:(g + 1) * LANES]
                y = xg * cos + pltpu.roll(xg, LANES // 2, 1) * sin
                o_ref[rs, g * LANES:(g + 1) * LANES] = y.astype(o_ref.dtype)

    @pl.when(j == 0)
    def _():
        step(True)

    @pl.when(j > 0)
    def _():
        step(False)


def _inproj_even(h, mods, norm_g, w, rope, rows, a_width):
    d, n = w.shape
    tm = rows.tm
    tn = _pick_tile(1024, a_width, n)
    cos, sin = rope
    nq, nqk = a_width // tn, 2 * a_width // tn

    def col_type(j):
        return (j >= nq).astype(jnp.int32) + (j >= nqk).astype(jnp.int32)

    tab = pl.BlockSpec((tm, LANES), lambda i, j: (rows.rope_row(i), col_type(j)))
    h_specs, h_args = _row_specs(h, rows, d)
    return pl.pallas_call(
        functools.partial(_inproj_even_kernel, n_lat=rows.n_lat),
        grid=(rows.n_lat + rows.n_ctx, n // tn),
        in_specs=h_specs + [
            pl.BlockSpec((None, N_MOD, d), lambda i, j: (rows.mod_row(i), 0, 0)),
            pl.BlockSpec((1, d), lambda i, j: (0, 0)),
            pl.BlockSpec((d, tn), lambda i, j: (0, j)),
            tab, tab,
        ],
        out_specs=pl.BlockSpec((tm, tn), lambda i, j: (i, j)),
        out_shape=jax.ShapeDtypeStruct((rows.rows_all, n), BF16),
        scratch_shapes=[pltpu.VMEM((tm, d), BF16)],
        name="inproj_even",
        compiler_params=_params(("parallel", "arbitrary")),
    )(*h_args, mods, norm_g.reshape(1, d), w, cos, sin)


def _inproj_odd_kernel(h_ref, mod_ref, g_ref, w_ref, wz_ref, o_ref, z_ref, u_ref, *, nq):
    j = pl.program_id(1)
    tm = o_ref.shape[0]
    rc = min(INPROJ_ROW_CHUNK, tm)
    col_scale = jnp.where(j < nq, C_KEY_DIM ** -0.5, 1.0).astype(F32)

    def step(first):
        w = w_ref[...]
        for r in range(tm // rc):
            rs = slice(r * rc, (r + 1) * rc)
            if first:
                u = _norm_mod(h_ref[rs, :], g_ref[...], mod_ref[0:1, :], mod_ref[1:2, :]).astype(BF16)
                u_ref[rs, :] = u
                zl = lax.broadcasted_iota(jnp.int32, (rc, LANES), 1)
                z_ref[rs, :] = jnp.where(zl < 2 * C_GATE_RANK, _dot(u, wz_ref[...]), 0.0)
            else:
                u = u_ref[rs, :]
            o_ref[rs, :] = (_dot(u, w) * col_scale).astype(o_ref.dtype)

    @pl.when(j == 0)
    def _():
        step(True)

    @pl.when(j > 0)
    def _():
        step(False)


def _inproj_odd(h, mods, norm_g, w, rows, key_w, n):
    d = w.shape[0]
    assert n % LANES == 0 and w.shape[1] == n + 2 * C_GATE_RANK
    tm = rows.tm
    tn = _pick_tile(1024, key_w, n)
    return pl.pallas_call(
        functools.partial(_inproj_odd_kernel, nq=key_w // tn),
        grid=(rows.n_lat + rows.n_ctx, n // tn),
        in_specs=[
            pl.BlockSpec((tm, d), lambda i, j: (i, 0)),
            pl.BlockSpec((None, N_MOD, d), lambda i, j: (rows.mod_row(i), 0, 0)),
            pl.BlockSpec((1, d), lambda i, j: (0, 0)),
            pl.BlockSpec((d, tn), lambda i, j: (0, j)),
            pl.BlockSpec((d, LANES), lambda i, j: (0, n // LANES)),
        ],
        out_specs=[
            pl.BlockSpec((tm, tn), lambda i, j: (i, j)),
            pl.BlockSpec((tm, LANES), lambda i, j: (i, 0)),
        ],
        out_shape=[
            jax.ShapeDtypeStruct((rows.rows_all, n), BF16),
            jax.ShapeDtypeStruct((rows.rows_all, LANES), F32),
        ],
        scratch_shapes=[pltpu.VMEM((tm, d), BF16)],
        name="inproj_odd",
        compiler_params=_params(("parallel", "arbitrary")),
    )(h, mods, norm_g.reshape(1, d), w, w)


def _attn_kernel(lam_ref, ql_ref, qc_ref, kl_ref, vl_ref, kc_ref, vc_ref, g_ref, ol_ref, oc_ref,
                 k_s, v_s, s0_s, s1_s, p0_s, p1_s, *, tq, kt, post_scale):
    ctx_len, seq = kc_ref.shape[0], kl_ref.shape[0]
    total = ctx_len + seq
    nkt, nq, ng = total // kt, seq // tq, kt // LANES
    s_bufs, p_bufs = (s0_s, s1_s), (p0_s, p1_s)
    lam = lam_ref[0]
    gain = g_ref[...] * post_scale

    for src_ref, lo in ((kc_ref, 0), (kl_ref, ctx_len)):
        for c in range(src_ref.shape[0] // LANES):
            k_s[:, lo + c * LANES:lo + (c + 1) * LANES] = src_ref[c * LANES:(c + 1) * LANES, :].T
    v_s[0:ctx_len, 0:LANES] = vc_ref[...]
    v_s[ctx_len:total, 0:LANES] = vl_ref[...]
    v_s[:, LANES:2 * LANES] = jnp.ones((total, LANES), BF16)

    def map_mask(shape, mi):
        lane = lax.broadcasted_iota(jnp.int32, shape, 1)
        return ((lane // (A_HEAD_DIM // 2)) % 2) == mi

    def q_map(t, mi):
        q = ql_ref[pl.ds(pl.multiple_of(t * tq, tq), tq), :]
        return jnp.where(map_mask(q.shape, mi), q, jnp.zeros_like(q))

    def scores(qa, buf, j, m):
        s = _dot(qa, k_s[:, j * kt:(j + 1) * kt])
        s_bufs[buf][:, j * kt:(j + 1) * kt] = s
        for g in range(ng):
            m = jnp.maximum(m, s[:, g * LANES:(g + 1) * LANES])
        return m

    def exps(buf, j, mb):
        x = s_bufs[buf][:, j * kt:(j + 1) * kt]
        ps = [jnp.exp2(x[:, g * LANES:(g + 1) * LANES] - mb).astype(BF16) for g in range(ng)]
        p_bufs[buf][:, j * kt:(j + 1) * kt] = ps[0] if ng == 1 else jnp.concatenate(ps, axis=1)

    def values(buf):
        return _dot(p_bufs[buf][...], v_s[...])

    def row_max(m):
        return jnp.broadcast_to(jnp.max(m, axis=-1, keepdims=True), m.shape)

    def finish(t, r0, r1):
        o = r0[:, 0:LANES] / r0[:, LANES:2 * LANES] - lam * (r1[:, 0:LANES] / r1[:, LANES:2 * LANES])
        ol_ref[pl.ds(pl.multiple_of(t * tq, tq), tq), :] = (_rms(o) * gain).astype(ol_ref.dtype)

    neg = jnp.full((tq, LANES), -jnp.inf, F32)

    qa = q_map(0, 0)
    m = neg
    for j in range(nkt):
        m = scores(qa, 0, j, m)
    mx0 = row_max(m)
    qb = q_map(0, 1)
    m = neg
    for j in range(nkt):
        m = scores(qb, 1, j, m)
        exps(0, j, mx0)

    def tile_body(t, carry):
        m1, r0p, r1p = carry
        finish(jnp.maximum(t - 1, 0), r0p, r1p)
        mx1 = row_max(m1)
        tn = jnp.minimum(t + 1, nq - 1)
        qa = q_map(tn, 0)
        m = neg
        for j in range(nkt):
            m = scores(qa, 0, j, m)
        mx0n = row_max(m)
        for j in range(nkt):
            exps(1, j, mx1)
        r0 = values(0)
        r1 = values(1)
        qb = q_map(tn, 1)
        m = neg
        for j in range(nkt):
            m = scores(qb, 1, j, m)
            exps(0, j, mx0n)
        return m, r0, r1

    ones = jnp.ones((tq, 2 * LANES), F32)
    _, r0, r1 = lax.fori_loop(0, nq, tile_body, (m, ones, ones))
    finish(nq - 1, r0, r1)

    qc = qc_ref[...]
    outs = []
    for mi in range(2):
        qm = jnp.where(map_mask(qc.shape, mi), qc, jnp.zeros_like(qc))
        s = _dot_nt(qm, kc_ref[...])
        p = jnp.exp2(s - jnp.max(s, axis=-1, keepdims=True))
        outs.append(_dot(p.astype(BF16), vc_ref[...]) / jnp.sum(p, axis=-1, keepdims=True))
    oc_ref[...] = (_rms(outs[0] - lam * outs[1]) * gain).astype(oc_ref.dtype)


def _diff_attn(p, lam, subln_g, post_scale, batch, seq, ctx_len, heads):
    total = seq + ctx_len
    tq = _pick_tile(256, seq)
    kt = _pick_tile(256, total)
    nkt = total // kt
    n_ctx_blk0 = batch * seq // ctx_len
    kern = functools.partial(_attn_kernel, tq=tq, kt=kt, post_scale=post_scale)

    def lat(off):
        return pl.BlockSpec((seq, LANES), lambda b, h: (b, off + h))

    def ctx(off):
        return pl.BlockSpec((ctx_len, LANES), lambda b, h: (n_ctx_blk0 + b, off + h))

    return pl.pallas_call(
        kern,
        grid=(batch, heads),
        in_specs=[
            pl.BlockSpec(memory_space=pltpu.SMEM),
            lat(0), ctx(0), lat(heads), lat(2 * heads), ctx(heads), ctx(2 * heads),
            pl.BlockSpec((1, LANES), lambda b, h: (0, 0)),
        ],
        out_specs=[
            pl.BlockSpec((seq, LANES), lambda b, h: (b, h)),
            pl.BlockSpec((ctx_len, LANES), lambda b, h: (b, h)),
        ],
        out_shape=[
            jax.ShapeDtypeStruct((batch * seq, heads * LANES), BF16),
            jax.ShapeDtypeStruct((batch * ctx_len, heads * LANES), BF16),
        ],
        scratch_shapes=[
            pltpu.VMEM((LANES, total), BF16),
            pltpu.VMEM((total, 2 * LANES), BF16),
            pltpu.VMEM((tq, total), F32),
            pltpu.VMEM((tq, total), F32),
            pltpu.VMEM((tq, total), BF16),
            pltpu.VMEM((tq, total), BF16),
        ],
        name="diff_attn",
        compiler_params=_params(("parallel", "parallel")),
    )(lam, p, p, p, p, p, p, subln_g.reshape(1, LANES))


def _dwconv(x, w, b):
    n = x.shape[0]
    row = lax.broadcasted_iota(jnp.int32, x.shape, 0)
    y = b + x * w[CONV_LEFT:CONV_LEFT + 1, :]
    for j in range(CONV_W):
        off = j - CONV_LEFT
        if off == 0:
            continue
        xs = pltpu.roll(x, (-off) % n, 0)
        ok = jnp.logical_and(row + off >= 0, row + off < n)
        y = y + jnp.where(ok, xs, 0.0) * w[j:j + 1, :]
    return y


def _lru_kernel(recl_ref, recc_ref, gtl_ref, gtc_ref, cw_ref, cb_ref, wa_ref, ba_ref, wx_ref, bx_ref,
                lam_ref, ol_ref, oc_ref, a_s, b_s, h_s, *, ctx_len, seq):
    total = ctx_len + seq
    nt = total // SUBLANES
    cw = cw_ref[...]
    cb = cb_ref[...]
    region_off = ((0, seq), (ctx_len, 0))
    decay_rate = [LRU_C * _softplus(-lam_ref[d]) for d in range(2)]
    for ri, rec_ref in enumerate((recc_ref, recl_ref)):
        n = rec_ref.shape[0]
        xc = _dwconv(rec_ref[...].astype(F32), cw, cb)
        xcb = xc.astype(BF16)
        pre = [(_dot(xcb, wa_ref[d].astype(BF16)), _dot(xcb, wx_ref[d].astype(BF16))) for d in range(2)]
        for d in range(2):
            r = _sigmoid(pre[d][0] + ba_ref[d])
            gi = _sigmoid(pre[d][1] + bx_ref[d])
            a = jnp.exp(-(r * decay_rate[d]))
            lo = region_off[ri][d]
            a_s[d, lo:lo + n, :] = a
            b_s[d, lo:lo + n, :] = jnp.sqrt(1.0 - a * a) * gi * xc

    row = lax.broadcasted_iota(jnp.int32, (SUBLANES, LANES), 0)

    def scan_tile(a, b, reverse):
        s = 1
        while s < SUBLANES:
            sh = (SUBLANES - s) if reverse else s
            a_sh = pltpu.roll(a, sh, 0)
            b_sh = pltpu.roll(b, sh, 0)
            ok = (row < SUBLANES - s) if reverse else (row >= s)
            b = jnp.where(ok, a * b_sh + b, b)
            a = jnp.where(ok, a * a_sh, a)
            s *= 2
        return a, b

    def body(p, carry):
        cf, cbk = carry
        tf = pl.multiple_of(p * SUBLANES, SUBLANES)
        a, b = scan_tile(a_s[0, pl.ds(tf, SUBLANES), :], b_s[0, pl.ds(tf, SUBLANES), :], False)
        hf = a * cf + b
        h_s[0, pl.ds(tf, SUBLANES), :] = hf
        tb = pl.multiple_of((nt - 1 - p) * SUBLANES, SUBLANES)
        a, b = scan_tile(a_s[1, pl.ds(tb, SUBLANES), :], b_s[1, pl.ds(tb, SUBLANES), :], True)
        hb = a * cbk + b
        h_s[1, pl.ds(tb, SUBLANES), :] = hb
        return hf[SUBLANES - 1:SUBLANES, :], hb[0:1, :]

    zero = jnp.zeros((1, LANES), F32)
    lax.fori_loop(0, nt, body, (zero, zero), unroll=4)

    oc_ref[...] = (_gelu_tanh(gtc_ref[...].astype(F32))
                   * (h_s[0, 0:ctx_len, :] + h_s[1, seq:total, :])).astype(oc_ref.dtype)
    ol_ref[...] = (_gelu_tanh(gtl_ref[...].astype(F32))
                   * (h_s[0, ctx_len:total, :] + h_s[1, 0:seq, :])).astype(ol_ref.dtype)


def _rglru(p, conv_w, conv_b, wa, ba, wx, bx, lru_lam, batch, seq, ctx_len, a_width, b_width):
    nb = b_width // B_BLOCK_DIM
    gt_off = 3 * a_width // LANES
    rec_off = (3 * a_width + b_width) // LANES
    n_ctx_blk0 = batch * seq // ctx_len
    total = seq + ctx_len
    kern = functools.partial(_lru_kernel, ctx_len=ctx_len, seq=seq)
    vec = pl.BlockSpec((2, 1, LANES), lambda b, n: (0, 0, n))
    mat = pl.BlockSpec((2, None, B_BLOCK_DIM, B_BLOCK_DIM), lambda b, n: (0, n, 0, 0))
    return pl.pallas_call(
        kern,
        grid=(batch, nb),
        in_specs=[
            pl.BlockSpec((seq, LANES), lambda b, n: (b, rec_off + n)),
            pl.BlockSpec((ctx_len, LANES), lambda b, n: (n_ctx_blk0 + b, rec_off + n)),
            pl.BlockSpec((seq, LANES), lambda b, n: (b, gt_off + n)),
            pl.BlockSpec((ctx_len, LANES), lambda b, n: (n_ctx_blk0 + b, gt_off + n)),
            pl.BlockSpec((CONV_W, LANES), lambda b, n: (0, n)),
            pl.BlockSpec((1, LANES), lambda b, n: (0, n)),
            mat, vec, mat, vec, vec,
        ],
        out_specs=[
            pl.BlockSpec((seq, LANES), lambda b, n: (b, n)),
            pl.BlockSpec((ctx_len, LANES), lambda b, n: (b, n)),
        ],
        out_shape=[
            jax.ShapeDtypeStruct((batch * seq, b_width), BF16),
            jax.ShapeDtypeStruct((batch * ctx_len, b_width), BF16),
        ],
        scratch_shapes=[
            pltpu.VMEM((2, total, LANES), F32),
            pltpu.VMEM((2, total, LANES), F32),
            pltpu.VMEM((2, total, LANES), F32),
        ],
        name="rglru",
        compiler_params=_params(("parallel", "parallel")),
    )(p, p, p, p, conv_w, conv_b.reshape(1, b_width), wa, ba.reshape(2, 1, b_width),
      wx, bx.reshape(2, 1, b_width), lru_lam.reshape(2, 1, b_width))


GLA_UNROLL = 4


def _gla_kernel(ql_ref, kl_ref, vl_ref, rl_ref, zl_ref, qc_ref, kc_ref, vc_ref, rc_ref, zc_ref,
                w2_ref, gb_ref, g_ref, al_ref, ac_ref, st_s, ol_s, oc_s, *, ctx_len, seq):
    c = C_CHUNK
    ri = lax.broadcasted_iota(jnp.int32, (c, c), 0)
    ci = lax.broadcasted_iota(jnp.int32, (c, c), 1)
    keep = (ri >= ci, ri <= ci)
    tri = tuple(jnp.where(k, 1.0, 0.0).astype(BF16) for k in keep)
    end_row = (c - 1, 0)

    def step(q_ref, k_ref, v_ref, z_ref, o_ref, chains, accumulate):
        rows = [pl.ds(t0, c) for t0, _ in chains]
        dirs = [d for _, d in chains]
        xs = [_dot(z_ref[r, :].astype(BF16), w2_ref[d]) + gb_ref[d] for r, d in zip(rows, dirs)]
        gs = [_log_sigmoid(x) * (1.0 / C_GATE_TAU) for x in xs]
        his = [g.astype(BF16) for g in gs]
        los = [(g - hi.astype(F32)).astype(BF16) for g, hi in zip(gs, his)]
        bcums = [_dot(tri[d], hi) + _dot(tri[d], lo) for d, hi, lo in zip(dirs, his, los)]
        qes, kes, kds, decays = [], [], [], []
        for r, d, bcum in zip(rows, dirs, bcums):
            b_end = bcum[end_row[d]:end_row[d] + 1, :]
            k = k_ref[r, :].astype(F32)
            qes.append((q_ref[r, :].astype(F32) * jnp.exp(bcum)).astype(BF16))
            kes.append((k * jnp.exp(-bcum)).astype(BF16))
            kds.append((k * jnp.exp(b_end - bcum)).astype(BF16))
            decays.append(jnp.exp(b_end))
        raw = [_dot_nt(qe, ke) for qe, ke in zip(qes, kes)]
        ups = [_dot_tn(v_ref[r, :], kd) for r, kd in zip(rows, kds)]
        atts = [jnp.where(keep[d], a, 0.0).astype(BF16) for d, a in zip(dirs, raw)]
        intra = [_dot(att, v_ref[r, :]) for att, r in zip(atts, rows)]
        for r, d, qe, up, decay, oi in zip(rows, dirs, qes, ups, decays, intra):
            st = st_s[d]
            o = oi + _dot_nt(qe, st.astype(BF16))
            st_s[d] = st * decay + up
            if accumulate:
                o_ref[r, :] += o
            else:
                o_ref[r, :] = o

    def sweep(q_ref, k_ref, v_ref, z_ref, o_ref, n):
        half = n // 2
        unroll = math.gcd(GLA_UNROLL, half)

        def run(lo, accumulate):
            def body(i, carry):
                chains = []
                for u in range(unroll):
                    ci_ = i * unroll + u
                    chains.append((pl.multiple_of(ci_ * c, c), 0))
                    chains.append((pl.multiple_of((n - 1 - ci_) * c, c), 1))
                step(q_ref, k_ref, v_ref, z_ref, o_ref, chains, accumulate)
                return carry
            lax.fori_loop(lo // unroll, (lo + half) // unroll, body, 0)

        run(0, False)
        run(half, True)

    st_s[...] = jnp.zeros_like(st_s)
    sweep(qc_ref, kc_ref, vc_ref, zc_ref, oc_s, ctx_len // c)
    sweep(ql_ref, kl_ref, vl_ref, zl_ref, ol_s, seq // c)

    g = g_ref[...]
    ac_ref[...] = (_rms(oc_s[...]) * g * _silu(rc_ref[...].astype(F32))).astype(ac_ref.dtype)
    al_ref[...] = (_rms(ol_s[...]) * g * _silu(rl_ref[...].astype(F32))).astype(al_ref.dtype)


def _gla(p, z, w2p, gate_b, norm_g, batch, seq, ctx_len, heads):
    d_model = heads * C_VAL_DIM
    key_w = heads * C_KEY_DIM
    n_ctx_blk0 = batch * seq // ctx_len
    kern = functools.partial(_gla_kernel, ctx_len=ctx_len, seq=seq)

    def lat(width, off):
        return pl.BlockSpec((seq, width), lambda b, h: (b, off + h))

    def ctx(width, off):
        return pl.BlockSpec((ctx_len, width), lambda b, h: (n_ctx_blk0 + b, off + h))

    specs = [
        lat(C_KEY_DIM, 0), lat(C_KEY_DIM, heads), lat(C_VAL_DIM, heads), lat(C_VAL_DIM, 2 * heads),
        pl.BlockSpec((seq, LANES), lambda b, h: (b, 0)),
        ctx(C_KEY_DIM, 0), ctx(C_KEY_DIM, heads), ctx(C_VAL_DIM, heads), ctx(C_VAL_DIM, 2 * heads),
        pl.BlockSpec((ctx_len, LANES), lambda b, h: (n_ctx_blk0 + b, 0)),
    ]
    specs += [
        pl.BlockSpec((2, LANES, C_KEY_DIM), lambda b, h: (0, 0, h)),
        pl.BlockSpec((2, 1, C_KEY_DIM), lambda b, h: (0, 0, h)),
        pl.BlockSpec((1, C_VAL_DIM), lambda b, h: (0, 0)),
    ]
    return pl.pallas_call(
        kern,
        grid=(batch, heads),
        in_specs=specs,
        out_specs=[
            pl.BlockSpec((seq, C_VAL_DIM), lambda b, h: (b, h)),
            pl.BlockSpec((ctx_len, C_VAL_DIM), lambda b, h: (b, h)),
        ],
        out_shape=[
            jax.ShapeDtypeStruct((batch * seq, d_model), BF16),
            jax.ShapeDtypeStruct((batch * ctx_len, d_model), BF16),
        ],
        scratch_shapes=[
            pltpu.VMEM((2, C_VAL_DIM, C_KEY_DIM), F32),
            pltpu.VMEM((seq, C_VAL_DIM), F32),
            pltpu.VMEM((ctx_len, C_VAL_DIM), F32),
        ],
        name="gla",
        compiler_params=_params(("parallel", "parallel")),
    )(p, p, p, p, z, p, p, p, p, z, w2p, gate_b.reshape(2, 1, key_w), norm_g.reshape(1, C_VAL_DIM))


def _outproj_kernel(al0_ref, ac0_ref, al1_ref, ac1_ref, w0_ref, w1_ref, *refs, n_lat):
    *h_refs, mod_ref, o_ref = refs
    is_lat = pl.program_id(0) < n_lat
    a0 = jnp.where(is_lat, al0_ref[...], ac0_ref[...])
    a1 = jnp.where(is_lat, al1_ref[...], ac1_ref[...])
    acc = _dot(a0, w0_ref[...]) + _dot(a1, w1_ref[...])
    o_ref[...] = _row_source(h_refs, n_lat)(slice(None)) + mod_ref[2:3, :] * acc


def _outproj(srcs, w, h, mods, rows, n_tiles):
    d = w.shape[1]
    kh = w.shape[0] // 2
    tm = rows.tm
    n_lat = rows.n_lat
    specs, args = [], []
    for lat_arr, lat_col, ctx_arr, ctx_row0, ctx_col in srcs:
        specs.append(pl.BlockSpec((tm, kh), lambda i, c=lat_col: (jnp.minimum(i, n_lat - 1), c)))
        specs.append(pl.BlockSpec((tm, kh), lambda i, r=ctx_row0, c=ctx_col: (r + jnp.maximum(i - n_lat, 0), c)))
        args += [lat_arr, ctx_arr]
    kern = functools.partial(_outproj_kernel, n_lat=n_lat)
    h_specs, h_args = _row_specs(h, rows, d)
    return pl.pallas_call(
        kern,
        grid=(n_tiles,),
        in_specs=specs + [
            pl.BlockSpec((kh, d), lambda i: (0, 0)),
            pl.BlockSpec((kh, d), lambda i: (1, 0)),
        ] + h_specs + [
            pl.BlockSpec((None, N_MOD, d), lambda i: (rows.mod_row(i), 0, 0)),
        ],
        out_specs=pl.BlockSpec((tm, d), lambda i: (i, 0)),
        out_shape=jax.ShapeDtypeStruct((n_tiles * tm, d), F32),
        name="outproj",
        compiler_params=_params(("parallel",)),
    )(*args, w, w, *h_args, mods)


MLP_ROW_CHUNK = 512


def _mlp_kernel(h_ref, mod_ref, g_ref, w1_ref, w2_ref, fg_ref, o_ref, u_ref, *, nf, final_norm):
    j = pl.program_id(1)
    tm = o_ref.shape[0]
    rc = min(MLP_ROW_CHUNK, tm)

    def step(first):
        w1, w2 = w1_ref[...], w2_ref[...]
        for r in range(tm // rc):
            rs = slice(r * rc, (r + 1) * rc)
            if first:
                u = _norm_mod(h_ref[rs, :], g_ref[...], mod_ref[3:4, :], mod_ref[4:5, :]).astype(BF16)
                u_ref[rs, :] = u
            else:
                u = u_ref[rs, :]
            t = jnp.maximum(_dot(u, w1), 0.0)
            part = _dot((t * t).astype(BF16), w2)
            if first:
                o_ref[rs, :] = part
            else:
                o_ref[rs, :] += part

    @pl.when(j == 0)
    def _():
        step(True)

    @pl.when(j > 0)
    def _():
        step(False)

    @pl.when(j == nf - 1)
    def _():
        y = h_ref[...] + mod_ref[5:6, :] * o_ref[...]
        if final_norm:
            y = _rms(y) * fg_ref[...]
        o_ref[...] = y


def _mlp(h, mods, norm_g, w1, w2, final_g, rows, n_tiles, tf, final_norm):
    d, ff = w1.shape
    nf = ff // tf
    tm = rows.tm
    kern = functools.partial(_mlp_kernel, nf=nf, final_norm=final_norm)
    return pl.pallas_call(
        kern,
        grid=(n_tiles, nf),
        in_specs=[
            pl.BlockSpec((tm, d), lambda i, j: (i, 0)),
            pl.BlockSpec((None, N_MOD, d), lambda i, j: (rows.mod_row(i), 0, 0)),
            pl.BlockSpec((1, d), lambda i, j: (0, 0)),
            pl.BlockSpec((d, tf), lambda i, j: (0, j)),
            pl.BlockSpec((tf, d), lambda i, j: (j, 0)),
            pl.BlockSpec((1, d), lambda i, j: (0, 0)),
        ],
        out_specs=pl.BlockSpec((tm, d), lambda i, j: (i, 0)),
        out_shape=jax.ShapeDtypeStruct((n_tiles * tm, d), F32),
        scratch_shapes=[pltpu.VMEM((tm, d), BF16)],
        name="mlp",
        compiler_params=_params(("parallel", "arbitrary")),
    )(h, mods, norm_g.reshape(1, d), w1, w2, final_g.reshape(1, d))


def _rope_tables(seq, pad_rows):
    inv = ROPE_BASE ** (-jnp.arange(ROPE_FREQS, dtype=F32) / ROPE_FREQS)
    t = jnp.arange(seq, dtype=jnp.int32)
    pos_r = (t // GRID_W).astype(F32)[:, None]
    pos_c = (t % GRID_W).astype(F32)[:, None]
    lane = jnp.arange(LANES, dtype=jnp.int32)[None, :]
    second = (lane // A_HEAD_DIM) == 1
    is_col = ((lane % (2 * ROPE_FREQS)) // ROPE_FREQS) == 1
    ang = jnp.where(is_col, pos_c, pos_r) * inv[lane % ROPE_FREQS]
    cos = jnp.cos(ang)
    sin = jnp.where(second, jnp.sin(ang), -jnp.sin(ang))
    qs = A_HEAD_DIM ** -0.5 * math.log2(math.e)
    ones = jnp.ones((pad_rows, LANES), F32)
    zeros = jnp.zeros((pad_rows, LANES), F32)
    one_all = jnp.ones((seq + pad_rows, LANES), F32)
    zero_all = jnp.zeros((seq + pad_rows, LANES), F32)

    def table(rot, pad):
        body = jnp.concatenate([rot, pad])
        return jnp.concatenate([qs * body, body, one_all if pad is ones else zero_all], axis=1)

    return table(cos, ones), table(sin, zeros)


def _permute_qk_columns(w, a_width):
    d = w.shape[0]
    qk = w[:, :2 * a_width].reshape(d, -1, 2, 2, 2, ROPE_FREQS).transpose(0, 1, 4, 2, 3, 5)
    return jnp.concatenate([qk.reshape(d, 2 * a_width), w[:, 2 * a_width:]], axis=1)


def kernel(x, c, ctx, c_ctx, ada_w, ada_b, norm1_g, norm2_g, mlp_w1, mlp_w2, ev_w_in, ev_w_out, ev_lambda_q1, ev_lambda_k1, ev_lambda_q2, ev_lambda_k2, ev_subln_g, ev_conv_w, ev_conv_b, ev_lru_wa, ev_lru_ba, ev_lru_wx, ev_lru_bx, ev_lru_lam, od_w_in, od_w_out, od_gate_w2, od_gate_b, od_norm_g, final_g):
    batch, seq, d = x.shape
    ctx_len = ctx.shape[1]
    depth = ada_w.shape[0]
    a_width = d // 2
    b_width = d - a_width
    a_heads = a_width // (2 * A_HEAD_DIM)
    c_heads = d // C_VAL_DIM
    key_w = c_heads * C_KEY_DIM

    tm = _pick_tile(1024, seq, batch * ctx_len)
    rows = _Rows(batch, seq, ctx_len, tm)
    rows_out = _Rows(batch, seq, ctx_len, _pick_tile(512, seq, batch * ctx_len))
    tf = _pick_tile(512, mlp_w1.shape[2])

    pad = (-(batch + 1)) % SUBLANES
    cc = jnp.concatenate([c, c_ctx[None, :], jnp.zeros((pad, d), F32)], axis=0)
    mods_all = _ada_mod(cc, ada_w, ada_b).reshape(depth, batch + 1 + pad, N_MOD, d)

    rope = _rope_tables(seq, tm)
    h = (x.reshape(batch * seq, d), ctx.reshape(batch * ctx_len, d))

    for i in range(depth):
        last = i == depth - 1
        mods = mods_all[i]
        j = i // 2
        if i % 2 == 0:
            lam_init = 0.8 - 0.6 * math.exp(-0.3 * i)
            lam = (jnp.exp(jnp.sum(ev_lambda_q1[j] * ev_lambda_k1[j]))
                   - jnp.exp(jnp.sum(ev_lambda_q2[j] * ev_lambda_k2[j])) + lam_init).reshape(1).astype(F32)
            w_in = _permute_qk_columns(ev_w_in[j], a_width).astype(BF16)
            p = _inproj_even(h, mods, norm1_g[i], w_in, rope, rows, a_width)
            ya_lat, ya_ctx = _diff_attn(p, lam, ev_subln_g[j], 1.0 - lam_init, batch, seq, ctx_len, a_heads)
            yb_lat, yb_ctx = _rglru(p, ev_conv_w[j], ev_conv_b[j], ev_lru_wa[j], ev_lru_ba[j],
                                    ev_lru_wx[j], ev_lru_bx[j], ev_lru_lam[j],
                                    batch, seq, ctx_len, a_width, b_width)
            srcs = [(ya_lat, 0, ya_ctx, 0, 0), (yb_lat, 0, yb_ctx, 0, 0)]
            w_out = ev_w_out[j]
        else:
            w_in = od_w_in[j]
            n_main = 2 * key_w + 2 * d
            p, z = _inproj_odd(h, mods, norm1_g[i], w_in.astype(BF16), rows, key_w, n_main)
            w2p = jnp.zeros((2, LANES, key_w), F32)
            for dd in range(2):
                w2p = w2p.at[dd, dd * C_GATE_RANK:(dd + 1) * C_GATE_RANK, :].set(od_gate_w2[j, dd])
            a_lat, a_ctx = _gla(p, z, w2p.astype(BF16), od_gate_b[j], od_norm_g[j], batch, seq, ctx_len, c_heads)
            srcs = [(a_lat, 0, a_ctx, 0, 0), (a_lat, 1, a_ctx, 0, 1)]
            w_out = od_w_out[j]

        n_out = rows_out.n_lat if last else rows_out.n_lat + rows_out.n_ctx
        h = _outproj(srcs, w_out.astype(BF16), h, mods, rows_out, n_out)
        n_mlp = rows.n_lat if last else rows.n_lat + rows.n_ctx
        h = _mlp(h, mods, norm2_g[i], mlp_w1[i].astype(BF16), mlp_w2[i].astype(BF16), final_g,
                 rows, n_mlp, tf, last)

    return h.reshape(batch, seq, d)
```

```python
import functools
import math

import jax
import jax.numpy as jnp
from jax import lax
from jax.experimental import pallas as pl
from jax.experimental.pallas import tpu as pltpu

F32 = jnp.float32
BF16 = jnp.bfloat16

NORM_EPS = 1e-6
N_MOD = 6
GRID_W = 64
A_HEAD_DIM = 64
ROPE_FREQS = A_HEAD_DIM // 4
ROPE_BASE = 10000.0
B_BLOCK_DIM = 128
CONV_W = 4
CONV_LEFT = 2
LRU_C = 8.0
C_KEY_DIM = 256
C_VAL_DIM = 512
C_GATE_RANK = 16
C_GATE_TAU = 16.0
C_CHUNK = 64

LANES = 128
SUBLANES = 8
VMEM_LIMIT = 56 * 1024 * 1024


def _params(sem, vmem=VMEM_LIMIT):
    return pltpu.CompilerParams(dimension_semantics=sem, vmem_limit_bytes=vmem)


def _dot(a, b):
    return jnp.dot(a, b, preferred_element_type=F32)


def _dot_nt(a, b):
    return lax.dot_general(a, b, (((1,), (1,)), ((), ())), preferred_element_type=F32)


def _dot_tn(a, b):
    return lax.dot_general(a, b, (((0,), (0,)), ((), ())), preferred_element_type=F32)


def _sigmoid(x):
    return 1.0 / (1.0 + jnp.exp(-x))


def _silu(x):
    return x * _sigmoid(x)


def _softplus(x):
    return jnp.maximum(x, 0.0) + jnp.log(1.0 + jnp.exp(-jnp.abs(x)))


def _log_sigmoid(x):
    return -_softplus(-x)


def _gelu_tanh(x):
    return 0.5 * x * (1.0 + jnp.tanh(math.sqrt(2.0 / math.pi) * (x + 0.044715 * (x * x * x))))


def _rms(x):
    return x * lax.rsqrt(jnp.mean(x * x, axis=-1, keepdims=True) + NORM_EPS)


def _norm_mod(x, g, shift, scale):
    return (_rms(x) * g) * (1.0 + scale) + shift


def _pick_tile(limit, *dims):
    t = limit
    while t > SUBLANES and any(d % t for d in dims):
        t //= 2
    assert all(d % t == 0 for d in dims), (limit, dims)
    return t


def _ada_kernel(c_ref, w_ref, b_ref, o_ref):
    s = _silu(c_ref[...]).astype(BF16)
    o_ref[...] = _dot(s, w_ref[...].astype(BF16)) + b_ref[...]


def _ada_mod(cc, ada_w, ada_b):
    depth, d, n = ada_w.shape
    rows = cc.shape[0]
    tn = _pick_tile(1024, n)
    return pl.pallas_call(
        _ada_kernel,
        grid=(depth, n // tn),
        in_specs=[
            pl.BlockSpec((rows, d), lambda l, j: (0, 0)),
            pl.BlockSpec((None, d, tn), lambda l, j: (l, 0, j)),
            pl.BlockSpec((None, 1, tn), lambda l, j: (l, 0, j)),
        ],
        out_specs=pl.BlockSpec((None, rows, tn), lambda l, j: (l, 0, j)),
        out_shape=jax.ShapeDtypeStruct((depth, rows, n), F32),
        name="ada_mod",
        compiler_params=_params(("parallel", "parallel")),
    )(cc, ada_w, ada_b.reshape(depth, 1, n))


class _Rows:
    def __init__(self, batch, seq, ctx_len, tm):
        self.batch, self.seq, self.ctx = batch, seq, ctx_len
        self.tm = tm
        self.n_lat = batch * seq // tm
        self.n_ctx = batch * ctx_len // tm
        self.per_batch = seq // tm
        self.rows_lat = batch * seq
        self.rows_all = batch * (seq + ctx_len)

    def mod_row(self, i):
        return jnp.where(i < self.n_lat, i // self.per_batch, self.batch)

    def rope_row(self, i):
        return jnp.where(i < self.n_lat, i % self.per_batch, self.per_batch)


INPROJ_ROW_CHUNK = 256


def _row_source(refs, n_lat):
    if len(refs) == 1:
        return lambda rs: refs[0][rs, :]
    is_lat = pl.program_id(0) < n_lat
    return lambda rs: jnp.where(is_lat, refs[0][rs, :], refs[1][rs, :])


def _row_specs(h, rows, width):
    tm, n_lat = rows.tm, rows.n_lat
    if not isinstance(h, tuple):
        return [pl.BlockSpec((tm, width), lambda i, *_: (i, 0))], [h]
    return [pl.BlockSpec((tm, width), lambda i, *_: (jnp.minimum(i, n_lat - 1), 0)),
            pl.BlockSpec((tm, width), lambda i, *_: (jnp.maximum(i - n_lat, 0), 0),
                         pipeline_mode=pl.Buffered(1))], list(h)


def _inproj_even_kernel(*refs, n_lat):
    *h_refs, mod_ref, g_ref, w_ref, cos_ref, sin_ref, o_ref, u_ref = refs
    h_rows = _row_source(h_refs, n_lat)
    j = pl.program_id(1)
    tm, tn = o_ref.shape
    rc = min(INPROJ_ROW_CHUNK, tm)

    def step(first):
        w = w_ref[...]
        for r in range(tm // rc):
            rs = slice(r * rc, (r + 1) * rc)
            if first:
                u = _norm_mod(h_rows(rs), g_ref[...], mod_ref[0:1, :], mod_ref[1:2, :]).astype(BF16)
                u_ref[rs, :] = u
            else:
                u = u_ref[rs, :]
            x = _dot(u, w)
            cos, sin = cos_ref[rs, :], sin_ref[rs, :]
            for g in range(tn // LANES):
                xg = x[:, g * LANES:(g + 1) * LANES]
                y = xg * cos + pltpu.roll(xg, LANES // 2, 1) * sin
                o_ref[rs, g * LANES:(g + 1) * LANES] = y.astype(o_ref.dtype)

    @pl.when(j == 0)
    def _():
        step(True)

    @pl.when(j > 0)
    def _():
        step(False)


def _inproj_even(h, mods, norm_g, w, layer, rope, rows, a_width):
    _, d, n = w.shape
    tm = rows.tm
    tn = _pick_tile(1024, a_width, n)
    cos, sin = rope
    nq, nqk = a_width // tn, 2 * a_width // tn

    def col_type(j):
        return (j >= nq).astype(jnp.int32) + (j >= nqk).astype(jnp.int32)

    tab = pl.BlockSpec((tm, LANES), lambda i, j: (rows.rope_row(i), col_type(j)))
    h_specs, h_args = _row_specs(h, rows, d)
    return pl.pallas_call(
        functools.partial(_inproj_even_kernel, n_lat=rows.n_lat),
        grid=(rows.n_lat + rows.n_ctx, n // tn),
        in_specs=h_specs + [
            pl.BlockSpec((None, N_MOD, d), lambda i, j: (rows.mod_row(i), 0, 0)),
            pl.BlockSpec((1, d), lambda i, j: (0, 0)),
            pl.BlockSpec((None, d, tn), lambda i, j: (layer, 0, j)),
            tab, tab,
        ],
        out_specs=pl.BlockSpec((tm, tn), lambda i, j: (i, j)),
        out_shape=jax.ShapeDtypeStruct((rows.rows_all, n), BF16),
        scratch_shapes=[pltpu.VMEM((tm, d), BF16)],
        name="inproj_even",
        compiler_params=_params(("parallel", "arbitrary")),
    )(*h_args, mods, norm_g.reshape(1, d), w, cos, sin)


def _inproj_odd_kernel(h_ref, mod_ref, g_ref, w_ref, wz_ref, o_ref, z_ref, u_ref, *, nq):
    j = pl.program_id(1)
    tm = o_ref.shape[0]
    rc = min(INPROJ_ROW_CHUNK, tm)
    col_scale = jnp.where(j < nq, C_KEY_DIM ** -0.5, 1.0).astype(F32)

    def step(first):
        w = w_ref[...]
        for r in range(tm // rc):
            rs = slice(r * rc, (r + 1) * rc)
            if first:
                u = _norm_mod(h_ref[rs, :], g_ref[...], mod_ref[0:1, :], mod_ref[1:2, :]).astype(BF16)
                u_ref[rs, :] = u
                zl = lax.broadcasted_iota(jnp.int32, (rc, LANES), 1)
                z_ref[rs, :] = jnp.where(zl < 2 * C_GATE_RANK, _dot(u, wz_ref[...]), 0.0)
            else:
                u = u_ref[rs, :]
            o_ref[rs, :] = (_dot(u, w) * col_scale).astype(o_ref.dtype)

    @pl.when(j == 0)
    def _():
        step(True)

    @pl.when(j > 0)
    def _():
        step(False)


def _inproj_odd(h, mods, norm_g, w, layer, rows, key_w, n):
    d = w.shape[1]
    assert n % LANES == 0 and w.shape[2] == n + 2 * C_GATE_RANK
    tm = rows.tm
    tn = _pick_tile(1024, key_w, n)
    return pl.pallas_call(
        functools.partial(_inproj_odd_kernel, nq=key_w // tn),
        grid=(rows.n_lat + rows.n_ctx, n // tn),
        in_specs=[
            pl.BlockSpec((tm, d), lambda i, j: (i, 0)),
            pl.BlockSpec((None, N_MOD, d), lambda i, j: (rows.mod_row(i), 0, 0)),
            pl.BlockSpec((1, d), lambda i, j: (0, 0)),
            pl.BlockSpec((None, d, tn), lambda i, j: (layer, 0, j)),
            pl.BlockSpec((None, d, LANES), lambda i, j: (layer, 0, n // LANES)),
        ],
        out_specs=[
            pl.BlockSpec((tm, tn), lambda i, j: (i, j)),
            pl.BlockSpec((tm, LANES), lambda i, j: (i, 0)),
        ],
        out_shape=[
            jax.ShapeDtypeStruct((rows.rows_all, n), BF16),
            jax.ShapeDtypeStruct((rows.rows_all, LANES), F32),
        ],
        scratch_shapes=[pltpu.VMEM((tm, d), BF16)],
        name="inproj_odd",
        compiler_params=_params(("parallel", "arbitrary")),
    )(h, mods, norm_g.reshape(1, d), w, w)


def _attn_kernel(lam_ref, ql_ref, qc_ref, kl_ref, vl_ref, kc_ref, vc_ref, g_ref, ol_ref, oc_ref,
                 k_s, v_s, s0_s, s1_s, p0_s, p1_s, *, tq, kt, post_scale):
    ctx_len, seq = kc_ref.shape[0], kl_ref.shape[0]
    total = ctx_len + seq
    nkt, nq, ng = total // kt, seq // tq, kt // LANES
    s_bufs, p_bufs = (s0_s, s1_s), (p0_s, p1_s)
    lam = lam_ref[0]
    gain = g_ref[...] * post_scale

    for src_ref, lo in ((kc_ref, 0), (kl_ref, ctx_len)):
        for c in range(src_ref.shape[0] // LANES):
            k_s[:, lo + c * LANES:lo + (c + 1) * LANES] = src_ref[c * LANES:(c + 1) * LANES, :].T
    v_s[0:ctx_len, 0:LANES] = vc_ref[...]
    v_s[ctx_len:total, 0:LANES] = vl_ref[...]
    v_s[:, LANES:2 * LANES] = jnp.ones((total, LANES), BF16)

    def map_mask(shape, mi):
        lane = lax.broadcasted_iota(jnp.int32, shape, 1)
        return ((lane // (A_HEAD_DIM // 2)) % 2) == mi

    def q_map(t, mi):
        q = ql_ref[pl.ds(pl.multiple_of(t * tq, tq), tq), :]
        return jnp.where(map_mask(q.shape, mi), q, jnp.zeros_like(q))

    def scores(qa, buf, j, m):
        s = _dot(qa, k_s[:, j * kt:(j + 1) * kt])
        s_bufs[buf][:, j * kt:(j + 1) * kt] = s
        for g in range(ng):
            m = jnp.maximum(m, s[:, g * LANES:(g + 1) * LANES])
        return m

    def exps(buf, j, mb):
        x = s_bufs[buf][:, j * kt:(j + 1) * kt]
        ps = [jnp.exp2(x[:, g * LANES:(g + 1) * LANES] - mb).astype(BF16) for g in range(ng)]
        p_bufs[buf][:, j * kt:(j + 1) * kt] = ps[0] if ng == 1 else jnp.concatenate(ps, axis=1)

    def values(buf):
        return _dot(p_bufs[buf][...], v_s[...])

    def row_max(m):
        return jnp.broadcast_to(jnp.max(m, axis=-1, keepdims=True), m.shape)

    def finish(t, r0, r1):
        o = r0[:, 0:LANES] / r0[:, LANES:2 * LANES] - lam * (r1[:, 0:LANES] / r1[:, LANES:2 * LANES])
        ol_ref[pl.ds(pl.multiple_of(t * tq, tq), tq), :] = (_rms(o) * gain).astype(ol_ref.dtype)

    neg = jnp.full((tq, LANES), -jnp.inf, F32)

    qa = q_map(0, 0)
    m = neg
    for j in range(nkt):
        m = scores(qa, 0, j, m)
    mx0 = row_max(m)
    qb = q_map(0, 1)
    m = neg
    for j in range(nkt):
        m = scores(qb, 1, j, m)
        exps(0, j, mx0)

    def tile_body(t, carry):
        m1, r0p, r1p = carry
        finish(jnp.maximum(t - 1, 0), r0p, r1p)
        mx1 = row_max(m1)
        tn = jnp.minimum(t + 1, nq - 1)
        qa = q_map(tn, 0)
        m = neg
        for j in range(nkt):
            m = scores(qa, 0, j, m)
        mx0n = row_max(m)
        for j in range(nkt):
            exps(1, j, mx1)
        r0 = values(0)
        r1 = values(1)
        qb = q_map(tn, 1)
        m = neg
        for j in range(nkt):
            m = scores(qb, 1, j, m)
            exps(0, j, mx0n)
        return m, r0, r1

    ones = jnp.ones((tq, 2 * LANES), F32)
    _, r0, r1 = lax.fori_loop(0, nq, tile_body, (m, ones, ones))
    finish(nq - 1, r0, r1)

    qc = qc_ref[...]
    outs = []
    for mi in range(2):
        qm = jnp.where(map_mask(qc.shape, mi), qc, jnp.zeros_like(qc))
        s = _dot_nt(qm, kc_ref[...])
        p = jnp.exp2(s - jnp.max(s, axis=-1, keepdims=True))
        outs.append(_dot(p.astype(BF16), vc_ref[...]) / jnp.sum(p, axis=-1, keepdims=True))
    oc_ref[...] = (_rms(outs[0] - lam * outs[1]) * gain).astype(oc_ref.dtype)


def _diff_attn(p, lam, subln_g, post_scale, batch, seq, ctx_len, heads):
    total = seq + ctx_len
    tq = _pick_tile(256, seq)
    kt = _pick_tile(256, total)
    nkt = total // kt
    n_ctx_blk0 = batch * seq // ctx_len
    kern = functools.partial(_attn_kernel, tq=tq, kt=kt, post_scale=post_scale)

    def lat(off):
        return pl.BlockSpec((seq, LANES), lambda b, h: (b, off + h))

    def ctx(off):
        return pl.BlockSpec((ctx_len, LANES), lambda b, h: (n_ctx_blk0 + b, off + h))

    return pl.pallas_call(
        kern,
        grid=(batch, heads),
        in_specs=[
            pl.BlockSpec(memory_space=pltpu.SMEM),
            lat(0), ctx(0), lat(heads), lat(2 * heads), ctx(heads), ctx(2 * heads),
            pl.BlockSpec((1, LANES), lambda b, h: (0, 0)),
        ],
        out_specs=[
            pl.BlockSpec((seq, LANES), lambda b, h: (b, h)),
            pl.BlockSpec((ctx_len, LANES), lambda b, h: (b, h)),
        ],
        out_shape=[
            jax.ShapeDtypeStruct((batch * seq, heads * LANES), BF16),
            jax.ShapeDtypeStruct((batch * ctx_len, heads * LANES), BF16),
        ],
        scratch_shapes=[
            pltpu.VMEM((LANES, total), BF16),
            pltpu.VMEM((total, 2 * LANES), BF16),
            pltpu.VMEM((tq, total), F32),
            pltpu.VMEM((tq, total), F32),
            pltpu.VMEM((tq, total), BF16),
            pltpu.VMEM((tq, total), BF16),
        ],
        name="diff_attn",
        compiler_params=_params(("parallel", "parallel")),
    )(lam, p, p, p, p, p, p, subln_g.reshape(1, LANES))


def _dwconv(x, w, b):
    n = x.shape[0]
    row = lax.broadcasted_iota(jnp.int32, x.shape, 0)
    y = b + x * w[CONV_LEFT:CONV_LEFT + 1, :]
    for j in range(CONV_W):
        off = j - CONV_LEFT
        if off == 0:
            continue
        xs = pltpu.roll(x, (-off) % n, 0)
        ok = jnp.logical_and(row + off >= 0, row + off < n)
        y = y + jnp.where(ok, xs, 0.0) * w[j:j + 1, :]
    return y


def _lru_kernel(recl_ref, recc_ref, gtl_ref, gtc_ref, cw_ref, cb_ref, wa_ref, ba_ref, wx_ref, bx_ref,
                lam_ref, ol_ref, oc_ref, a_s, b_s, h_s, *, ctx_len, seq):
    total = ctx_len + seq
    nt = total // SUBLANES
    cw = cw_ref[...]
    cb = cb_ref[...]
    region_off = ((0, seq), (ctx_len, 0))
    decay_rate = [LRU_C * _softplus(-lam_ref[d]) for d in range(2)]
    for ri, rec_ref in enumerate((recc_ref, recl_ref)):
        n = rec_ref.shape[0]
        xc = _dwconv(rec_ref[...].astype(F32), cw, cb)
        xcb = xc.astype(BF16)
        pre = [(_dot(xcb, wa_ref[d].astype(BF16)), _dot(xcb, wx_ref[d].astype(BF16))) for d in range(2)]
        for d in range(2):
            r = _sigmoid(pre[d][0] + ba_ref[d])
            gi = _sigmoid(pre[d][1] + bx_ref[d])
            a = jnp.exp(-(r * decay_rate[d]))
            lo = region_off[ri][d]
            a_s[d, lo:lo + n, :] = a
            b_s[d, lo:lo + n, :] = jnp.sqrt(1.0 - a * a) * gi * xc

    row = lax.broadcasted_iota(jnp.int32, (SUBLANES, LANES), 0)

    def scan_tile(a, b, reverse):
        s = 1
        while s < SUBLANES:
            sh = (SUBLANES - s) if reverse else s
            a_sh = pltpu.roll(a, sh, 0)
            b_sh = pltpu.roll(b, sh, 0)
            ok = (row < SUBLANES - s) if reverse else (row >= s)
            b = jnp.where(ok, a * b_sh + b, b)
            a = jnp.where(ok, a * a_sh, a)
            s *= 2
        return a, b

    def body(p, carry):
        cf, cbk = carry
        tf = pl.multiple_of(p * SUBLANES, SUBLANES)
        a, b = scan_tile(a_s[0, pl.ds(tf, SUBLANES), :], b_s[0, pl.ds(tf, SUBLANES), :], False)
        hf = a * cf + b
        h_s[0, pl.ds(tf, SUBLANES), :] = hf
        tb = pl.multiple_of((nt - 1 - p) * SUBLANES, SUBLANES)
        a, b = scan_tile(a_s[1, pl.ds(tb, SUBLANES), :], b_s[1, pl.ds(tb, SUBLANES), :], True)
        hb = a * cbk + b
        h_s[1, pl.ds(tb, SUBLANES), :] = hb
        return hf[SUBLANES - 1:SUBLANES, :], hb[0:1, :]

    zero = jnp.zeros((1, LANES), F32)
    lax.fori_loop(0, nt, body, (zero, zero), unroll=4)

    oc_ref[...] = (_gelu_tanh(gtc_ref[...].astype(F32))
                   * (h_s[0, 0:ctx_len, :] + h_s[1, seq:total, :])).astype(oc_ref.dtype)
    ol_ref[...] = (_gelu_tanh(gtl_ref[...].astype(F32))
                   * (h_s[0, ctx_len:total, :] + h_s[1, 0:seq, :])).astype(ol_ref.dtype)


def _rglru(p, conv_w, conv_b, wa, ba, wx, bx, lru_lam, batch, seq, ctx_len, a_width, b_width):
    nb = b_width // B_BLOCK_DIM
    gt_off = 3 * a_width // LANES
    rec_off = (3 * a_width + b_width) // LANES
    n_ctx_blk0 = batch * seq // ctx_len
    total = seq + ctx_len
    kern = functools.partial(_lru_kernel, ctx_len=ctx_len, seq=seq)
    vec = pl.BlockSpec((2, 1, LANES), lambda b, n: (0, 0, n))
    mat = pl.BlockSpec((2, None, B_BLOCK_DIM, B_BLOCK_DIM), lambda b, n: (0, n, 0, 0))
    return pl.pallas_call(
        kern,
        grid=(batch, nb),
        in_specs=[
            pl.BlockSpec((seq, LANES), lambda b, n: (b, rec_off + n)),
            pl.BlockSpec((ctx_len, LANES), lambda b, n: (n_ctx_blk0 + b, rec_off + n)),
            pl.BlockSpec((seq, LANES), lambda b, n: (b, gt_off + n)),
            pl.BlockSpec((ctx_len, LANES), lambda b, n: (n_ctx_blk0 + b, gt_off + n)),
            pl.BlockSpec((CONV_W, LANES), lambda b, n: (0, n)),
            pl.BlockSpec((1, LANES), lambda b, n: (0, n)),
            mat, vec, mat, vec, vec,
        ],
        out_specs=[
            pl.BlockSpec((seq, LANES), lambda b, n: (b, n)),
            pl.BlockSpec((ctx_len, LANES), lambda b, n: (b, n)),
        ],
        out_shape=[
            jax.ShapeDtypeStruct((batch * seq, b_width), BF16),
            jax.ShapeDtypeStruct((batch * ctx_len, b_width), BF16),
        ],
        scratch_shapes=[
            pltpu.VMEM((2, total, LANES), F32),
            pltpu.VMEM((2, total, LANES), F32),
            pltpu.VMEM((2, total, LANES), F32),
        ],
        name="rglru",
        compiler_params=_params(("parallel", "parallel")),
    )(p, p, p, p, conv_w, conv_b.reshape(1, b_width), wa, ba.reshape(2, 1, b_width),
      wx, bx.reshape(2, 1, b_width), lru_lam.reshape(2, 1, b_width))


GLA_UNROLL = 4


def _gla_kernel(ql_ref, kl_ref, vl_ref, rl_ref, zl_ref, qc_ref, kc_ref, vc_ref, rc_ref, zc_ref,
                w2_ref, gb_ref, g_ref, al_ref, ac_ref, st_s, ol_s, oc_s, *, ctx_len, seq):
    c = C_CHUNK
    ri = lax.broadcasted_iota(jnp.int32, (c, c), 0)
    ci = lax.broadcasted_iota(jnp.int32, (c, c), 1)
    keep = (ri >= ci, ri <= ci)
    tri = tuple(jnp.where(k, 1.0, 0.0).astype(BF16) for k in keep)
    end_row = (c - 1, 0)

    def step(q_ref, k_ref, v_ref, z_ref, o_ref, chains, accumulate):
        rows = [pl.ds(t0, c) for t0, _ in chains]
        dirs = [d for _, d in chains]
        xs = [_dot(z_ref[r, :].astype(BF16), w2_ref[d]) + gb_ref[d] for r, d in zip(rows, dirs)]
        gs = [_log_sigmoid(x) * (1.0 / C_GATE_TAU) for x in xs]
        his = [g.astype(BF16) for g in gs]
        los = [(g - hi.astype(F32)).astype(BF16) for g, hi in zip(gs, his)]
        bcums = [_dot(tri[d], hi) + _dot(tri[d], lo) for d, hi, lo in zip(dirs, his, los)]
        qes, kes, kds, decays = [], [], [], []
        for r, d, bcum in zip(rows, dirs, bcums):
            b_end = bcum[end_row[d]:end_row[d] + 1, :]
            k = k_ref[r, :].astype(F32)
            qes.append((q_ref[r, :].astype(F32) * jnp.exp(bcum)).astype(BF16))
            kes.append((k * jnp.exp(-bcum)).astype(BF16))
            kds.append((k * jnp.exp(b_end - bcum)).astype(BF16))
            decays.append(jnp.exp(b_end))
        raw = [_dot_nt(qe, ke) for qe, ke in zip(qes, kes)]
        ups = [_dot_tn(v_ref[r, :], kd) for r, kd in zip(rows, kds)]
        atts = [jnp.where(keep[d], a, 0.0).astype(BF16) for d, a in zip(dirs, raw)]
        intra = [_dot(att, v_ref[r, :]) for att, r in zip(atts, rows)]
        for r, d, qe, up, decay, oi in zip(rows, dirs, qes, ups, decays, intra):
            st = st_s[d]
            o = oi + _dot_nt(qe, st.astype(BF16))
            st_s[d] = st * decay + up
            if accumulate:
                o_ref[r, :] += o
            else:
                o_ref[r, :] = o

    def sweep(q_ref, k_ref, v_ref, z_ref, o_ref, n):
        half = n // 2
        unroll = math.gcd(GLA_UNROLL, half)

        def run(lo, accumulate):
            def body(i, carry):
                chains = []
                for u in range(unroll):
                    ci_ = i * unroll + u
                    chains.append((pl.multiple_of(ci_ * c, c), 0))
                    chains.append((pl.multiple_of((n - 1 - ci_) * c, c), 1))
                step(q_ref, k_ref, v_ref, z_ref, o_ref, chains, accumulate)
                return carry
            lax.fori_loop(lo // unroll, (lo + half) // unroll, body, 0)

        run(0, False)
        run(half, True)

    st_s[...] = jnp.zeros_like(st_s)
    sweep(qc_ref, kc_ref, vc_ref, zc_ref, oc_s, ctx_len // c)
    sweep(ql_ref, kl_ref, vl_ref, zl_ref, ol_s, seq // c)

    g = g_ref[...]
    ac_ref[...] = (_rms(oc_s[...]) * g * _silu(rc_ref[...].astype(F32))).astype(ac_ref.dtype)
    al_ref[...] = (_rms(ol_s[...]) * g * _silu(rl_ref[...].astype(F32))).astype(al_ref.dtype)


def _gla(p, z, w2p, gate_b, norm_g, batch, seq, ctx_len, heads):
    d_model = heads * C_VAL_DIM
    key_w = heads * C_KEY_DIM
    n_ctx_blk0 = batch * seq // ctx_len
    kern = functools.partial(_gla_kernel, ctx_len=ctx_len, seq=seq)

    def lat(width, off):
        return pl.BlockSpec((seq, width), lambda b, h: (b, off + h))

    def ctx(width, off):
        return pl.BlockSpec((ctx_len, width), lambda b, h: (n_ctx_blk0 + b, off + h))

    specs = [
        lat(C_KEY_DIM, 0), lat(C_KEY_DIM, heads), lat(C_VAL_DIM, heads), lat(C_VAL_DIM, 2 * heads),
        pl.BlockSpec((seq, LANES), lambda b, h: (b, 0)),
        ctx(C_KEY_DIM, 0), ctx(C_KEY_DIM, heads), ctx(C_VAL_DIM, heads), ctx(C_VAL_DIM, 2 * heads),
        pl.BlockSpec((ctx_len, LANES), lambda b, h: (n_ctx_blk0 + b, 0)),
    ]
    specs += [
        pl.BlockSpec((2, LANES, C_KEY_DIM), lambda b, h: (0, 0, h)),
        pl.BlockSpec((2, 1, C_KEY_DIM), lambda b, h: (0, 0, h)),
        pl.BlockSpec((1, C_VAL_DIM), lambda b, h: (0, 0)),
    ]
    return pl.pallas_call(
        kern,
        grid=(batch, heads),
        in_specs=specs,
        out_specs=[
            pl.BlockSpec((seq, C_VAL_DIM), lambda b, h: (b, h)),
            pl.BlockSpec((ctx_len, C_VAL_DIM), lambda b, h: (b, h)),
        ],
        out_shape=[
            jax.ShapeDtypeStruct((batch * seq, d_model), BF16),
            jax.ShapeDtypeStruct((batch * ctx_len, d_model), BF16),
        ],
        scratch_shapes=[
            pltpu.VMEM((2, C_VAL_DIM, C_KEY_DIM), F32),
            pltpu.VMEM((seq, C_VAL_DIM), F32),
            pltpu.VMEM((ctx_len, C_VAL_DIM), F32),
        ],
        name="gla",
        compiler_params=_params(("parallel", "parallel")),
    )(p, p, p, p, z, p, p, p, p, z, w2p, gate_b.reshape(2, 1, key_w), norm_g.reshape(1, C_VAL_DIM))


def _outproj_kernel(al0_ref, ac0_ref, al1_ref, ac1_ref, w0_ref, w1_ref, *refs, n_lat):
    *h_refs, mod_ref, o_ref = refs
    is_lat = pl.program_id(0) < n_lat
    a0 = jnp.where(is_lat, al0_ref[...], ac0_ref[...])
    a1 = jnp.where(is_lat, al1_ref[...], ac1_ref[...])
    acc = _dot(a0, w0_ref[...]) + _dot(a1, w1_ref[...])
    o_ref[...] = _row_source(h_refs, n_lat)(slice(None)) + mod_ref[2:3, :] * acc


def _outproj(srcs, w, layer, h, mods, rows, n_tiles):
    d = w.shape[2]
    kh = w.shape[1] // 2
    tm = rows.tm
    n_lat = rows.n_lat
    specs, args = [], []
    for lat_arr, lat_col, ctx_arr, ctx_row0, ctx_col in srcs:
        specs.append(pl.BlockSpec((tm, kh), lambda i, c=lat_col: (jnp.minimum(i, n_lat - 1), c)))
        specs.append(pl.BlockSpec((tm, kh), lambda i, r=ctx_row0, c=ctx_col: (r + jnp.maximum(i - n_lat, 0), c)))
        args += [lat_arr, ctx_arr]
    kern = functools.partial(_outproj_kernel, n_lat=n_lat)
    h_specs, h_args = _row_specs(h, rows, d)
    return pl.pallas_call(
        kern,
        grid=(n_tiles,),
        in_specs=specs + [
            pl.BlockSpec((None, kh, d), lambda i: (layer, 0, 0)),
            pl.BlockSpec((None, kh, d), lambda i: (layer, 1, 0)),
        ] + h_specs + [
            pl.BlockSpec((None, N_MOD, d), lambda i: (rows.mod_row(i), 0, 0)),
        ],
        out_specs=pl.BlockSpec((tm, d), lambda i: (i, 0)),
        out_shape=jax.ShapeDtypeStruct((n_tiles * tm, d), F32),
        name="outproj",
        compiler_params=_params(("parallel",)),
    )(*args, w, w, *h_args, mods)


MLP_ROW_CHUNK = 512


def _mlp_kernel(h_ref, mod_ref, g_ref, w1_ref, w2_ref, fg_ref, o_ref, u_ref, *, nf, final_norm):
    j = pl.program_id(1)
    tm = o_ref.shape[0]
    rc = min(MLP_ROW_CHUNK, tm)

    def step(first):
        w1, w2 = w1_ref[...], w2_ref[...]
        for r in range(tm // rc):
            rs = slice(r * rc, (r + 1) * rc)
            if first:
                u = _norm_mod(h_ref[rs, :], g_ref[...], mod_ref[3:4, :], mod_ref[4:5, :]).astype(BF16)
                u_ref[rs, :] = u
            else:
                u = u_ref[rs, :]
            t = jnp.maximum(_dot(u, w1), 0.0)
            part = _dot((t * t).astype(BF16), w2)
            if first:
                o_ref[rs, :] = part
            else:
                o_ref[rs, :] += part

    @pl.when(j == 0)
    def _():
        step(True)

    @pl.when(j > 0)
    def _():
        step(False)

    @pl.when(j == nf - 1)
    def _():
        y = h_ref[...] + mod_ref[5:6, :] * o_ref[...]
        if final_norm:
            y = _rms(y) * fg_ref[...]
        o_ref[...] = y


def _mlp(h, mods, norm_g, w1, w2, layer, final_g, rows, n_tiles, tf, final_norm):
    _, d, ff = w1.shape
    nf = ff // tf
    tm = rows.tm
    kern = functools.partial(_mlp_kernel, nf=nf, final_norm=final_norm)
    return pl.pallas_call(
        kern,
        grid=(n_tiles, nf),
        in_specs=[
            pl.BlockSpec((tm, d), lambda i, j: (i, 0)),
            pl.BlockSpec((None, N_MOD, d), lambda i, j: (rows.mod_row(i), 0, 0)),
            pl.BlockSpec((1, d), lambda i, j: (0, 0)),
            pl.BlockSpec((None, d, tf), lambda i, j: (layer, 0, j)),
            pl.BlockSpec((None, tf, d), lambda i, j: (layer, j, 0)),
            pl.BlockSpec((1, d), lambda i, j: (0, 0)),
        ],
        out_specs=pl.BlockSpec((tm, d), lambda i, j: (i, 0)),
        out_shape=jax.ShapeDtypeStruct((n_tiles * tm, d), F32),
        scratch_shapes=[pltpu.VMEM((tm, d), BF16)],
        name="mlp",
        compiler_params=_params(("parallel", "arbitrary")),
    )(h, mods, norm_g.reshape(1, d), w1, w2, final_g.reshape(1, d))


def _rope_tables(seq, pad_rows):
    inv = ROPE_BASE ** (-jnp.arange(ROPE_FREQS, dtype=F32) / ROPE_FREQS)
    t = jnp.arange(seq, dtype=jnp.int32)
    pos_r = (t // GRID_W).astype(F32)[:, None]
    pos_c = (t % GRID_W).astype(F32)[:, None]
    lane = jnp.arange(LANES, dtype=jnp.int32)[None, :]
    second = (lane // A_HEAD_DIM) == 1
    is_col = ((lane % (2 * ROPE_FREQS)) // ROPE_FREQS) == 1
    ang = jnp.where(is_col, pos_c, pos_r) * inv[lane % ROPE_FREQS]
    cos = jnp.cos(ang)
    sin = jnp.where(second, jnp.sin(ang), -jnp.sin(ang))
    qs = A_HEAD_DIM ** -0.5 * math.log2(math.e)
    ones = jnp.ones((pad_rows, LANES), F32)
    zeros = jnp.zeros((pad_rows, LANES), F32)
    one_all = jnp.ones((seq + pad_rows, LANES), F32)
    zero_all = jnp.zeros((seq + pad_rows, LANES), F32)

    def table(rot, pad):
        body = jnp.concatenate([rot, pad])
        return jnp.concatenate([qs * body, body, one_all if pad is ones else zero_all], axis=1)

    return table(cos, ones), table(sin, zeros)


def _permute_qk_columns(w, a_width):
    lead = w.shape[:-1]
    qk = w[..., :2 * a_width].reshape(*lead, -1, 2, 2, 2, ROPE_FREQS)
    qk = jnp.swapaxes(jnp.swapaxes(qk, -2, -3), -3, -4)
    return jnp.concatenate([qk.reshape(*lead, 2 * a_width), w[..., 2 * a_width:]], axis=-1)


def kernel(x, c, ctx, c_ctx, ada_w, ada_b, norm1_g, norm2_g, mlp_w1, mlp_w2, ev_w_in, ev_w_out, ev_lambda_q1, ev_lambda_k1, ev_lambda_q2, ev_lambda_k2, ev_subln_g, ev_conv_w, ev_conv_b, ev_lru_wa, ev_lru_ba, ev_lru_wx, ev_lru_bx, ev_lru_lam, od_w_in, od_w_out, od_gate_w2, od_gate_b, od_norm_g, final_g):
    batch, seq, d = x.shape
    ctx_len = ctx.shape[1]
    depth = ada_w.shape[0]
    a_width = d // 2
    b_width = d - a_width
    a_heads = a_width // (2 * A_HEAD_DIM)
    c_heads = d // C_VAL_DIM
    key_w = c_heads * C_KEY_DIM

    tm = _pick_tile(1024, seq, batch * ctx_len)
    rows = _Rows(batch, seq, ctx_len, tm)
    rows_out = _Rows(batch, seq, ctx_len, _pick_tile(512, seq, batch * ctx_len))
    tf = _pick_tile(512, mlp_w1.shape[2])

    pad = (-(batch + 1)) % SUBLANES
    cc = jnp.concatenate([c, c_ctx[None, :], jnp.zeros((pad, d), F32)], axis=0)
    mods_all = _ada_mod(cc, ada_w, ada_b).reshape(depth, batch + 1 + pad, N_MOD, d)

    rope = _rope_tables(seq, tm)
    h = (x.reshape(batch * seq, d), ctx.reshape(batch * ctx_len, d))

    ev_w_in_b = _permute_qk_columns(ev_w_in, a_width).astype(BF16)
    od_w_in_b = od_w_in.astype(BF16)
    ev_w_out_b, od_w_out_b = ev_w_out.astype(BF16), od_w_out.astype(BF16)
    mlp_w1_b, mlp_w2_b = mlp_w1.astype(BF16), mlp_w2.astype(BF16)

    for i in range(depth):
        last = i == depth - 1
        mods = mods_all[i]
        j = i // 2
        if i % 2 == 0:
            lam_init = 0.8 - 0.6 * math.exp(-0.3 * i)
            lam = (jnp.exp(jnp.sum(ev_lambda_q1[j] * ev_lambda_k1[j]))
                   - jnp.exp(jnp.sum(ev_lambda_q2[j] * ev_lambda_k2[j])) + lam_init).reshape(1).astype(F32)
            p = _inproj_even(h, mods, norm1_g[i], ev_w_in_b, j, rope, rows, a_width)
            ya_lat, ya_ctx = _diff_attn(p, lam, ev_subln_g[j], 1.0 - lam_init, batch, seq, ctx_len, a_heads)
            yb_lat, yb_ctx = _rglru(p, ev_conv_w[j], ev_conv_b[j], ev_lru_wa[j], ev_lru_ba[j],
                                    ev_lru_wx[j], ev_lru_bx[j], ev_lru_lam[j],
                                    batch, seq, ctx_len, a_width, b_width)
            srcs = [(ya_lat, 0, ya_ctx, 0, 0), (yb_lat, 0, yb_ctx, 0, 0)]
            w_out = ev_w_out_b
        else:
            n_main = 2 * key_w + 2 * d
            p, z = _inproj_odd(h, mods, norm1_g[i], od_w_in_b, j, rows, key_w, n_main)
            w2p = jnp.zeros((2, LANES, key_w), F32)
            for dd in range(2):
                w2p = w2p.at[dd, dd * C_GATE_RANK:(dd + 1) * C_GATE_RANK, :].set(od_gate_w2[j, dd])
            a_lat, a_ctx = _gla(p, z, w2p.astype(BF16), od_gate_b[j], od_norm_g[j], batch, seq, ctx_len, c_heads)
            srcs = [(a_lat, 0, a_ctx, 0, 0), (a_lat, 1, a_ctx, 0, 1)]
            w_out = od_w_out_b

        n_out = rows_out.n_lat if last else rows_out.n_lat + rows_out.n_ctx
        h = _outproj(srcs, w_out, j, h, mods, rows_out, n_out)
        n_mlp = rows.n_lat if last else rows.n_lat + rows.n_ctx
        h = _mlp(h, mods, norm2_g[i], mlp_w1_b, mlp_w2_b, i, final_g, rows, n_mlp, tf, last)

    return h.reshape(batch, seq, d)
```

```python
import functools
import math

import jax
import jax.numpy as jnp
from jax import lax
from jax.experimental import pallas as pl
from jax.experimental.pallas import tpu as pltpu

F32 = jnp.float32
BF16 = jnp.bfloat16

NORM_EPS = 1e-6
N_MOD = 6
GRID_W = 64
A_HEAD_DIM = 64
ROPE_FREQS = A_HEAD_DIM // 4
ROPE_BASE = 10000.0
B_BLOCK_DIM = 128
CONV_W = 4
CONV_LEFT = 2
LRU_C = 8.0
C_KEY_DIM = 256
C_VAL_DIM = 512
C_GATE_RANK = 16
C_GATE_TAU = 16.0
C_CHUNK = 64

LANES = 128
SUBLANES = 8
VMEM_LIMIT = 56 * 1024 * 1024


def _params(sem, vmem=VMEM_LIMIT):
    return pltpu.CompilerParams(dimension_semantics=sem, vmem_limit_bytes=vmem)


def _dot(a, b):
    return jnp.dot(a, b, preferred_element_type=F32)


def _dot_nt(a, b):
    return lax.dot_general(a, b, (((1,), (1,)), ((), ())), preferred_element_type=F32)


def _dot_tn(a, b):
    return lax.dot_general(a, b, (((0,), (0,)), ((), ())), preferred_element_type=F32)


def _sigmoid(x):
    return 1.0 / (1.0 + jnp.exp(-x))


def _silu(x):
    return x * _sigmoid(x)


def _softplus(x):
    return jnp.maximum(x, 0.0) + jnp.log(1.0 + jnp.exp(-jnp.abs(x)))


def _log_sigmoid(x):
    return -_softplus(-x)


def _gelu_tanh(x):
    return 0.5 * x * (1.0 + jnp.tanh(math.sqrt(2.0 / math.pi) * (x + 0.044715 * (x * x * x))))


def _rms(x):
    return x * lax.rsqrt(jnp.mean(x * x, axis=-1, keepdims=True) + NORM_EPS)


def _norm_mod(x, g, shift, scale):
    return (_rms(x) * g) * (1.0 + scale) + shift


def _pick_tile(limit, *dims):
    t = limit
    while t > SUBLANES and any(d % t for d in dims):
        t //= 2
    assert all(d % t == 0 for d in dims), (limit, dims)
    return t


def _ada_kernel(c_ref, w_ref, b_ref, o_ref):
    s = _silu(c_ref[...]).astype(BF16)
    o_ref[...] = _dot(s, w_ref[...].astype(BF16)) + b_ref[...]


def _ada_mod(cc, ada_w, ada_b):
    depth, d, n = ada_w.shape
    rows = cc.shape[0]
    tn = _pick_tile(1024, n)
    return pl.pallas_call(
        _ada_kernel,
        grid=(depth, n // tn),
        in_specs=[
            pl.BlockSpec((rows, d), lambda l, j: (0, 0)),
            pl.BlockSpec((None, d, tn), lambda l, j: (l, 0, j)),
            pl.BlockSpec((None, 1, tn), lambda l, j: (l, 0, j)),
        ],
        out_specs=pl.BlockSpec((None, rows, tn), lambda l, j: (l, 0, j)),
        out_shape=jax.ShapeDtypeStruct((depth, rows, n), F32),
        name="ada_mod",
        compiler_params=_params(("parallel", "parallel")),
    )(cc, ada_w, ada_b.reshape(depth, 1, n))


class _Rows:
    def __init__(self, batch, seq, ctx_len, tm):
        self.batch, self.seq, self.ctx = batch, seq, ctx_len
        self.tm = tm
        self.n_lat = batch * seq // tm
        self.n_ctx = batch * ctx_len // tm
        self.per_batch = seq // tm
        self.rows_lat = batch * seq
        self.rows_all = batch * (seq + ctx_len)

    def mod_row(self, i):
        return jnp.where(i < self.n_lat, i // self.per_batch, self.batch)

    def rope_row(self, i):
        return jnp.where(i < self.n_lat, i % self.per_batch, self.per_batch)


INPROJ_ROW_CHUNK = 256


def _row_source(refs, n_lat):
    if len(refs) == 1:
        return lambda rs: refs[0][rs, :]
    is_lat = pl.program_id(0) < n_lat
    return lambda rs: jnp.where(is_lat, refs[0][rs, :], refs[1][rs, :])


def _row_specs(h, rows, width):
    tm, n_lat = rows.tm, rows.n_lat
    if not isinstance(h, tuple):
        return [pl.BlockSpec((tm, width), lambda i, *_: (i, 0))], [h]
    return [pl.BlockSpec((tm, width), lambda i, *_: (jnp.minimum(i, n_lat - 1), 0)),
            pl.BlockSpec((tm, width), lambda i, *_: (jnp.maximum(i - n_lat, 0), 0),
                         pipeline_mode=pl.Buffered(1))], list(h)


def _inproj_even_kernel(*refs, n_lat):
    *h_refs, mod_ref, g_ref, w_ref, cos_ref, sin_ref, o_ref, u_ref = refs
    h_rows = _row_source(h_refs, n_lat)
    j = pl.program_id(1)
    tm, tn = o_ref.shape
    rc = min(INPROJ_ROW_CHUNK, tm)

    def step(first):
        w = w_ref[...]
        for r in range(tm // rc):
            rs = slice(r * rc, (r + 1) * rc)
            if first:
                u = _norm_mod(h_rows(rs), g_ref[...], mod_ref[0:1, :], mod_ref[1:2, :]).astype(BF16)
                u_ref[rs, :] = u
            else:
                u = u_ref[rs, :]
            x = _dot(u, w)
            cos, sin = cos_ref[rs, :], sin_ref[rs, :]
            for g in range(tn // LANES):
                xg = x[:, g * LANES:(g + 1) * LANES]
                y = xg * cos + pltpu.roll(xg, LANES // 2, 1) * sin
                o_ref[rs, g * LANES:(g + 1) * LANES] = y.astype(o_ref.dtype)

    @pl.when(j == 0)
    def _():
        step(True)

    @pl.when(j > 0)
    def _():
        step(False)


def _inproj_even(h, mods, norm_g, w, layer, rope, rows, a_width):
    _, d, n = w.shape
    tm = rows.tm
    tn = _pick_tile(1024, a_width, n)
    cos, sin = rope
    nq, nqk = a_width // tn, 2 * a_width // tn

    def col_type(j):
        return (j >= nq).astype(jnp.int32) + (j >= nqk).astype(jnp.int32)

    tab = pl.BlockSpec((tm, LANES), lambda i, j: (rows.rope_row(i), col_type(j)))
    h_specs, h_args = _row_specs(h, rows, d)
    return pl.pallas_call(
        functools.partial(_inproj_even_kernel, n_lat=rows.n_lat),
        grid=(rows.n_lat + rows.n_ctx, n // tn),
        in_specs=h_specs + [
            pl.BlockSpec((None, N_MOD, d), lambda i, j: (rows.mod_row(i), 0, 0)),
            pl.BlockSpec((1, d), lambda i, j: (0, 0)),
            pl.BlockSpec((None, d, tn), lambda i, j: (layer, 0, j)),
            tab, tab,
        ],
        out_specs=pl.BlockSpec((tm, tn), lambda i, j: (i, j)),
        out_shape=jax.ShapeDtypeStruct((rows.rows_all, n), BF16),
        scratch_shapes=[pltpu.VMEM((tm, d), BF16)],
        name="inproj_even",
        compiler_params=_params(("parallel", "arbitrary")),
    )(*h_args, mods, norm_g.reshape(1, d), w, cos, sin)


def _inproj_odd_kernel(h_ref, mod_ref, g_ref, w_ref, wz_ref, o_ref, z_ref, u_ref, *, nq):
    j = pl.program_id(1)
    tm = o_ref.shape[0]
    rc = min(INPROJ_ROW_CHUNK, tm)
    col_scale = jnp.where(j < nq, C_KEY_DIM ** -0.5, 1.0).astype(F32)

    def step(first):
        w = w_ref[...]
        for r in range(tm // rc):
            rs = slice(r * rc, (r + 1) * rc)
            if first:
                u = _norm_mod(h_ref[rs, :], g_ref[...], mod_ref[0:1, :], mod_ref[1:2, :]).astype(BF16)
                u_ref[rs, :] = u
                zl = lax.broadcasted_iota(jnp.int32, (rc, LANES), 1)
                z_ref[rs, :] = jnp.where(zl < 2 * C_GATE_RANK, _dot(u, wz_ref[...]), 0.0)
            else:
                u = u_ref[rs, :]
            o_ref[rs, :] = (_dot(u, w) * col_scale).astype(o_ref.dtype)

    @pl.when(j == 0)
    def _():
        step(True)

    @pl.when(j > 0)
    def _():
        step(False)


def _inproj_odd(h, mods, norm_g, w, layer, rows, key_w, n):
    d = w.shape[1]
    assert n % LANES == 0 and w.shape[2] == n + 2 * C_GATE_RANK
    tm = rows.tm
    tn = _pick_tile(1024, key_w, n)
    return pl.pallas_call(
        functools.partial(_inproj_odd_kernel, nq=key_w // tn),
        grid=(rows.n_lat + rows.n_ctx, n // tn),
        in_specs=[
            pl.BlockSpec((tm, d), lambda i, j: (i, 0)),
            pl.BlockSpec((None, N_MOD, d), lambda i, j: (rows.mod_row(i), 0, 0)),
            pl.BlockSpec((1, d), lambda i, j: (0, 0)),
            pl.BlockSpec((None, d, tn), lambda i, j: (layer, 0, j)),
            pl.BlockSpec((None, d, LANES), lambda i, j: (layer, 0, n // LANES)),
        ],
        out_specs=[
            pl.BlockSpec((tm, tn), lambda i, j: (i, j)),
            pl.BlockSpec((tm, LANES), lambda i, j: (i, 0)),
        ],
        out_shape=[
            jax.ShapeDtypeStruct((rows.rows_all, n), BF16),
            jax.ShapeDtypeStruct((rows.rows_all, LANES), F32),
        ],
        scratch_shapes=[pltpu.VMEM((tm, d), BF16)],
        name="inproj_odd",
        compiler_params=_params(("parallel", "arbitrary")),
    )(h, mods, norm_g.reshape(1, d), w, w)


def _attn_kernel(lam_ref, ql_ref, qc_ref, kl_ref, vl_ref, kc_ref, vc_ref, g_ref, ol_ref, oc_ref,
                 k_s, v_s, s0_s, s1_s, p0_s, p1_s, *, tq, kt, post_scale):
    ctx_len, seq = kc_ref.shape[0], kl_ref.shape[0]
    total = ctx_len + seq
    nkt, nq, ng = total // kt, seq // tq, kt // LANES
    s_bufs, p_bufs = (s0_s, s1_s), (p0_s, p1_s)
    lam = lam_ref[0]
    gain = g_ref[...] * post_scale

    for src_ref, lo in ((kc_ref, 0), (kl_ref, ctx_len)):
        for c in range(src_ref.shape[0] // LANES):
            k_s[:, lo + c * LANES:lo + (c + 1) * LANES] = src_ref[c * LANES:(c + 1) * LANES, :].T
    v_s[0:ctx_len, 0:LANES] = vc_ref[...]
    v_s[ctx_len:total, 0:LANES] = vl_ref[...]
    v_s[:, LANES:2 * LANES] = jnp.ones((total, LANES), BF16)

    def map_mask(shape, mi):
        lane = lax.broadcasted_iota(jnp.int32, shape, 1)
        return ((lane // (A_HEAD_DIM // 2)) % 2) == mi

    def q_map(t, mi):
        q = ql_ref[pl.ds(pl.multiple_of(t * tq, tq), tq), :]
        return jnp.where(map_mask(q.shape, mi), q, jnp.zeros_like(q))

    def scores(qa, buf, j, m):
        s = _dot(qa, k_s[:, j * kt:(j + 1) * kt])
        s_bufs[buf][:, j * kt:(j + 1) * kt] = s
        for g in range(ng):
            m = jnp.maximum(m, s[:, g * LANES:(g + 1) * LANES])
        return m

    def exps(buf, j, mb):
        x = s_bufs[buf][:, j * kt:(j + 1) * kt]
        ps = [jnp.exp2(x[:, g * LANES:(g + 1) * LANES] - mb).astype(BF16) for g in range(ng)]
        p_bufs[buf][:, j * kt:(j + 1) * kt] = ps[0] if ng == 1 else jnp.concatenate(ps, axis=1)

    def values(buf):
        return _dot(p_bufs[buf][...], v_s[...])

    def row_max(m):
        return jnp.broadcast_to(jnp.max(m, axis=-1, keepdims=True), m.shape)

    def finish(t, r0, r1):
        o = r0[:, 0:LANES] / r0[:, LANES:2 * LANES] - lam * (r1[:, 0:LANES] / r1[:, LANES:2 * LANES])
        ol_ref[pl.ds(pl.multiple_of(t * tq, tq), tq), :] = (_rms(o) * gain).astype(ol_ref.dtype)

    neg = jnp.full((tq, LANES), -jnp.inf, F32)

    qa = q_map(0, 0)
    m = neg
    for j in range(nkt):
        m = scores(qa, 0, j, m)
    mx0 = row_max(m)
    qb = q_map(0, 1)
    m = neg
    for j in range(nkt):
        m = scores(qb, 1, j, m)
        exps(0, j, mx0)

    def tile_body(t, carry):
        m1, r0p, r1p = carry
        finish(jnp.maximum(t - 1, 0), r0p, r1p)
        mx1 = row_max(m1)
        tn = jnp.minimum(t + 1, nq - 1)
        qa = q_map(tn, 0)
        m = neg
        for j in range(nkt):
            m = scores(qa, 0, j, m)
        mx0n = row_max(m)
        for j in range(nkt):
            exps(1, j, mx1)
        r0 = values(0)
        r1 = values(1)
        qb = q_map(tn, 1)
        m = neg
        for j in range(nkt):
            m = scores(qb, 1, j, m)
            exps(0, j, mx0n)
        return m, r0, r1

    ones = jnp.ones((tq, 2 * LANES), F32)
    _, r0, r1 = lax.fori_loop(0, nq, tile_body, (m, ones, ones))
    finish(nq - 1, r0, r1)

    qc = qc_ref[...]
    outs = []
    for mi in range(2):
        qm = jnp.where(map_mask(qc.shape, mi), qc, jnp.zeros_like(qc))
        s = _dot_nt(qm, kc_ref[...])
        p = jnp.exp2(s - jnp.max(s, axis=-1, keepdims=True))
        outs.append(_dot(p.astype(BF16), vc_ref[...]) / jnp.sum(p, axis=-1, keepdims=True))
    oc_ref[...] = (_rms(outs[0] - lam * outs[1]) * gain).astype(oc_ref.dtype)


def _diff_attn(p, lam, subln_g, post_scale, batch, seq, ctx_len, heads):
    total = seq + ctx_len
    tq = _pick_tile(256, seq)
    kt = _pick_tile(256, total)
    nkt = total // kt
    n_ctx_blk0 = batch * seq // ctx_len
    kern = functools.partial(_attn_kernel, tq=tq, kt=kt, post_scale=post_scale)

    def lat(off):
        return pl.BlockSpec((seq, LANES), lambda b, h: (b, off + h))

    def ctx(off):
        return pl.BlockSpec((ctx_len, LANES), lambda b, h: (n_ctx_blk0 + b, off + h))

    return pl.pallas_call(
        kern,
        grid=(batch, heads),
        in_specs=[
            pl.BlockSpec(memory_space=pltpu.SMEM),
            lat(0), ctx(0), lat(heads), lat(2 * heads), ctx(heads), ctx(2 * heads),
            pl.BlockSpec((1, LANES), lambda b, h: (0, 0)),
        ],
        out_specs=[
            pl.BlockSpec((seq, LANES), lambda b, h: (b, h)),
            pl.BlockSpec((ctx_len, LANES), lambda b, h: (b, h)),
        ],
        out_shape=[
            jax.ShapeDtypeStruct((batch * seq, heads * LANES), BF16),
            jax.ShapeDtypeStruct((batch * ctx_len, heads * LANES), BF16),
        ],
        scratch_shapes=[
            pltpu.VMEM((LANES, total), BF16),
            pltpu.VMEM((total, 2 * LANES), BF16),
            pltpu.VMEM((tq, total), F32),
            pltpu.VMEM((tq, total), F32),
            pltpu.VMEM((tq, total), BF16),
            pltpu.VMEM((tq, total), BF16),
        ],
        name="diff_attn",
        compiler_params=_params(("parallel", "parallel")),
    )(lam, p, p, p, p, p, p, subln_g.reshape(1, LANES))


def _dwconv(x, w, b):
    n = x.shape[0]
    row = lax.broadcasted_iota(jnp.int32, x.shape, 0)
    y = b + x * w[CONV_LEFT:CONV_LEFT + 1, :]
    for j in range(CONV_W):
        off = j - CONV_LEFT
        if off == 0:
            continue
        xs = pltpu.roll(x, (-off) % n, 0)
        ok = jnp.logical_and(row + off >= 0, row + off < n)
        y = y + jnp.where(ok, xs, 0.0) * w[j:j + 1, :]
    return y


def _lru_kernel(recl_ref, recc_ref, gtl_ref, gtc_ref, cw_ref, cb_ref, wa_ref, ba_ref, wx_ref, bx_ref,
                lam_ref, ol_ref, oc_ref, a_s, b_s, h_s, *, ctx_len, seq):
    total = ctx_len + seq
    nt = total // SUBLANES
    cw = cw_ref[...]
    cb = cb_ref[...]
    region_off = ((0, seq), (ctx_len, 0))
    decay_rate = [LRU_C * _softplus(-lam_ref[d]) for d in range(2)]
    for ri, rec_ref in enumerate((recc_ref, recl_ref)):
        n = rec_ref.shape[0]
        xc = _dwconv(rec_ref[...].astype(F32), cw, cb)
        xcb = xc.astype(BF16)
        pre = [(_dot(xcb, wa_ref[d].astype(BF16)), _dot(xcb, wx_ref[d].astype(BF16))) for d in range(2)]
        for d in range(2):
            r = _sigmoid(pre[d][0] + ba_ref[d])
            gi = _sigmoid(pre[d][1] + bx_ref[d])
            a = jnp.exp(-(r * decay_rate[d]))
            lo = region_off[ri][d]
            a_s[d, lo:lo + n, :] = a
            b_s[d, lo:lo + n, :] = jnp.sqrt(1.0 - a * a) * gi * xc

    row = lax.broadcasted_iota(jnp.int32, (SUBLANES, LANES), 0)

    def scan_tile(a, b, reverse):
        s = 1
        while s < SUBLANES:
            sh = (SUBLANES - s) if reverse else s
            a_sh = pltpu.roll(a, sh, 0)
            b_sh = pltpu.roll(b, sh, 0)
            ok = (row < SUBLANES - s) if reverse else (row >= s)
            b = jnp.where(ok, a * b_sh + b, b)
            a = jnp.where(ok, a * a_sh, a)
            s *= 2
        return a, b

    def body(p, carry):
        cf, cbk = carry
        tf = pl.multiple_of(p * SUBLANES, SUBLANES)
        a, b = scan_tile(a_s[0, pl.ds(tf, SUBLANES), :], b_s[0, pl.ds(tf, SUBLANES), :], False)
        hf = a * cf + b
        h_s[0, pl.ds(tf, SUBLANES), :] = hf
        tb = pl.multiple_of((nt - 1 - p) * SUBLANES, SUBLANES)
        a, b = scan_tile(a_s[1, pl.ds(tb, SUBLANES), :], b_s[1, pl.ds(tb, SUBLANES), :], True)
        hb = a * cbk + b
        h_s[1, pl.ds(tb, SUBLANES), :] = hb
        return hf[SUBLANES - 1:SUBLANES, :], hb[0:1, :]

    zero = jnp.zeros((1, LANES), F32)
    lax.fori_loop(0, nt, body, (zero, zero), unroll=4)

    oc_ref[...] = (_gelu_tanh(gtc_ref[...].astype(F32))
                   * (h_s[0, 0:ctx_len, :] + h_s[1, seq:total, :])).astype(oc_ref.dtype)
    ol_ref[...] = (_gelu_tanh(gtl_ref[...].astype(F32))
                   * (h_s[0, ctx_len:total, :] + h_s[1, 0:seq, :])).astype(ol_ref.dtype)


def _rglru(p, conv_w, conv_b, wa, ba, wx, bx, lru_lam, batch, seq, ctx_len, a_width, b_width):
    nb = b_width // B_BLOCK_DIM
    gt_off = 3 * a_width // LANES
    rec_off = (3 * a_width + b_width) // LANES
    n_ctx_blk0 = batch * seq // ctx_len
    total = seq + ctx_len
    kern = functools.partial(_lru_kernel, ctx_len=ctx_len, seq=seq)
    vec = pl.BlockSpec((2, 1, LANES), lambda b, n: (0, 0, n))
    mat = pl.BlockSpec((2, None, B_BLOCK_DIM, B_BLOCK_DIM), lambda b, n: (0, n, 0, 0))
    return pl.pallas_call(
        kern,
        grid=(batch, nb),
        in_specs=[
            pl.BlockSpec((seq, LANES), lambda b, n: (b, rec_off + n)),
            pl.BlockSpec((ctx_len, LANES), lambda b, n: (n_ctx_blk0 + b, rec_off + n)),
            pl.BlockSpec((seq, LANES), lambda b, n: (b, gt_off + n)),
            pl.BlockSpec((ctx_len, LANES), lambda b, n: (n_ctx_blk0 + b, gt_off + n)),
            pl.BlockSpec((CONV_W, LANES), lambda b, n: (0, n)),
            pl.BlockSpec((1, LANES), lambda b, n: (0, n)),
            mat, vec, mat, vec, vec,
        ],
        out_specs=[
            pl.BlockSpec((seq, LANES), lambda b, n: (b, n)),
            pl.BlockSpec((ctx_len, LANES), lambda b, n: (b, n)),
        ],
        out_shape=[
            jax.ShapeDtypeStruct((batch * seq, b_width), BF16),
            jax.ShapeDtypeStruct((batch * ctx_len, b_width), BF16),
        ],
        scratch_shapes=[
            pltpu.VMEM((2, total, LANES), F32),
            pltpu.VMEM((2, total, LANES), F32),
            pltpu.VMEM((2, total, LANES), F32),
        ],
        name="rglru",
        compiler_params=_params(("parallel", "parallel")),
    )(p, p, p, p, conv_w, conv_b.reshape(1, b_width), wa, ba.reshape(2, 1, b_width),
      wx, bx.reshape(2, 1, b_width), lru_lam.reshape(2, 1, b_width))


GLA_UNROLL = 4


def _gla_kernel(ql_ref, kl_ref, vl_ref, rl_ref, zl_ref, qc_ref, kc_ref, vc_ref, rc_ref, zc_ref,
                w2_ref, gb_ref, g_ref, al_ref, ac_ref, st_s, ol_s, oc_s, *, ctx_len, seq):
    c = C_CHUNK
    ri = lax.broadcasted_iota(jnp.int32, (c, c), 0)
    ci = lax.broadcasted_iota(jnp.int32, (c, c), 1)
    keep = (ri >= ci, ri <= ci)
    tri = tuple(jnp.where(k, 1.0, 0.0).astype(BF16) for k in keep)
    end_row = (c - 1, 0)

    def step(q_ref, k_ref, v_ref, z_ref, o_ref, chains, accumulate):
        rows = [pl.ds(t0, c) for t0, _ in chains]
        dirs = [d for _, d in chains]
        xs = [_dot(z_ref[r, :].astype(BF16), w2_ref[d]) + gb_ref[d] for r, d in zip(rows, dirs)]
        gs = [_log_sigmoid(x) * (1.0 / C_GATE_TAU) for x in xs]
        his = [g.astype(BF16) for g in gs]
        los = [(g - hi.astype(F32)).astype(BF16) for g, hi in zip(gs, his)]
        bcums = [_dot(tri[d], hi) + _dot(tri[d], lo) for d, hi, lo in zip(dirs, his, los)]
        qes, kes, kds, decays = [], [], [], []
        for r, d, bcum in zip(rows, dirs, bcums):
            b_end = bcum[end_row[d]:end_row[d] + 1, :]
            k = k_ref[r, :].astype(F32)
            qes.append((q_ref[r, :].astype(F32) * jnp.exp(bcum)).astype(BF16))
            kes.append((k * jnp.exp(-bcum)).astype(BF16))
            kds.append((k * jnp.exp(b_end - bcum)).astype(BF16))
            decays.append(jnp.exp(b_end))
        raw = [_dot_nt(qe, ke) for qe, ke in zip(qes, kes)]
        ups = [_dot_tn(v_ref[r, :], kd) for r, kd in zip(rows, kds)]
        atts = [jnp.where(keep[d], a, 0.0).astype(BF16) for d, a in zip(dirs, raw)]
        intra = [_dot(att, v_ref[r, :]) for att, r in zip(atts, rows)]
        for r, d, qe, up, decay, oi in zip(rows, dirs, qes, ups, decays, intra):
            st = st_s[d]
            o = oi + _dot_nt(qe, st.astype(BF16))
            st_s[d] = st * decay + up
            if accumulate:
                o_ref[r, :] += o
            else:
                o_ref[r, :] = o

    def sweep(q_ref, k_ref, v_ref, z_ref, o_ref, n):
        half = n // 2
        unroll = math.gcd(GLA_UNROLL, half)

        def run(lo, accumulate):
            def body(i, carry):
                chains = []
                for u in range(unroll):
                    ci_ = i * unroll + u
                    chains.append((pl.multiple_of(ci_ * c, c), 0))
                    chains.append((pl.multiple_of((n - 1 - ci_) * c, c), 1))
                step(q_ref, k_ref, v_ref, z_ref, o_ref, chains, accumulate)
                return carry
            lax.fori_loop(lo // unroll, (lo + half) // unroll, body, 0)

        run(0, False)
        run(half, True)

    st_s[...] = jnp.zeros_like(st_s)
    sweep(qc_ref, kc_ref, vc_ref, zc_ref, oc_s, ctx_len // c)
    sweep(ql_ref, kl_ref, vl_ref, zl_ref, ol_s, seq // c)

    g = g_ref[...]
    ac_ref[...] = (_rms(oc_s[...]) * g * _silu(rc_ref[...].astype(F32))).astype(ac_ref.dtype)
    al_ref[...] = (_rms(ol_s[...]) * g * _silu(rl_ref[...].astype(F32))).astype(al_ref.dtype)


def _gla(p, z, w2p, gate_b, norm_g, batch, seq, ctx_len, heads):
    d_model = heads * C_VAL_DIM
    key_w = heads * C_KEY_DIM
    n_ctx_blk0 = batch * seq // ctx_len
    kern = functools.partial(_gla_kernel, ctx_len=ctx_len, seq=seq)

    def lat(width, off):
        return pl.BlockSpec((seq, width), lambda b, h: (b, off + h))

    def ctx(width, off):
        return pl.BlockSpec((ctx_len, width), lambda b, h: (n_ctx_blk0 + b, off + h))

    specs = [
        lat(C_KEY_DIM, 0), lat(C_KEY_DIM, heads), lat(C_VAL_DIM, heads), lat(C_VAL_DIM, 2 * heads),
        pl.BlockSpec((seq, LANES), lambda b, h: (b, 0)),
        ctx(C_KEY_DIM, 0), ctx(C_KEY_DIM, heads), ctx(C_VAL_DIM, heads), ctx(C_VAL_DIM, 2 * heads),
        pl.BlockSpec((ctx_len, LANES), lambda b, h: (n_ctx_blk0 + b, 0)),
    ]
    specs += [
        pl.BlockSpec((2, LANES, C_KEY_DIM), lambda b, h: (0, 0, h)),
        pl.BlockSpec((2, 1, C_KEY_DIM), lambda b, h: (0, 0, h)),
        pl.BlockSpec((1, C_VAL_DIM), lambda b, h: (0, 0)),
    ]
    return pl.pallas_call(
        kern,
        grid=(batch, heads),
        in_specs=specs,
        out_specs=[
            pl.BlockSpec((seq, C_VAL_DIM), lambda b, h: (b, h)),
            pl.BlockSpec((ctx_len, C_VAL_DIM), lambda b, h: (b, h)),
        ],
        out_shape=[
            jax.ShapeDtypeStruct((batch * seq, d_model), BF16),
            jax.ShapeDtypeStruct((batch * ctx_len, d_model), BF16),
        ],
        scratch_shapes=[
            pltpu.VMEM((2, C_VAL_DIM, C_KEY_DIM), F32),
            pltpu.VMEM((seq, C_VAL_DIM), F32),
            pltpu.VMEM((ctx_len, C_VAL_DIM), F32),
        ],
        name="gla",
        compiler_params=_params(("parallel", "parallel")),
    )(p, p, p, p, z, p, p, p, p, z, w2p, gate_b.reshape(2, 1, key_w), norm_g.reshape(1, C_VAL_DIM))


def _outproj_kernel(al0_ref, ac0_ref, al1_ref, ac1_ref, w0_ref, w1_ref, *refs, n_lat):
    *h_refs, mod_ref, o_ref = refs
    is_lat = pl.program_id(0) < n_lat
    a0 = jnp.where(is_lat, al0_ref[...], ac0_ref[...])
    a1 = jnp.where(is_lat, al1_ref[...], ac1_ref[...])
    acc = _dot(a0, w0_ref[...]) + _dot(a1, w1_ref[...])
    o_ref[...] = _row_source(h_refs, n_lat)(slice(None)) + mod_ref[2:3, :] * acc


def _outproj(srcs, w, layer, h, mods, rows, n_tiles):
    d = w.shape[2]
    kh = w.shape[1] // 2
    tm = rows.tm
    n_lat = rows.n_lat
    specs, args = [], []
    for lat_arr, lat_col, ctx_arr, ctx_row0, ctx_col in srcs:
        specs.append(pl.BlockSpec((tm, kh), lambda i, c=lat_col: (jnp.minimum(i, n_lat - 1), c)))
        specs.append(pl.BlockSpec((tm, kh), lambda i, r=ctx_row0, c=ctx_col: (r + jnp.maximum(i - n_lat, 0), c)))
        args += [lat_arr, ctx_arr]
    kern = functools.partial(_outproj_kernel, n_lat=n_lat)
    h_specs, h_args = _row_specs(h, rows, d)
    return pl.pallas_call(
        kern,
        grid=(n_tiles,),
        in_specs=specs + [
            pl.BlockSpec((None, kh, d), lambda i: (layer, 0, 0)),
            pl.BlockSpec((None, kh, d), lambda i: (layer, 1, 0)),
        ] + h_specs + [
            pl.BlockSpec((None, N_MOD, d), lambda i: (rows.mod_row(i), 0, 0)),
        ],
        out_specs=pl.BlockSpec((tm, d), lambda i: (i, 0)),
        out_shape=jax.ShapeDtypeStruct((n_tiles * tm, d), F32),
        name="outproj",
        compiler_params=_params(("parallel",)),
    )(*args, w, w, *h_args, mods)


MLP_ROW_CHUNK = 512


def _mlp_kernel(h_ref, mod_ref, g_ref, w1_ref, w2_ref, fg_ref, o_ref, u_ref, *, nf, final_norm):
    j = pl.program_id(1)
    tm = o_ref.shape[0]
    rc = min(MLP_ROW_CHUNK, tm)

    def step(first):
        w1, w2 = w1_ref[...].astype(BF16), w2_ref[...].astype(BF16)
        for r in range(tm // rc):
            rs = slice(r * rc, (r + 1) * rc)
            if first:
                u = _norm_mod(h_ref[rs, :], g_ref[...], mod_ref[3:4, :], mod_ref[4:5, :]).astype(BF16)
                u_ref[rs, :] = u
            else:
                u = u_ref[rs, :]
            t = jnp.maximum(_dot(u, w1), 0.0)
            part = _dot((t * t).astype(BF16), w2)
            if first:
                o_ref[rs, :] = part
            else:
                o_ref[rs, :] += part

    @pl.when(j == 0)
    def _():
        step(True)

    @pl.when(j > 0)
    def _():
        step(False)

    @pl.when(j == nf - 1)
    def _():
        y = h_ref[...] + mod_ref[5:6, :] * o_ref[...]
        if final_norm:
            y = _rms(y) * fg_ref[...]
        o_ref[...] = y


def _mlp(h, mods, norm_g, w1, w2, layer, final_g, rows, n_tiles, tf, final_norm):
    _, d, ff = w1.shape
    nf = ff // tf
    tm = rows.tm
    kern = functools.partial(_mlp_kernel, nf=nf, final_norm=final_norm)
    return pl.pallas_call(
        kern,
        grid=(n_tiles, nf),
        in_specs=[
            pl.BlockSpec((tm, d), lambda i, j: (i, 0), pipeline_mode=pl.Buffered(1)),
            pl.BlockSpec((None, N_MOD, d), lambda i, j: (rows.mod_row(i), 0, 0)),
            pl.BlockSpec((1, d), lambda i, j: (0, 0)),
            pl.BlockSpec((None, d, tf), lambda i, j: (layer, 0, j)),
            pl.BlockSpec((None, tf, d), lambda i, j: (layer, j, 0)),
            pl.BlockSpec((1, d), lambda i, j: (0, 0)),
        ],
        out_specs=pl.BlockSpec((tm, d), lambda i, j: (i, 0)),
        out_shape=jax.ShapeDtypeStruct((n_tiles * tm, d), F32),
        scratch_shapes=[pltpu.VMEM((tm, d), BF16)],
        name="mlp",
        compiler_params=_params(("parallel", "arbitrary")),
    )(h, mods, norm_g.reshape(1, d), w1, w2, final_g.reshape(1, d))


def _rope_tables(seq, pad_rows):
    inv = ROPE_BASE ** (-jnp.arange(ROPE_FREQS, dtype=F32) / ROPE_FREQS)
    t = jnp.arange(seq, dtype=jnp.int32)
    pos_r = (t // GRID_W).astype(F32)[:, None]
    pos_c = (t % GRID_W).astype(F32)[:, None]
    lane = jnp.arange(LANES, dtype=jnp.int32)[None, :]
    second = (lane // A_HEAD_DIM) == 1
    is_col = ((lane % (2 * ROPE_FREQS)) // ROPE_FREQS) == 1
    ang = jnp.where(is_col, pos_c, pos_r) * inv[lane % ROPE_FREQS]
    cos = jnp.cos(ang)
    sin = jnp.where(second, jnp.sin(ang), -jnp.sin(ang))
    qs = A_HEAD_DIM ** -0.5 * math.log2(math.e)
    ones = jnp.ones((pad_rows, LANES), F32)
    zeros = jnp.zeros((pad_rows, LANES), F32)
    one_all = jnp.ones((seq + pad_rows, LANES), F32)
    zero_all = jnp.zeros((seq + pad_rows, LANES), F32)

    def table(rot, pad):
        body = jnp.concatenate([rot, pad])
        return jnp.concatenate([qs * body, body, one_all if pad is ones else zero_all], axis=1)

    return table(cos, ones), table(sin, zeros)


def _permute_qk_columns(w, a_width):
    lead = w.shape[:-1]
    qk = w[..., :2 * a_width].reshape(*lead, -1, 2, 2, 2, ROPE_FREQS)
    qk = jnp.swapaxes(jnp.swapaxes(qk, -2, -3), -3, -4)
    return jnp.concatenate([qk.reshape(*lead, 2 * a_width), w[..., 2 * a_width:]], axis=-1)


def kernel(x, c, ctx, c_ctx, ada_w, ada_b, norm1_g, norm2_g, mlp_w1, mlp_w2, ev_w_in, ev_w_out, ev_lambda_q1, ev_lambda_k1, ev_lambda_q2, ev_lambda_k2, ev_subln_g, ev_conv_w, ev_conv_b, ev_lru_wa, ev_lru_ba, ev_lru_wx, ev_lru_bx, ev_lru_lam, od_w_in, od_w_out, od_gate_w2, od_gate_b, od_norm_g, final_g):
    batch, seq, d = x.shape
    ctx_len = ctx.shape[1]
    depth = ada_w.shape[0]
    a_width = d // 2
    b_width = d - a_width
    a_heads = a_width // (2 * A_HEAD_DIM)
    c_heads = d // C_VAL_DIM
    key_w = c_heads * C_KEY_DIM

    tm = _pick_tile(1024, seq, batch * ctx_len)
    rows = _Rows(batch, seq, ctx_len, tm)
    rows_out = _Rows(batch, seq, ctx_len, _pick_tile(512, seq, batch * ctx_len))
    tf = _pick_tile(512, mlp_w1.shape[2])

    pad = (-(batch + 1)) % SUBLANES
    cc = jnp.concatenate([c, c_ctx[None, :], jnp.zeros((pad, d), F32)], axis=0)
    mods_all = _ada_mod(cc, ada_w, ada_b).reshape(depth, batch + 1 + pad, N_MOD, d)

    rope = _rope_tables(seq, tm)
    h = (x.reshape(batch * seq, d), ctx.reshape(batch * ctx_len, d))

    ev_w_in_b = _permute_qk_columns(ev_w_in, a_width).astype(BF16)
    od_w_in_b = od_w_in.astype(BF16)
    ev_w_out_b, od_w_out_b = ev_w_out.astype(BF16), od_w_out.astype(BF16)

    for i in range(depth):
        last = i == depth - 1
        mods = mods_all[i]
        j = i // 2
        if i % 2 == 0:
            lam_init = 0.8 - 0.6 * math.exp(-0.3 * i)
            lam = (jnp.exp(jnp.sum(ev_lambda_q1[j] * ev_lambda_k1[j]))
                   - jnp.exp(jnp.sum(ev_lambda_q2[j] * ev_lambda_k2[j])) + lam_init).reshape(1).astype(F32)
            p = _inproj_even(h, mods, norm1_g[i], ev_w_in_b, j, rope, rows, a_width)
            ya_lat, ya_ctx = _diff_attn(p, lam, ev_subln_g[j], 1.0 - lam_init, batch, seq, ctx_len, a_heads)
            yb_lat, yb_ctx = _rglru(p, ev_conv_w[j], ev_conv_b[j], ev_lru_wa[j], ev_lru_ba[j],
                                    ev_lru_wx[j], ev_lru_bx[j], ev_lru_lam[j],
                                    batch, seq, ctx_len, a_width, b_width)
            srcs = [(ya_lat, 0, ya_ctx, 0, 0), (yb_lat, 0, yb_ctx, 0, 0)]
            w_out = ev_w_out_b
        else:
            n_main = 2 * key_w + 2 * d
            p, z = _inproj_odd(h, mods, norm1_g[i], od_w_in_b, j, rows, key_w, n_main)
            w2p = jnp.zeros((2, LANES, key_w), F32)
            for dd in range(2):
                w2p = w2p.at[dd, dd * C_GATE_RANK:(dd + 1) * C_GATE_RANK, :].set(od_gate_w2[j, dd])
            a_lat, a_ctx = _gla(p, z, w2p.astype(BF16), od_gate_b[j], od_norm_g[j], batch, seq, ctx_len, c_heads)
            srcs = [(a_lat, 0, a_ctx, 0, 0), (a_lat, 1, a_ctx, 0, 1)]
            w_out = od_w_out_b

        n_out = rows_out.n_lat if last else rows_out.n_lat + rows_out.n_ctx
        h = _outproj(srcs, w_out, j, h, mods, rows_out, n_out)
        n_mlp = rows.n_lat if last else rows.n_lat + rows.n_ctx
        h = _mlp(h, mods, norm2_g[i], mlp_w1, mlp_w2, i, final_g, rows, n_mlp, tf, last)

    return h.reshape(batch, seq, d)
```

```python
import functools
import math

import jax
import jax.numpy as jnp
from jax import lax
from jax.experimental import pallas as pl
from jax.experimental.pallas import tpu as pltpu

F32 = jnp.float32
BF16 = jnp.bfloat16

NORM_EPS = 1e-6
N_MOD = 6
GRID_W = 64
A_HEAD_DIM = 64
ROPE_FREQS = A_HEAD_DIM // 4
ROPE_BASE = 10000.0
B_BLOCK_DIM = 128
CONV_W = 4
CONV_LEFT = 2
LRU_C = 8.0
C_KEY_DIM = 256
C_VAL_DIM = 512
C_GATE_RANK = 16
C_GATE_TAU = 16.0
C_CHUNK = 64

LANES = 128
SUBLANES = 8
VMEM_LIMIT = 56 * 1024 * 1024

ROW_TILE = 1024
OUT_ROW_TILE = 512
ADA_COL_TILE = 1024
EVEN_COL_TILE = 1024
ODD_COL_TILE = 2048
FF_TILE = 512
ATTN_Q_TILE = 256
ATTN_KEY_TILE = 256


def _params(sem, vmem=VMEM_LIMIT):
    return pltpu.CompilerParams(dimension_semantics=sem, vmem_limit_bytes=vmem)


def _dot(a, b):
    return jnp.dot(a, b, preferred_element_type=F32)


def _dot_nt(a, b):
    return lax.dot_general(a, b, (((1,), (1,)), ((), ())), preferred_element_type=F32)


def _dot_tn(a, b):
    return lax.dot_general(a, b, (((0,), (0,)), ((), ())), preferred_element_type=F32)


def _sigmoid(x):
    return 1.0 / (1.0 + jnp.exp(-x))


def _silu(x):
    return x * _sigmoid(x)


def _softplus(x):
    return jnp.maximum(x, 0.0) + jnp.log(1.0 + jnp.exp(-jnp.abs(x)))


def _log_sigmoid(x):
    return -_softplus(-x)


def _gelu_tanh(x):
    return 0.5 * x * (1.0 + jnp.tanh(math.sqrt(2.0 / math.pi) * (x + 0.044715 * (x * x * x))))


def _rms(x):
    return x * lax.rsqrt(jnp.mean(x * x, axis=-1, keepdims=True) + NORM_EPS)


def _norm_mod(x, g, shift, scale):
    return (_rms(x) * g) * (1.0 + scale) + shift


def _pick_tile(limit, *dims):
    t = limit
    while t > SUBLANES and any(d % t for d in dims):
        t //= 2
    assert all(d % t == 0 for d in dims), (limit, dims)
    return t


def _ada_kernel(c_ref, w_ref, b_ref, o_ref):
    s = _silu(c_ref[...]).astype(BF16)
    o_ref[...] = _dot(s, w_ref[...].astype(BF16)) + b_ref[...]


def _ada_mod(cc, ada_w, ada_b):
    depth, d, n = ada_w.shape
    rows = cc.shape[0]
    tn = _pick_tile(ADA_COL_TILE, n)
    return pl.pallas_call(
        _ada_kernel,
        grid=(depth, n // tn),
        in_specs=[
            pl.BlockSpec((rows, d), lambda l, j: (0, 0)),
            pl.BlockSpec((None, d, tn), lambda l, j: (l, 0, j)),
            pl.BlockSpec((None, 1, tn), lambda l, j: (l, 0, j)),
        ],
        out_specs=pl.BlockSpec((None, rows, tn), lambda l, j: (l, 0, j)),
        out_shape=jax.ShapeDtypeStruct((depth, rows, n), F32),
        name="ada_mod",
        compiler_params=_params(("parallel", "parallel")),
    )(cc, ada_w, ada_b.reshape(depth, 1, n))


class _Rows:
    def __init__(self, batch, seq, ctx_len, tm):
        self.batch, self.seq, self.ctx = batch, seq, ctx_len
        self.tm = tm
        self.n_lat = batch * seq // tm
        self.n_ctx = batch * ctx_len // tm
        self.per_batch = seq // tm
        self.rows_lat = batch * seq
        self.rows_all = batch * (seq + ctx_len)

    def mod_row(self, i):
        return jnp.where(i < self.n_lat, i // self.per_batch, self.batch)

    def rope_row(self, i):
        return jnp.where(i < self.n_lat, i % self.per_batch, self.per_batch)


INPROJ_ROW_CHUNK = 256


def _row_source(refs, n_lat):
    if len(refs) == 1:
        return lambda rs: refs[0][rs, :]
    is_lat = pl.program_id(0) < n_lat
    return lambda rs: jnp.where(is_lat, refs[0][rs, :], refs[1][rs, :])


def _row_specs(h, rows, width):
    tm, n_lat = rows.tm, rows.n_lat
    if not isinstance(h, tuple):
        return [pl.BlockSpec((tm, width), lambda i, *_: (i, 0))], [h]
    return [pl.BlockSpec((tm, width), lambda i, *_: (jnp.minimum(i, n_lat - 1), 0)),
            pl.BlockSpec((tm, width), lambda i, *_: (jnp.maximum(i - n_lat, 0), 0),
                         pipeline_mode=pl.Buffered(1))], list(h)


def _inproj_even_kernel(*refs, n_lat):
    *h_refs, mod_ref, g_ref, w_ref, cos_ref, sin_ref, o_ref, u_ref = refs
    h_rows = _row_source(h_refs, n_lat)
    j = pl.program_id(1)
    tm, tn = o_ref.shape
    rc = min(INPROJ_ROW_CHUNK, tm)

    def step(first):
        w = w_ref[...]
        for r in range(tm // rc):
            rs = slice(r * rc, (r + 1) * rc)
            if first:
                u = _norm_mod(h_rows(rs), g_ref[...], mod_ref[0:1, :], mod_ref[1:2, :]).astype(BF16)
                u_ref[rs, :] = u
            else:
                u = u_ref[rs, :]
            x = _dot(u, w)
            cos, sin = cos_ref[rs, :], sin_ref[rs, :]
            for g in range(tn // LANES):
                xg = x[:, g * LANES:(g + 1) * LANES]
                y = xg * cos + pltpu.roll(xg, LANES // 2, 1) * sin
                o_ref[rs, g * LANES:(g + 1) * LANES] = y.astype(o_ref.dtype)

    @pl.when(j == 0)
    def _():
        step(True)

    @pl.when(j > 0)
    def _():
        step(False)


def _inproj_even(h, mods, norm_g, w, layer, rope, rows, a_width):
    _, d, n = w.shape
    tm = rows.tm
    tn = _pick_tile(EVEN_COL_TILE, a_width, n)
    cos, sin = rope
    nq, nqk = a_width // tn, 2 * a_width // tn

    def col_type(j):
        return (j >= nq).astype(jnp.int32) + (j >= nqk).astype(jnp.int32)

    tab = pl.BlockSpec((tm, LANES), lambda i, j: (rows.rope_row(i), col_type(j)))
    h_specs, h_args = _row_specs(h, rows, d)
    return pl.pallas_call(
        functools.partial(_inproj_even_kernel, n_lat=rows.n_lat),
        grid=(rows.n_lat + rows.n_ctx, n // tn),
        in_specs=h_specs + [
            pl.BlockSpec((None, N_MOD, d), lambda i, j: (rows.mod_row(i), 0, 0)),
            pl.BlockSpec((1, d), lambda i, j: (0, 0)),
            pl.BlockSpec((None, d, tn), lambda i, j: (layer, 0, j)),
            tab, tab,
        ],
        out_specs=pl.BlockSpec((tm, tn), lambda i, j: (i, j)),
        out_shape=jax.ShapeDtypeStruct((rows.rows_all, n), BF16),
        scratch_shapes=[pltpu.VMEM((tm, d), BF16)],
        name="inproj_even",
        compiler_params=_params(("parallel", "arbitrary")),
    )(*h_args, mods, norm_g.reshape(1, d), w, cos, sin)


def _inproj_odd_kernel(h_ref, mod_ref, g_ref, w_ref, wz_ref, o_ref, z_ref, u_ref, *, key_w):
    j = pl.program_id(1)
    tm, tn = o_ref.shape
    rc = min(INPROJ_ROW_CHUNK, tm)
    col = j * tn + lax.broadcasted_iota(jnp.int32, (1, tn), 1)
    col_scale = jnp.where(col < key_w, C_KEY_DIM ** -0.5, 1.0).astype(F32)

    def step(first):
        w = w_ref[...]
        for r in range(tm // rc):
            rs = slice(r * rc, (r + 1) * rc)
            if first:
                u = _norm_mod(h_ref[rs, :], g_ref[...], mod_ref[0:1, :], mod_ref[1:2, :]).astype(BF16)
                u_ref[rs, :] = u
                zl = lax.broadcasted_iota(jnp.int32, (rc, LANES), 1)
                z_ref[rs, :] = jnp.where(zl < 2 * C_GATE_RANK, _dot(u, wz_ref[...]), 0.0)
            else:
                u = u_ref[rs, :]
            o_ref[rs, :] = (_dot(u, w) * col_scale).astype(o_ref.dtype)

    @pl.when(j == 0)
    def _():
        step(True)

    @pl.when(j > 0)
    def _():
        step(False)


def _inproj_odd(h, mods, norm_g, w, layer, rows, key_w, n):
    d = w.shape[1]
    assert n % LANES == 0 and w.shape[2] == n + 2 * C_GATE_RANK
    tm = rows.tm
    tn = _pick_tile(ODD_COL_TILE, n)
    return pl.pallas_call(
        functools.partial(_inproj_odd_kernel, key_w=key_w),
        grid=(rows.n_lat + rows.n_ctx, n // tn),
        in_specs=[
            pl.BlockSpec((tm, d), lambda i, j: (i, 0)),
            pl.BlockSpec((None, N_MOD, d), lambda i, j: (rows.mod_row(i), 0, 0)),
            pl.BlockSpec((1, d), lambda i, j: (0, 0)),
            pl.BlockSpec((None, d, tn), lambda i, j: (layer, 0, j)),
            pl.BlockSpec((None, d, LANES), lambda i, j: (layer, 0, n // LANES)),
        ],
        out_specs=[
            pl.BlockSpec((tm, tn), lambda i, j: (i, j)),
            pl.BlockSpec((tm, LANES), lambda i, j: (i, 0)),
        ],
        out_shape=[
            jax.ShapeDtypeStruct((rows.rows_all, n), BF16),
            jax.ShapeDtypeStruct((rows.rows_all, LANES), F32),
        ],
        scratch_shapes=[pltpu.VMEM((tm, d), BF16)],
        name="inproj_odd",
        compiler_params=_params(("parallel", "arbitrary")),
    )(h, mods, norm_g.reshape(1, d), w, w)


def _attn_kernel(lam_ref, ql_ref, qc_ref, kl_ref, vl_ref, kc_ref, vc_ref, g_ref, ol_ref, oc_ref,
                 k_s, v_s, s0_s, s1_s, p0_s, p1_s, *, tq, kt, post_scale):
    ctx_len, seq = kc_ref.shape[0], kl_ref.shape[0]
    total = ctx_len + seq
    nkt, nq, ng = total // kt, seq // tq, kt // LANES
    s_bufs, p_bufs = (s0_s, s1_s), (p0_s, p1_s)
    lam = lam_ref[0]
    gain = g_ref[...] * post_scale

    for src_ref, lo in ((kc_ref, 0), (kl_ref, ctx_len)):
        for c in range(src_ref.shape[0] // LANES):
            k_s[:, lo + c * LANES:lo + (c + 1) * LANES] = src_ref[c * LANES:(c + 1) * LANES, :].T
    v_s[0:ctx_len, 0:LANES] = vc_ref[...]
    v_s[ctx_len:total, 0:LANES] = vl_ref[...]
    v_s[:, LANES:2 * LANES] = jnp.ones((total, LANES), BF16)

    def map_mask(shape, mi):
        lane = lax.broadcasted_iota(jnp.int32, shape, 1)
        return ((lane // (A_HEAD_DIM // 2)) % 2) == mi

    def q_map(t, mi):
        q = ql_ref[pl.ds(pl.multiple_of(t * tq, tq), tq), :]
        return jnp.where(map_mask(q.shape, mi), q, jnp.zeros_like(q))

    def scores(qa, buf, j, m):
        s = _dot(qa, k_s[:, j * kt:(j + 1) * kt])
        s_bufs[buf][:, j * kt:(j + 1) * kt] = s
        for g in range(ng):
            m = jnp.maximum(m, s[:, g * LANES:(g + 1) * LANES])
        return m

    def exps(buf, j, mb):
        x = s_bufs[buf][:, j * kt:(j + 1) * kt]
        ps = [jnp.exp2(x[:, g * LANES:(g + 1) * LANES] - mb).astype(BF16) for g in range(ng)]
        p_bufs[buf][:, j * kt:(j + 1) * kt] = ps[0] if ng == 1 else jnp.concatenate(ps, axis=1)

    def values(buf):
        return _dot(p_bufs[buf][...], v_s[...])

    def row_max(m):
        return jnp.broadcast_to(jnp.max(m, axis=-1, keepdims=True), m.shape)

    def finish(t, r0, r1):
        o = r0[:, 0:LANES] / r0[:, LANES:2 * LANES] - lam * (r1[:, 0:LANES] / r1[:, LANES:2 * LANES])
        ol_ref[pl.ds(pl.multiple_of(t * tq, tq), tq), :] = (_rms(o) * gain).astype(ol_ref.dtype)

    neg = jnp.full((tq, LANES), -jnp.inf, F32)

    qa = q_map(0, 0)
    m = neg
    for j in range(nkt):
        m = scores(qa, 0, j, m)
    mx0 = row_max(m)
    qb = q_map(0, 1)
    m = neg
    for j in range(nkt):
        m = scores(qb, 1, j, m)
        exps(0, j, mx0)

    def tile_body(t, carry):
        m1, r0p, r1p = carry
        finish(jnp.maximum(t - 1, 0), r0p, r1p)
        mx1 = row_max(m1)
        tn = jnp.minimum(t + 1, nq - 1)
        qa = q_map(tn, 0)
        m = neg
        for j in range(nkt):
            m = scores(qa, 0, j, m)
        mx0n = row_max(m)
        for j in range(nkt):
            exps(1, j, mx1)
        r0 = values(0)
        r1 = values(1)
        qb = q_map(tn, 1)
        m = neg
        for j in range(nkt):
            m = scores(qb, 1, j, m)
            exps(0, j, mx0n)
        return m, r0, r1

    ones = jnp.ones((tq, 2 * LANES), F32)
    _, r0, r1 = lax.fori_loop(0, nq, tile_body, (m, ones, ones))
    finish(nq - 1, r0, r1)

    qc = qc_ref[...]
    outs = []
    for mi in range(2):
        qm = jnp.where(map_mask(qc.shape, mi), qc, jnp.zeros_like(qc))
        s = _dot_nt(qm, kc_ref[...])
        p = jnp.exp2(s - jnp.max(s, axis=-1, keepdims=True))
        outs.append(_dot(p.astype(BF16), vc_ref[...]) / jnp.sum(p, axis=-1, keepdims=True))
    oc_ref[...] = (_rms(outs[0] - lam * outs[1]) * gain).astype(oc_ref.dtype)


def _diff_attn(p, lam, subln_g, post_scale, batch, seq, ctx_len, heads):
    total = seq + ctx_len
    tq = _pick_tile(ATTN_Q_TILE, seq)
    kt = _pick_tile(ATTN_KEY_TILE, total)
    nkt = total // kt
    n_ctx_blk0 = batch * seq // ctx_len
    kern = functools.partial(_attn_kernel, tq=tq, kt=kt, post_scale=post_scale)

    def lat(off):
        return pl.BlockSpec((seq, LANES), lambda b, h: (b, off + h))

    def ctx(off):
        return pl.BlockSpec((ctx_len, LANES), lambda b, h: (n_ctx_blk0 + b, off + h))

    return pl.pallas_call(
        kern,
        grid=(batch, heads),
        in_specs=[
            pl.BlockSpec(memory_space=pltpu.SMEM),
            lat(0), ctx(0), lat(heads), lat(2 * heads), ctx(heads), ctx(2 * heads),
            pl.BlockSpec((1, LANES), lambda b, h: (0, 0)),
        ],
        out_specs=[
            pl.BlockSpec((seq, LANES), lambda b, h: (b, h)),
            pl.BlockSpec((ctx_len, LANES), lambda b, h: (b, h)),
        ],
        out_shape=[
            jax.ShapeDtypeStruct((batch * seq, heads * LANES), BF16),
            jax.ShapeDtypeStruct((batch * ctx_len, heads * LANES), BF16),
        ],
        scratch_shapes=[
            pltpu.VMEM((LANES, total), BF16),
            pltpu.VMEM((total, 2 * LANES), BF16),
            pltpu.VMEM((tq, total), F32),
            pltpu.VMEM((tq, total), F32),
            pltpu.VMEM((tq, total), BF16),
            pltpu.VMEM((tq, total), BF16),
        ],
        name="diff_attn",
        compiler_params=_params(("parallel", "parallel")),
    )(lam, p, p, p, p, p, p, subln_g.reshape(1, LANES))


def _dwconv(x, w, b):
    n = x.shape[0]
    row = lax.broadcasted_iota(jnp.int32, x.shape, 0)
    y = b + x * w[CONV_LEFT:CONV_LEFT + 1, :]
    for j in range(CONV_W):
        off = j - CONV_LEFT
        if off == 0:
            continue
        xs = pltpu.roll(x, (-off) % n, 0)
        ok = jnp.logical_and(row + off >= 0, row + off < n)
        y = y + jnp.where(ok, xs, 0.0) * w[j:j + 1, :]
    return y


def _lru_kernel(recl_ref, recc_ref, gtl_ref, gtc_ref, cw_ref, cb_ref, wa_ref, ba_ref, wx_ref, bx_ref,
                lam_ref, ol_ref, oc_ref, a_s, b_s, h_s, *, ctx_len, seq):
    total = ctx_len + seq
    nt = total // SUBLANES
    cw = cw_ref[...]
    cb = cb_ref[...]
    region_off = ((0, seq), (ctx_len, 0))
    decay_rate = [LRU_C * _softplus(-lam_ref[d]) for d in range(2)]
    for ri, rec_ref in enumerate((recc_ref, recl_ref)):
        n = rec_ref.shape[0]
        xc = _dwconv(rec_ref[...].astype(F32), cw, cb)
        xcb = xc.astype(BF16)
        pre = [(_dot(xcb, wa_ref[d].astype(BF16)), _dot(xcb, wx_ref[d].astype(BF16))) for d in range(2)]
        for d in range(2):
            r = _sigmoid(pre[d][0] + ba_ref[d])
            gi = _sigmoid(pre[d][1] + bx_ref[d])
            a = jnp.exp(-(r * decay_rate[d]))
            lo = region_off[ri][d]
            a_s[d, lo:lo + n, :] = a
            b_s[d, lo:lo + n, :] = jnp.sqrt(1.0 - a * a) * gi * xc

    row = lax.broadcasted_iota(jnp.int32, (SUBLANES, LANES), 0)

    def scan_tile(a, b, reverse):
        s = 1
        while s < SUBLANES:
            sh = (SUBLANES - s) if reverse else s
            a_sh = pltpu.roll(a, sh, 0)
            b_sh = pltpu.roll(b, sh, 0)
            ok = (row < SUBLANES - s) if reverse else (row >= s)
            b = jnp.where(ok, a * b_sh + b, b)
            a = jnp.where(ok, a * a_sh, a)
            s *= 2
        return a, b

    def body(p, carry):
        cf, cbk = carry
        tf = pl.multiple_of(p * SUBLANES, SUBLANES)
        a, b = scan_tile(a_s[0, pl.ds(tf, SUBLANES), :], b_s[0, pl.ds(tf, SUBLANES), :], False)
        hf = a * cf + b
        h_s[0, pl.ds(tf, SUBLANES), :] = hf
        tb = pl.multiple_of((nt - 1 - p) * SUBLANES, SUBLANES)
        a, b = scan_tile(a_s[1, pl.ds(tb, SUBLANES), :], b_s[1, pl.ds(tb, SUBLANES), :], True)
        hb = a * cbk + b
        h_s[1, pl.ds(tb, SUBLANES), :] = hb
        return hf[SUBLANES - 1:SUBLANES, :], hb[0:1, :]

    zero = jnp.zeros((1, LANES), F32)
    lax.fori_loop(0, nt, body, (zero, zero), unroll=4)

    oc_ref[...] = (_gelu_tanh(gtc_ref[...].astype(F32))
                   * (h_s[0, 0:ctx_len, :] + h_s[1, seq:total, :])).astype(oc_ref.dtype)
    ol_ref[...] = (_gelu_tanh(gtl_ref[...].astype(F32))
                   * (h_s[0, ctx_len:total, :] + h_s[1, 0:seq, :])).astype(ol_ref.dtype)


def _rglru(p, conv_w, conv_b, wa, ba, wx, bx, lru_lam, batch, seq, ctx_len, a_width, b_width):
    nb = b_width // B_BLOCK_DIM
    gt_off = 3 * a_width // LANES
    rec_off = (3 * a_width + b_width) // LANES
    n_ctx_blk0 = batch * seq // ctx_len
    total = seq + ctx_len
    kern = functools.partial(_lru_kernel, ctx_len=ctx_len, seq=seq)
    vec = pl.BlockSpec((2, 1, LANES), lambda b, n: (0, 0, n))
    mat = pl.BlockSpec((2, None, B_BLOCK_DIM, B_BLOCK_DIM), lambda b, n: (0, n, 0, 0))
    return pl.pallas_call(
        kern,
        grid=(batch, nb),
        in_specs=[
            pl.BlockSpec((seq, LANES), lambda b, n: (b, rec_off + n)),
            pl.BlockSpec((ctx_len, LANES), lambda b, n: (n_ctx_blk0 + b, rec_off + n)),
            pl.BlockSpec((seq, LANES), lambda b, n: (b, gt_off + n)),
            pl.BlockSpec((ctx_len, LANES), lambda b, n: (n_ctx_blk0 + b, gt_off + n)),
            pl.BlockSpec((CONV_W, LANES), lambda b, n: (0, n)),
            pl.BlockSpec((1, LANES), lambda b, n: (0, n)),
            mat, vec, mat, vec, vec,
        ],
        out_specs=[
            pl.BlockSpec((seq, LANES), lambda b, n: (b, n)),
            pl.BlockSpec((ctx_len, LANES), lambda b, n: (b, n)),
        ],
        out_shape=[
            jax.ShapeDtypeStruct((batch * seq, b_width), BF16),
            jax.ShapeDtypeStruct((batch * ctx_len, b_width), BF16),
        ],
        scratch_shapes=[
            pltpu.VMEM((2, total, LANES), F32),
            pltpu.VMEM((2, total, LANES), F32),
            pltpu.VMEM((2, total, LANES), F32),
        ],
        name="rglru",
        compiler_params=_params(("parallel", "parallel")),
    )(p, p, p, p, conv_w, conv_b.reshape(1, b_width), wa, ba.reshape(2, 1, b_width),
      wx, bx.reshape(2, 1, b_width), lru_lam.reshape(2, 1, b_width))


GLA_UNROLL = 4


def _gla_kernel(ql_ref, kl_ref, vl_ref, rl_ref, zl_ref, qc_ref, kc_ref, vc_ref, rc_ref, zc_ref,
                w2_ref, gb_ref, g_ref, al_ref, ac_ref, st_s, ol_s, oc_s, *, ctx_len, seq):
    c = C_CHUNK
    ri = lax.broadcasted_iota(jnp.int32, (c, c), 0)
    ci = lax.broadcasted_iota(jnp.int32, (c, c), 1)
    keep = (ri >= ci, ri <= ci)
    tri = tuple(jnp.where(k, 1.0, 0.0).astype(BF16) for k in keep)
    end_row = (c - 1, 0)

    def step(q_ref, k_ref, v_ref, z_ref, o_ref, chains, accumulate):
        rows = [pl.ds(t0, c) for t0, _ in chains]
        dirs = [d for _, d in chains]
        xs = [_dot(z_ref[r, :].astype(BF16), w2_ref[d]) + gb_ref[d] for r, d in zip(rows, dirs)]
        gs = [_log_sigmoid(x) * (1.0 / C_GATE_TAU) for x in xs]
        his = [g.astype(BF16) for g in gs]
        los = [(g - hi.astype(F32)).astype(BF16) for g, hi in zip(gs, his)]
        bcums = [_dot(tri[d], hi) + _dot(tri[d], lo) for d, hi, lo in zip(dirs, his, los)]
        qes, kes, kds, decays = [], [], [], []
        for r, d, bcum in zip(rows, dirs, bcums):
            b_end = bcum[end_row[d]:end_row[d] + 1, :]
            k = k_ref[r, :].astype(F32)
            qes.append((q_ref[r, :].astype(F32) * jnp.exp(bcum)).astype(BF16))
            kes.append((k * jnp.exp(-bcum)).astype(BF16))
            kds.append((k * jnp.exp(b_end - bcum)).astype(BF16))
            decays.append(jnp.exp(b_end))
        raw = [_dot_nt(qe, ke) for qe, ke in zip(qes, kes)]
        ups = [_dot_tn(v_ref[r, :], kd) for r, kd in zip(rows, kds)]
        atts = [jnp.where(keep[d], a, 0.0).astype(BF16) for d, a in zip(dirs, raw)]
        intra = [_dot(att, v_ref[r, :]) for att, r in zip(atts, rows)]
        for r, d, qe, up, decay, oi in zip(rows, dirs, qes, ups, decays, intra):
            st = st_s[d]
            o = oi + _dot_nt(qe, st.astype(BF16))
            st_s[d] = st * decay + up
            if accumulate:
                o_ref[r, :] += o
            else:
                o_ref[r, :] = o

    def sweep(q_ref, k_ref, v_ref, z_ref, o_ref, n):
        half = n // 2
        unroll = math.gcd(GLA_UNROLL, half)

        def run(lo, accumulate):
            def body(i, carry):
                chains = []
                for u in range(unroll):
                    ci_ = i * unroll + u
                    chains.append((pl.multiple_of(ci_ * c, c), 0))
                    chains.append((pl.multiple_of((n - 1 - ci_) * c, c), 1))
                step(q_ref, k_ref, v_ref, z_ref, o_ref, chains, accumulate)
                return carry
            lax.fori_loop(lo // unroll, (lo + half) // unroll, body, 0)

        run(0, False)
        run(half, True)

    st_s[...] = jnp.zeros_like(st_s)
    sweep(qc_ref, kc_ref, vc_ref, zc_ref, oc_s, ctx_len // c)
    sweep(ql_ref, kl_ref, vl_ref, zl_ref, ol_s, seq // c)

    g = g_ref[...]
    ac_ref[...] = (_rms(oc_s[...]) * g * _silu(rc_ref[...].astype(F32))).astype(ac_ref.dtype)
    al_ref[...] = (_rms(ol_s[...]) * g * _silu(rl_ref[...].astype(F32))).astype(al_ref.dtype)


def _gla(p, z, w2p, gate_b, norm_g, batch, seq, ctx_len, heads):
    d_model = heads * C_VAL_DIM
    key_w = heads * C_KEY_DIM
    n_ctx_blk0 = batch * seq // ctx_len
    kern = functools.partial(_gla_kernel, ctx_len=ctx_len, seq=seq)

    def lat(width, off):
        return pl.BlockSpec((seq, width), lambda b, h: (b, off + h))

    def ctx(width, off):
        return pl.BlockSpec((ctx_len, width), lambda b, h: (n_ctx_blk0 + b, off + h))

    specs = [
        lat(C_KEY_DIM, 0), lat(C_KEY_DIM, heads), lat(C_VAL_DIM, heads), lat(C_VAL_DIM, 2 * heads),
        pl.BlockSpec((seq, LANES), lambda b, h: (b, 0)),
        ctx(C_KEY_DIM, 0), ctx(C_KEY_DIM, heads), ctx(C_VAL_DIM, heads), ctx(C_VAL_DIM, 2 * heads),
        pl.BlockSpec((ctx_len, LANES), lambda b, h: (n_ctx_blk0 + b, 0)),
    ]
    specs += [
        pl.BlockSpec((2, LANES, C_KEY_DIM), lambda b, h: (0, 0, h)),
        pl.BlockSpec((2, 1, C_KEY_DIM), lambda b, h: (0, 0, h)),
        pl.BlockSpec((1, C_VAL_DIM), lambda b, h: (0, 0)),
    ]
    return pl.pallas_call(
        kern,
        grid=(batch, heads),
        in_specs=specs,
        out_specs=[
            pl.BlockSpec((seq, C_VAL_DIM), lambda b, h: (b, h)),
            pl.BlockSpec((ctx_len, C_VAL_DIM), lambda b, h: (b, h)),
        ],
        out_shape=[
            jax.ShapeDtypeStruct((batch * seq, d_model), BF16),
            jax.ShapeDtypeStruct((batch * ctx_len, d_model), BF16),
        ],
        scratch_shapes=[
            pltpu.VMEM((2, C_VAL_DIM, C_KEY_DIM), F32),
            pltpu.VMEM((seq, C_VAL_DIM), F32),
            pltpu.VMEM((ctx_len, C_VAL_DIM), F32),
        ],
        name="gla",
        compiler_params=_params(("parallel", "parallel")),
    )(p, p, p, p, z, p, p, p, p, z, w2p, gate_b.reshape(2, 1, key_w), norm_g.reshape(1, C_VAL_DIM))


def _outproj_kernel(al0_ref, ac0_ref, al1_ref, ac1_ref, w0_ref, w1_ref, *refs, n_lat):
    *h_refs, mod_ref, o_ref = refs
    is_lat = pl.program_id(0) < n_lat
    a0 = jnp.where(is_lat, al0_ref[...], ac0_ref[...])
    a1 = jnp.where(is_lat, al1_ref[...], ac1_ref[...])
    acc = _dot(a0, w0_ref[...]) + _dot(a1, w1_ref[...])
    o_ref[...] = _row_source(h_refs, n_lat)(slice(None)) + mod_ref[2:3, :] * acc


def _outproj(srcs, w, layer, h, mods, rows, n_tiles):
    d = w.shape[2]
    kh = w.shape[1] // 2
    tm = rows.tm
    n_lat = rows.n_lat
    specs, args = [], []
    for lat_arr, lat_col, ctx_arr, ctx_row0, ctx_col in srcs:
        specs.append(pl.BlockSpec((tm, kh), lambda i, c=lat_col: (jnp.minimum(i, n_lat - 1), c)))
        specs.append(pl.BlockSpec((tm, kh), lambda i, r=ctx_row0, c=ctx_col: (r + jnp.maximum(i - n_lat, 0), c)))
        args += [lat_arr, ctx_arr]
    kern = functools.partial(_outproj_kernel, n_lat=n_lat)
    h_specs, h_args = _row_specs(h, rows, d)
    return pl.pallas_call(
        kern,
        grid=(n_tiles,),
        in_specs=specs + [
            pl.BlockSpec((None, kh, d), lambda i: (layer, 0, 0)),
            pl.BlockSpec((None, kh, d), lambda i: (layer, 1, 0)),
        ] + h_specs + [
            pl.BlockSpec((None, N_MOD, d), lambda i: (rows.mod_row(i), 0, 0)),
        ],
        out_specs=pl.BlockSpec((tm, d), lambda i: (i, 0)),
        out_shape=jax.ShapeDtypeStruct((n_tiles * tm, d), F32),
        name="outproj",
        compiler_params=_params(("parallel",)),
    )(*args, w, w, *h_args, mods)


MLP_ROW_CHUNK = 512


def _mlp_kernel(h_ref, mod_ref, g_ref, w1_ref, w2_ref, fg_ref, o_ref, u_ref, *, nf, final_norm):
    j = pl.program_id(1)
    tm = o_ref.shape[0]
    rc = min(MLP_ROW_CHUNK, tm)

    def step(first):
        w1, w2 = w1_ref[...], w2_ref[...]
        for r in range(tm // rc):
            rs = slice(r * rc, (r + 1) * rc)
            if first:
                u = _norm_mod(h_ref[rs, :], g_ref[...], mod_ref[3:4, :], mod_ref[4:5, :]).astype(BF16)
                u_ref[rs, :] = u
            else:
                u = u_ref[rs, :]
            t = jnp.maximum(_dot(u, w1), 0.0)
            part = _dot((t * t).astype(BF16), w2)
            if first:
                o_ref[rs, :] = part
            else:
                o_ref[rs, :] += part

    @pl.when(j == 0)
    def _():
        step(True)

    @pl.when(j > 0)
    def _():
        step(False)

    @pl.when(j == nf - 1)
    def _():
        y = h_ref[...] + mod_ref[5:6, :] * o_ref[...]
        if final_norm:
            y = _rms(y) * fg_ref[...]
        o_ref[...] = y


def _mlp(h, mods, norm_g, w1, w2, layer, final_g, rows, n_tiles, tf, final_norm):
    _, d, ff = w1.shape
    nf = ff // tf
    tm = rows.tm
    kern = functools.partial(_mlp_kernel, nf=nf, final_norm=final_norm)
    return pl.pallas_call(
        kern,
        grid=(n_tiles, nf),
        in_specs=[
            pl.BlockSpec((tm, d), lambda i, j: (i, 0)),
            pl.BlockSpec((None, N_MOD, d), lambda i, j: (rows.mod_row(i), 0, 0)),
            pl.BlockSpec((1, d), lambda i, j: (0, 0)),
            pl.BlockSpec((None, d, tf), lambda i, j: (layer, 0, j)),
            pl.BlockSpec((None, tf, d), lambda i, j: (layer, j, 0)),
            pl.BlockSpec((1, d), lambda i, j: (0, 0)),
        ],
        out_specs=pl.BlockSpec((tm, d), lambda i, j: (i, 0)),
        out_shape=jax.ShapeDtypeStruct((n_tiles * tm, d), F32),
        scratch_shapes=[pltpu.VMEM((tm, d), BF16)],
        name="mlp",
        compiler_params=_params(("parallel", "arbitrary")),
    )(h, mods, norm_g.reshape(1, d), w1, w2, final_g.reshape(1, d))


def _rope_tables(seq, pad_rows):
    inv = ROPE_BASE ** (-jnp.arange(ROPE_FREQS, dtype=F32) / ROPE_FREQS)
    t = jnp.arange(seq, dtype=jnp.int32)
    pos_r = (t // GRID_W).astype(F32)[:, None]
    pos_c = (t % GRID_W).astype(F32)[:, None]
    lane = jnp.arange(LANES, dtype=jnp.int32)[None, :]
    second = (lane // A_HEAD_DIM) == 1
    is_col = ((lane % (2 * ROPE_FREQS)) // ROPE_FREQS) == 1
    ang = jnp.where(is_col, pos_c, pos_r) * inv[lane % ROPE_FREQS]
    cos = jnp.cos(ang)
    sin = jnp.where(second, jnp.sin(ang), -jnp.sin(ang))
    qs = A_HEAD_DIM ** -0.5 * math.log2(math.e)
    ones = jnp.ones((pad_rows, LANES), F32)
    zeros = jnp.zeros((pad_rows, LANES), F32)
    one_all = jnp.ones((seq + pad_rows, LANES), F32)
    zero_all = jnp.zeros((seq + pad_rows, LANES), F32)

    def table(rot, pad):
        body = jnp.concatenate([rot, pad])
        return jnp.concatenate([qs * body, body, one_all if pad is ones else zero_all], axis=1)

    return table(cos, ones), table(sin, zeros)


def _permute_qk_columns(w, a_width):
    lead = w.shape[:-1]
    qk = w[..., :2 * a_width].reshape(*lead, -1, 2, 2, 2, ROPE_FREQS)
    qk = jnp.swapaxes(jnp.swapaxes(qk, -2, -3), -3, -4)
    return jnp.concatenate([qk.reshape(*lead, 2 * a_width), w[..., 2 * a_width:]], axis=-1)


def kernel(x, c, ctx, c_ctx, ada_w, ada_b, norm1_g, norm2_g, mlp_w1, mlp_w2, ev_w_in, ev_w_out, ev_lambda_q1, ev_lambda_k1, ev_lambda_q2, ev_lambda_k2, ev_subln_g, ev_conv_w, ev_conv_b, ev_lru_wa, ev_lru_ba, ev_lru_wx, ev_lru_bx, ev_lru_lam, od_w_in, od_w_out, od_gate_w2, od_gate_b, od_norm_g, final_g):
    batch, seq, d = x.shape
    ctx_len = ctx.shape[1]
    depth = ada_w.shape[0]
    a_width = d // 2
    b_width = d - a_width
    a_heads = a_width // (2 * A_HEAD_DIM)
    c_heads = d // C_VAL_DIM
    key_w = c_heads * C_KEY_DIM

    tm = _pick_tile(ROW_TILE, seq, batch * ctx_len)
    rows = _Rows(batch, seq, ctx_len, tm)
    rows_out = _Rows(batch, seq, ctx_len, _pick_tile(OUT_ROW_TILE, seq, batch * ctx_len))
    tf = _pick_tile(FF_TILE, mlp_w1.shape[2])

    pad = (-(batch + 1)) % SUBLANES
    cc = jnp.concatenate([c, c_ctx[None, :], jnp.zeros((pad, d), F32)], axis=0)
    mods_all = _ada_mod(cc, ada_w, ada_b).reshape(depth, batch + 1 + pad, N_MOD, d)

    rope = _rope_tables(seq, tm)
    h = (x.reshape(batch * seq, d), ctx.reshape(batch * ctx_len, d))

    ev_w_in_b = _permute_qk_columns(ev_w_in, a_width).astype(BF16)
    od_w_in_b = od_w_in.astype(BF16)
    ev_w_out_b, od_w_out_b = ev_w_out.astype(BF16), od_w_out.astype(BF16)
    mlp_w1_b, mlp_w2_b = mlp_w1.astype(BF16), mlp_w2.astype(BF16)

    for i in range(depth):
        last = i == depth - 1
        mods = mods_all[i]
        j = i // 2
        if i % 2 == 0:
            lam_init = 0.8 - 0.6 * math.exp(-0.3 * i)
            lam = (jnp.exp(jnp.sum(ev_lambda_q1[j] * ev_lambda_k1[j]))
                   - jnp.exp(jnp.sum(ev_lambda_q2[j] * ev_lambda_k2[j])) + lam_init).reshape(1).astype(F32)
            p = _inproj_even(h, mods, norm1_g[i], ev_w_in_b, j, rope, rows, a_width)
            ya_lat, ya_ctx = _diff_attn(p, lam, ev_subln_g[j], 1.0 - lam_init, batch, seq, ctx_len, a_heads)
            yb_lat, yb_ctx = _rglru(p, ev_conv_w[j], ev_conv_b[j], ev_lru_wa[j], ev_lru_ba[j],
                                    ev_lru_wx[j], ev_lru_bx[j], ev_lru_lam[j],
                                    batch, seq, ctx_len, a_width, b_width)
            srcs = [(ya_lat, 0, ya_ctx, 0, 0), (yb_lat, 0, yb_ctx, 0, 0)]
            w_out = ev_w_out_b
        else:
            n_main = 2 * key_w + 2 * d
            p, z = _inproj_odd(h, mods, norm1_g[i], od_w_in_b, j, rows, key_w, n_main)
            w2p = jnp.zeros((2, LANES, key_w), F32)
            for dd in range(2):
                w2p = w2p.at[dd, dd * C_GATE_RANK:(dd + 1) * C_GATE_RANK, :].set(od_gate_w2[j, dd])
            a_lat, a_ctx = _gla(p, z, w2p.astype(BF16), od_gate_b[j], od_norm_g[j], batch, seq, ctx_len, c_heads)
            srcs = [(a_lat, 0, a_ctx, 0, 0), (a_lat, 1, a_ctx, 0, 1)]
            w_out = od_w_out_b

        n_out = rows_out.n_lat if last else rows_out.n_lat + rows_out.n_ctx
        h = _outproj(srcs, w_out, j, h, mods, rows_out, n_out)
        n_mlp = rows.n_lat if last else rows.n_lat + rows.n_ctx
        h = _mlp(h, mods, norm2_g[i], mlp_w1_b, mlp_w2_b, i, final_g, rows, n_mlp, tf, last)

    return h.reshape(batch, seq, d)
```

```python
import functools
import math

import jax
import jax.numpy as jnp
from jax import lax
from jax.experimental import pallas as pl
from jax.experimental.pallas import tpu as pltpu

F32 = jnp.float32
BF16 = jnp.bfloat16

NORM_EPS = 1e-6
N_MOD = 6
GRID_W = 64
A_HEAD_DIM = 64
ROPE_FREQS = A_HEAD_DIM // 4
ROPE_BASE = 10000.0
B_BLOCK_DIM = 128
CONV_W = 4
CONV_LEFT = 2
LRU_C = 8.0
C_KEY_DIM = 256
C_VAL_DIM = 512
C_GATE_RANK = 16
C_GATE_TAU = 16.0
C_CHUNK = 64

LANES = 128
SUBLANES = 8
VMEM_LIMIT = 56 * 1024 * 1024

ROW_TILE = 1024
OUT_ROW_TILE = 512
ADA_COL_TILE = 1024
EVEN_COL_TILE = 1024
ODD_COL_TILE = 2048
FF_TILE = 1024
ATTN_Q_TILE = 256
ATTN_KEY_TILE = 256


def _params(sem, vmem=VMEM_LIMIT):
    return pltpu.CompilerParams(dimension_semantics=sem, vmem_limit_bytes=vmem)


def _dot(a, b):
    return jnp.dot(a, b, preferred_element_type=F32)


def _dot_nt(a, b):
    return lax.dot_general(a, b, (((1,), (1,)), ((), ())), preferred_element_type=F32)


def _dot_tn(a, b):
    return lax.dot_general(a, b, (((0,), (0,)), ((), ())), preferred_element_type=F32)


def _sigmoid(x):
    return 1.0 / (1.0 + jnp.exp(-x))


def _silu(x):
    return x * _sigmoid(x)


def _softplus(x):
    return jnp.maximum(x, 0.0) + jnp.log(1.0 + jnp.exp(-jnp.abs(x)))


def _log_sigmoid(x):
    return -_softplus(-x)


def _gelu_tanh(x):
    return 0.5 * x * (1.0 + jnp.tanh(math.sqrt(2.0 / math.pi) * (x + 0.044715 * (x * x * x))))


def _rms(x):
    return x * lax.rsqrt(jnp.mean(x * x, axis=-1, keepdims=True) + NORM_EPS)


def _norm_mod(x, g, shift, scale):
    return (_rms(x) * g) * (1.0 + scale) + shift


def _pick_tile(limit, *dims):
    t = limit
    while t > SUBLANES and any(d % t for d in dims):
        t //= 2
    assert all(d % t == 0 for d in dims), (limit, dims)
    return t


def _ada_kernel(c_ref, w_ref, b_ref, o_ref):
    s = _silu(c_ref[...]).astype(BF16)
    o_ref[...] = _dot(s, w_ref[...].astype(BF16)) + b_ref[...]


def _ada_mod(cc, ada_w, ada_b):
    depth, d, n = ada_w.shape
    rows = cc.shape[0]
    tn = _pick_tile(ADA_COL_TILE, n)
    return pl.pallas_call(
        _ada_kernel,
        grid=(depth, n // tn),
        in_specs=[
            pl.BlockSpec((rows, d), lambda l, j: (0, 0)),
            pl.BlockSpec((None, d, tn), lambda l, j: (l, 0, j)),
            pl.BlockSpec((None, 1, tn), lambda l, j: (l, 0, j)),
        ],
        out_specs=pl.BlockSpec((None, rows, tn), lambda l, j: (l, 0, j)),
        out_shape=jax.ShapeDtypeStruct((depth, rows, n), F32),
        name="ada_mod",
        compiler_params=_params(("parallel", "parallel")),
    )(cc, ada_w, ada_b.reshape(depth, 1, n))


class _Rows:
    def __init__(self, batch, seq, ctx_len, tm):
        self.batch, self.seq, self.ctx = batch, seq, ctx_len
        self.tm = tm
        self.n_lat = batch * seq // tm
        self.n_ctx = batch * ctx_len // tm
        self.per_batch = seq // tm
        self.rows_lat = batch * seq
        self.rows_all = batch * (seq + ctx_len)

    def mod_row(self, i):
        return jnp.where(i < self.n_lat, i // self.per_batch, self.batch)

    def rope_row(self, i):
        return jnp.where(i < self.n_lat, i % self.per_batch, self.per_batch)


INPROJ_ROW_CHUNK = 256


def _row_source(refs, n_lat):
    if len(refs) == 1:
        return lambda rs: refs[0][rs, :]
    is_lat = pl.program_id(0) < n_lat
    return lambda rs: jnp.where(is_lat, refs[0][rs, :], refs[1][rs, :])


def _row_specs(h, rows, width):
    tm, n_lat = rows.tm, rows.n_lat
    if not isinstance(h, tuple):
        return [pl.BlockSpec((tm, width), lambda i, *_: (i, 0))], [h]
    return [pl.BlockSpec((tm, width), lambda i, *_: (jnp.minimum(i, n_lat - 1), 0)),
            pl.BlockSpec((tm, width), lambda i, *_: (jnp.maximum(i - n_lat, 0), 0),
                         pipeline_mode=pl.Buffered(1))], list(h)


def _inproj_even_kernel(*refs, n_lat):
    *h_refs, mod_ref, g_ref, w_ref, cos_ref, sin_ref, o_ref, u_ref = refs
    h_rows = _row_source(h_refs, n_lat)
    j = pl.program_id(1)
    tm, tn = o_ref.shape
    rc = min(INPROJ_ROW_CHUNK, tm)

    def step(first):
        w = w_ref[...]
        for r in range(tm // rc):
            rs = slice(r * rc, (r + 1) * rc)
            if first:
                u = _norm_mod(h_rows(rs), g_ref[...], mod_ref[0:1, :], mod_ref[1:2, :]).astype(BF16)
                u_ref[rs, :] = u
            else:
                u = u_ref[rs, :]
            x = _dot(u, w)
            cos, sin = cos_ref[rs, :], sin_ref[rs, :]
            for g in range(tn // LANES):
                xg = x[:, g * LANES:(g + 1) * LANES]
                y = xg * cos + pltpu.roll(xg, LANES // 2, 1) * sin
                o_ref[rs, g * LANES:(g + 1) * LANES] = y.astype(o_ref.dtype)

    @pl.when(j == 0)
    def _():
        step(True)

    @pl.when(j > 0)
    def _():
        step(False)


def _inproj_even(h, mods, norm_g, w, layer, rope, rows, a_width):
    _, d, n = w.shape
    tm = rows.tm
    tn = _pick_tile(EVEN_COL_TILE, a_width, n)
    cos, sin = rope
    nq, nqk = a_width // tn, 2 * a_width // tn

    def col_type(j):
        return (j >= nq).astype(jnp.int32) + (j >= nqk).astype(jnp.int32)

    tab = pl.BlockSpec((tm, LANES), lambda i, j: (rows.rope_row(i), col_type(j)))
    h_specs, h_args = _row_specs(h, rows, d)
    return pl.pallas_call(
        functools.partial(_inproj_even_kernel, n_lat=rows.n_lat),
        grid=(rows.n_lat + rows.n_ctx, n // tn),
        in_specs=h_specs + [
            pl.BlockSpec((None, N_MOD, d), lambda i, j: (rows.mod_row(i), 0, 0)),
            pl.BlockSpec((1, d), lambda i, j: (0, 0)),
            pl.BlockSpec((None, d, tn), lambda i, j: (layer, 0, j)),
            tab, tab,
        ],
        out_specs=pl.BlockSpec((tm, tn), lambda i, j: (i, j)),
        out_shape=jax.ShapeDtypeStruct((rows.rows_all, n), BF16),
        scratch_shapes=[pltpu.VMEM((tm, d), BF16)],
        name="inproj_even",
        compiler_params=_params(("parallel", "arbitrary")),
    )(*h_args, mods, norm_g.reshape(1, d), w, cos, sin)


def _inproj_odd_kernel(h_ref, mod_ref, g_ref, w_ref, wz_ref, o_ref, z_ref, u_ref, *, key_w):
    j = pl.program_id(1)
    tm, tn = o_ref.shape
    rc = min(INPROJ_ROW_CHUNK, tm)
    col = j * tn + lax.broadcasted_iota(jnp.int32, (1, tn), 1)
    col_scale = jnp.where(col < key_w, C_KEY_DIM ** -0.5, 1.0).astype(F32)

    def step(first):
        w = w_ref[...]
        for r in range(tm // rc):
            rs = slice(r * rc, (r + 1) * rc)
            if first:
                u = _norm_mod(h_ref[rs, :], g_ref[...], mod_ref[0:1, :], mod_ref[1:2, :]).astype(BF16)
                u_ref[rs, :] = u
                zl = lax.broadcasted_iota(jnp.int32, (rc, LANES), 1)
                z_ref[rs, :] = jnp.where(zl < 2 * C_GATE_RANK, _dot(u, wz_ref[...]), 0.0)
            else:
                u = u_ref[rs, :]
            o_ref[rs, :] = (_dot(u, w) * col_scale).astype(o_ref.dtype)

    @pl.when(j == 0)
    def _():
        step(True)

    @pl.when(j > 0)
    def _():
        step(False)


def _inproj_odd(h, mods, norm_g, w, layer, rows, key_w, n):
    d = w.shape[1]
    assert n % LANES == 0 and w.shape[2] == n + 2 * C_GATE_RANK
    tm = rows.tm
    tn = _pick_tile(ODD_COL_TILE, n)
    return pl.pallas_call(
        functools.partial(_inproj_odd_kernel, key_w=key_w),
        grid=(rows.n_lat + rows.n_ctx, n // tn),
        in_specs=[
            pl.BlockSpec((tm, d), lambda i, j: (i, 0)),
            pl.BlockSpec((None, N_MOD, d), lambda i, j: (rows.mod_row(i), 0, 0)),
            pl.BlockSpec((1, d), lambda i, j: (0, 0)),
            pl.BlockSpec((None, d, tn), lambda i, j: (layer, 0, j)),
            pl.BlockSpec((None, d, LANES), lambda i, j: (layer, 0, n // LANES)),
        ],
        out_specs=[
            pl.BlockSpec((tm, tn), lambda i, j: (i, j)),
            pl.BlockSpec((tm, LANES), lambda i, j: (i, 0)),
        ],
        out_shape=[
            jax.ShapeDtypeStruct((rows.rows_all, n), BF16),
            jax.ShapeDtypeStruct((rows.rows_all, LANES), F32),
        ],
        scratch_shapes=[pltpu.VMEM((tm, d), BF16)],
        name="inproj_odd",
        compiler_params=_params(("parallel", "arbitrary")),
    )(h, mods, norm_g.reshape(1, d), w, w)


def _attn_kernel(lam_ref, ql_ref, qc_ref, kl_ref, vl_ref, kc_ref, vc_ref, g_ref, ol_ref, oc_ref,
                 k_s, v_s, s0_s, s1_s, p0_s, p1_s, *, tq, kt, post_scale):
    ctx_len, seq = kc_ref.shape[0], kl_ref.shape[0]
    total = ctx_len + seq
    nkt, nq, ng = total // kt, seq // tq, kt // LANES
    s_bufs, p_bufs = (s0_s, s1_s), (p0_s, p1_s)
    lam = lam_ref[0]
    gain = g_ref[...] * post_scale

    for src_ref, lo in ((kc_ref, 0), (kl_ref, ctx_len)):
        for c in range(src_ref.shape[0] // LANES):
            k_s[:, lo + c * LANES:lo + (c + 1) * LANES] = src_ref[c * LANES:(c + 1) * LANES, :].T
    v_s[0:ctx_len, 0:LANES] = vc_ref[...]
    v_s[ctx_len:total, 0:LANES] = vl_ref[...]
    v_s[:, LANES:2 * LANES] = jnp.ones((total, LANES), BF16)

    def map_mask(shape, mi):
        lane = lax.broadcasted_iota(jnp.int32, shape, 1)
        return ((lane // (A_HEAD_DIM // 2)) % 2) == mi

    def q_map(t, mi):
        q = ql_ref[pl.ds(pl.multiple_of(t * tq, tq), tq), :]
        return jnp.where(map_mask(q.shape, mi), q, jnp.zeros_like(q))

    def scores(qa, buf, j, m):
        s = _dot(qa, k_s[:, j * kt:(j + 1) * kt])
        s_bufs[buf][:, j * kt:(j + 1) * kt] = s
        for g in range(ng):
            m = jnp.maximum(m, s[:, g * LANES:(g + 1) * LANES])
        return m

    def exps(buf, j, mb):
        x = s_bufs[buf][:, j * kt:(j + 1) * kt]
        ps = [jnp.exp2(x[:, g * LANES:(g + 1) * LANES] - mb).astype(BF16) for g in range(ng)]
        p_bufs[buf][:, j * kt:(j + 1) * kt] = ps[0] if ng == 1 else jnp.concatenate(ps, axis=1)

    def values(buf):
        return _dot(p_bufs[buf][...], v_s[...])

    def row_max(m):
        return jnp.broadcast_to(jnp.max(m, axis=-1, keepdims=True), m.shape)

    def finish(t, r0, r1):
        o = r0[:, 0:LANES] / r0[:, LANES:2 * LANES] - lam * (r1[:, 0:LANES] / r1[:, LANES:2 * LANES])
        ol_ref[pl.ds(pl.multiple_of(t * tq, tq), tq), :] = (_rms(o) * gain).astype(ol_ref.dtype)

    neg = jnp.full((tq, LANES), -jnp.inf, F32)

    qa = q_map(0, 0)
    m = neg
    for j in range(nkt):
        m = scores(qa, 0, j, m)
    mx0 = row_max(m)
    qb = q_map(0, 1)
    m = neg
    for j in range(nkt):
        m = scores(qb, 1, j, m)
        exps(0, j, mx0)

    def tile_body(t, carry):
        m1, r0p, r1p = carry
        finish(jnp.maximum(t - 1, 0), r0p, r1p)
        mx1 = row_max(m1)
        tn = jnp.minimum(t + 1, nq - 1)
        qa = q_map(tn, 0)
        m = neg
        for j in range(nkt):
            m = scores(qa, 0, j, m)
        mx0n = row_max(m)
        for j in range(nkt):
            exps(1, j, mx1)
        r0 = values(0)
        r1 = values(1)
        qb = q_map(tn, 1)
        m = neg
        for j in range(nkt):
            m = scores(qb, 1, j, m)
            exps(0, j, mx0n)
        return m, r0, r1

    ones = jnp.ones((tq, 2 * LANES), F32)
    _, r0, r1 = lax.fori_loop(0, nq, tile_body, (m, ones, ones))
    finish(nq - 1, r0, r1)

    qc = qc_ref[...]
    outs = []
    for mi in range(2):
        qm = jnp.where(map_mask(qc.shape, mi), qc, jnp.zeros_like(qc))
        s = _dot_nt(qm, kc_ref[...])
        p = jnp.exp2(s - jnp.max(s, axis=-1, keepdims=True))
        outs.append(_dot(p.astype(BF16), vc_ref[...]) / jnp.sum(p, axis=-1, keepdims=True))
    oc_ref[...] = (_rms(outs[0] - lam * outs[1]) * gain).astype(oc_ref.dtype)


def _diff_attn(p, lam, subln_g, post_scale, batch, seq, ctx_len, heads):
    total = seq + ctx_len
    tq = _pick_tile(ATTN_Q_TILE, seq)
    kt = _pick_tile(ATTN_KEY_TILE, total)
    nkt = total // kt
    n_ctx_blk0 = batch * seq // ctx_len
    kern = functools.partial(_attn_kernel, tq=tq, kt=kt, post_scale=post_scale)

    def lat(off):
        return pl.BlockSpec((seq, LANES), lambda b, h: (b, off + h))

    def ctx(off):
        return pl.BlockSpec((ctx_len, LANES), lambda b, h: (n_ctx_blk0 + b, off + h))

    return pl.pallas_call(
        kern,
        grid=(batch, heads),
        in_specs=[
            pl.BlockSpec(memory_space=pltpu.SMEM),
            lat(0), ctx(0), lat(heads), lat(2 * heads), ctx(heads), ctx(2 * heads),
            pl.BlockSpec((1, LANES), lambda b, h: (0, 0)),
        ],
        out_specs=[
            pl.BlockSpec((seq, LANES), lambda b, h: (b, h)),
            pl.BlockSpec((ctx_len, LANES), lambda b, h: (b, h)),
        ],
        out_shape=[
            jax.ShapeDtypeStruct((batch * seq, heads * LANES), BF16),
            jax.ShapeDtypeStruct((batch * ctx_len, heads * LANES), BF16),
        ],
        scratch_shapes=[
            pltpu.VMEM((LANES, total), BF16),
            pltpu.VMEM((total, 2 * LANES), BF16),
            pltpu.VMEM((tq, total), F32),
            pltpu.VMEM((tq, total), F32),
            pltpu.VMEM((tq, total), BF16),
            pltpu.VMEM((tq, total), BF16),
        ],
        name="diff_attn",
        compiler_params=_params(("parallel", "parallel")),
    )(lam, p, p, p, p, p, p, subln_g.reshape(1, LANES))


def _dwconv(x, w, b):
    n = x.shape[0]
    row = lax.broadcasted_iota(jnp.int32, x.shape, 0)
    y = b + x * w[CONV_LEFT:CONV_LEFT + 1, :]
    for j in range(CONV_W):
        off = j - CONV_LEFT
        if off == 0:
            continue
        xs = pltpu.roll(x, (-off) % n, 0)
        ok = jnp.logical_and(row + off >= 0, row + off < n)
        y = y + jnp.where(ok, xs, 0.0) * w[j:j + 1, :]
    return y


def _lru_kernel(recl_ref, recc_ref, gtl_ref, gtc_ref, cw_ref, cb_ref, wa_ref, ba_ref, wx_ref, bx_ref,
                lam_ref, ol_ref, oc_ref, a_s, b_s, h_s, *, ctx_len, seq):
    total = ctx_len + seq
    nt = total // SUBLANES
    cw = cw_ref[...]
    cb = cb_ref[...]
    region_off = ((0, seq), (ctx_len, 0))
    decay_rate = [LRU_C * _softplus(-lam_ref[d]) for d in range(2)]
    for ri, rec_ref in enumerate((recc_ref, recl_ref)):
        n = rec_ref.shape[0]
        xc = _dwconv(rec_ref[...].astype(F32), cw, cb)
        xcb = xc.astype(BF16)
        pre = [(_dot(xcb, wa_ref[d].astype(BF16)), _dot(xcb, wx_ref[d].astype(BF16))) for d in range(2)]
        for d in range(2):
            r = _sigmoid(pre[d][0] + ba_ref[d])
            gi = _sigmoid(pre[d][1] + bx_ref[d])
            a = jnp.exp(-(r * decay_rate[d]))
            lo = region_off[ri][d]
            a_s[d, lo:lo + n, :] = a
            b_s[d, lo:lo + n, :] = jnp.sqrt(1.0 - a * a) * gi * xc

    row = lax.broadcasted_iota(jnp.int32, (SUBLANES, LANES), 0)

    def scan_tile(a, b, reverse):
        s = 1
        while s < SUBLANES:
            sh = (SUBLANES - s) if reverse else s
            a_sh = pltpu.roll(a, sh, 0)
            b_sh = pltpu.roll(b, sh, 0)
            ok = (row < SUBLANES - s) if reverse else (row >= s)
            b = jnp.where(ok, a * b_sh + b, b)
            a = jnp.where(ok, a * a_sh, a)
            s *= 2
        return a, b

    def body(p, carry):
        cf, cbk = carry
        tf = pl.multiple_of(p * SUBLANES, SUBLANES)
        a, b = scan_tile(a_s[0, pl.ds(tf, SUBLANES), :], b_s[0, pl.ds(tf, SUBLANES), :], False)
        hf = a * cf + b
        h_s[0, pl.ds(tf, SUBLANES), :] = hf
        tb = pl.multiple_of((nt - 1 - p) * SUBLANES, SUBLANES)
        a, b = scan_tile(a_s[1, pl.ds(tb, SUBLANES), :], b_s[1, pl.ds(tb, SUBLANES), :], True)
        hb = a * cbk + b
        h_s[1, pl.ds(tb, SUBLANES), :] = hb
        return hf[SUBLANES - 1:SUBLANES, :], hb[0:1, :]

    zero = jnp.zeros((1, LANES), F32)
    lax.fori_loop(0, nt, body, (zero, zero), unroll=4)

    oc_ref[...] = (_gelu_tanh(gtc_ref[...].astype(F32))
                   * (h_s[0, 0:ctx_len, :] + h_s[1, seq:total, :])).astype(oc_ref.dtype)
    ol_ref[...] = (_gelu_tanh(gtl_ref[...].astype(F32))
                   * (h_s[0, ctx_len:total, :] + h_s[1, 0:seq, :])).astype(ol_ref.dtype)


def _rglru(p, conv_w, conv_b, wa, ba, wx, bx, lru_lam, batch, seq, ctx_len, a_width, b_width):
    nb = b_width // B_BLOCK_DIM
    gt_off = 3 * a_width // LANES
    rec_off = (3 * a_width + b_width) // LANES
    n_ctx_blk0 = batch * seq // ctx_len
    total = seq + ctx_len
    kern = functools.partial(_lru_kernel, ctx_len=ctx_len, seq=seq)
    vec = pl.BlockSpec((2, 1, LANES), lambda b, n: (0, 0, n))
    mat = pl.BlockSpec((2, None, B_BLOCK_DIM, B_BLOCK_DIM), lambda b, n: (0, n, 0, 0))
    return pl.pallas_call(
        kern,
        grid=(batch, nb),
        in_specs=[
            pl.BlockSpec((seq, LANES), lambda b, n: (b, rec_off + n)),
            pl.BlockSpec((ctx_len, LANES), lambda b, n: (n_ctx_blk0 + b, rec_off + n)),
            pl.BlockSpec((seq, LANES), lambda b, n: (b, gt_off + n)),
            pl.BlockSpec((ctx_len, LANES), lambda b, n: (n_ctx_blk0 + b, gt_off + n)),
            pl.BlockSpec((CONV_W, LANES), lambda b, n: (0, n)),
            pl.BlockSpec((1, LANES), lambda b, n: (0, n)),
            mat, vec, mat, vec, vec,
        ],
        out_specs=[
            pl.BlockSpec((seq, LANES), lambda b, n: (b, n)),
            pl.BlockSpec((ctx_len, LANES), lambda b, n: (b, n)),
        ],
        out_shape=[
            jax.ShapeDtypeStruct((batch * seq, b_width), BF16),
            jax.ShapeDtypeStruct((batch * ctx_len, b_width), BF16),
        ],
        scratch_shapes=[
            pltpu.VMEM((2, total, LANES), F32),
            pltpu.VMEM((2, total, LANES), F32),
            pltpu.VMEM((2, total, LANES), F32),
        ],
        name="rglru",
        compiler_params=_params(("parallel", "parallel")),
    )(p, p, p, p, conv_w, conv_b.reshape(1, b_width), wa, ba.reshape(2, 1, b_width),
      wx, bx.reshape(2, 1, b_width), lru_lam.reshape(2, 1, b_width))


GLA_UNROLL = 4


def _gla_kernel(ql_ref, kl_ref, vl_ref, rl_ref, zl_ref, qc_ref, kc_ref, vc_ref, rc_ref, zc_ref,
                w2_ref, gb_ref, g_ref, al_ref, ac_ref, st_s, ol_s, oc_s, *, ctx_len, seq):
    c = C_CHUNK
    ri = lax.broadcasted_iota(jnp.int32, (c, c), 0)
    ci = lax.broadcasted_iota(jnp.int32, (c, c), 1)
    keep = (ri >= ci, ri <= ci)
    tri = tuple(jnp.where(k, 1.0, 0.0).astype(BF16) for k in keep)
    end_row = (c - 1, 0)

    def step(q_ref, k_ref, v_ref, z_ref, o_ref, chains, accumulate):
        rows = [pl.ds(t0, c) for t0, _ in chains]
        dirs = [d for _, d in chains]
        xs = [_dot(z_ref[r, :].astype(BF16), w2_ref[d]) + gb_ref[d] for r, d in zip(rows, dirs)]
        gs = [_log_sigmoid(x) * (1.0 / C_GATE_TAU) for x in xs]
        his = [g.astype(BF16) for g in gs]
        los = [(g - hi.astype(F32)).astype(BF16) for g, hi in zip(gs, his)]
        bcums = [_dot(tri[d], hi) + _dot(tri[d], lo) for d, hi, lo in zip(dirs, his, los)]
        qes, kes, kds, decays = [], [], [], []
        for r, d, bcum in zip(rows, dirs, bcums):
            b_end = bcum[end_row[d]:end_row[d] + 1, :]
            k = k_ref[r, :].astype(F32)
            qes.append((q_ref[r, :].astype(F32) * jnp.exp(bcum)).astype(BF16))
            kes.append((k * jnp.exp(-bcum)).astype(BF16))
            kds.append((k * jnp.exp(b_end - bcum)).astype(BF16))
            decays.append(jnp.exp(b_end))
        raw = [_dot_nt(qe, ke) for qe, ke in zip(qes, kes)]
        ups = [_dot_tn(v_ref[r, :], kd) for r, kd in zip(rows, kds)]
        atts = [jnp.where(keep[d], a, 0.0).astype(BF16) for d, a in zip(dirs, raw)]
        intra = [_dot(att, v_ref[r, :]) for att, r in zip(atts, rows)]
        for r, d, qe, up, decay, oi in zip(rows, dirs, qes, ups, decays, intra):
            st = st_s[d]
            o = oi + _dot_nt(qe, st.astype(BF16))
            st_s[d] = st * decay + up
            if accumulate:
                o_ref[r, :] += o
            else:
                o_ref[r, :] = o

    def sweep(q_ref, k_ref, v_ref, z_ref, o_ref, n):
        half = n // 2
        unroll = math.gcd(GLA_UNROLL, half)

        def run(lo, accumulate):
            def body(i, carry):
                chains = []
                for u in range(unroll):
                    ci_ = i * unroll + u
                    chains.append((pl.multiple_of(ci_ * c, c), 0))
                    chains.append((pl.multiple_of((n - 1 - ci_) * c, c), 1))
                step(q_ref, k_ref, v_ref, z_ref, o_ref, chains, accumulate)
                return carry
            lax.fori_loop(lo // unroll, (lo + half) // unroll, body, 0)

        run(0, False)
        run(half, True)

    st_s[...] = jnp.zeros_like(st_s)
    sweep(qc_ref, kc_ref, vc_ref, zc_ref, oc_s, ctx_len // c)
    sweep(ql_ref, kl_ref, vl_ref, zl_ref, ol_s, seq // c)

    g = g_ref[...]
    ac_ref[...] = (_rms(oc_s[...]) * g * _silu(rc_ref[...].astype(F32))).astype(ac_ref.dtype)
    al_ref[...] = (_rms(ol_s[...]) * g * _silu(rl_ref[...].astype(F32))).astype(al_ref.dtype)


def _gla(p, z, w2p, gate_b, norm_g, batch, seq, ctx_len, heads):
    d_model = heads * C_VAL_DIM
    key_w = heads * C_KEY_DIM
    n_ctx_blk0 = batch * seq // ctx_len
    kern = functools.partial(_gla_kernel, ctx_len=ctx_len, seq=seq)

    def lat(width, off):
        return pl.BlockSpec((seq, width), lambda b, h: (b, off + h))

    def ctx(width, off):
        return pl.BlockSpec((ctx_len, width), lambda b, h: (n_ctx_blk0 + b, off + h))

    specs = [
        lat(C_KEY_DIM, 0), lat(C_KEY_DIM, heads), lat(C_VAL_DIM, heads), lat(C_VAL_DIM, 2 * heads),
        pl.BlockSpec((seq, LANES), lambda b, h: (b, 0)),
        ctx(C_KEY_DIM, 0), ctx(C_KEY_DIM, heads), ctx(C_VAL_DIM, heads), ctx(C_VAL_DIM, 2 * heads),
        pl.BlockSpec((ctx_len, LANES), lambda b, h: (n_ctx_blk0 + b, 0)),
    ]
    specs += [
        pl.BlockSpec((2, LANES, C_KEY_DIM), lambda b, h: (0, 0, h)),
        pl.BlockSpec((2, 1, C_KEY_DIM), lambda b, h: (0, 0, h)),
        pl.BlockSpec((1, C_VAL_DIM), lambda b, h: (0, 0)),
    ]
    return pl.pallas_call(
        kern,
        grid=(batch, heads),
        in_specs=specs,
        out_specs=[
            pl.BlockSpec((seq, C_VAL_DIM), lambda b, h: (b, h)),
            pl.BlockSpec((ctx_len, C_VAL_DIM), lambda b, h: (b, h)),
        ],
        out_shape=[
            jax.ShapeDtypeStruct((batch * seq, d_model), BF16),
            jax.ShapeDtypeStruct((batch * ctx_len, d_model), BF16),
        ],
        scratch_shapes=[
            pltpu.VMEM((2, C_VAL_DIM, C_KEY_DIM), F32),
            pltpu.VMEM((seq, C_VAL_DIM), F32),
            pltpu.VMEM((ctx_len, C_VAL_DIM), F32),
        ],
        name="gla",
        compiler_params=_params(("parallel", "parallel")),
    )(p, p, p, p, z, p, p, p, p, z, w2p, gate_b.reshape(2, 1, key_w), norm_g.reshape(1, C_VAL_DIM))


def _outproj_kernel(al0_ref, ac0_ref, al1_ref, ac1_ref, w0_ref, w1_ref, *refs, n_lat):
    *h_refs, mod_ref, o_ref = refs
    is_lat = pl.program_id(0) < n_lat
    a0 = jnp.where(is_lat, al0_ref[...], ac0_ref[...])
    a1 = jnp.where(is_lat, al1_ref[...], ac1_ref[...])
    acc = _dot(a0, w0_ref[...]) + _dot(a1, w1_ref[...])
    o_ref[...] = _row_source(h_refs, n_lat)(slice(None)) + mod_ref[2:3, :] * acc


def _outproj(srcs, w, layer, h, mods, rows, n_tiles):
    d = w.shape[2]
    kh = w.shape[1] // 2
    tm = rows.tm
    n_lat = rows.n_lat
    specs, args = [], []
    for lat_arr, lat_col, ctx_arr, ctx_row0, ctx_col in srcs:
        specs.append(pl.BlockSpec((tm, kh), lambda i, c=lat_col: (jnp.minimum(i, n_lat - 1), c)))
        specs.append(pl.BlockSpec((tm, kh), lambda i, r=ctx_row0, c=ctx_col: (r + jnp.maximum(i - n_lat, 0), c)))
        args += [lat_arr, ctx_arr]
    kern = functools.partial(_outproj_kernel, n_lat=n_lat)
    h_specs, h_args = _row_specs(h, rows, d)
    return pl.pallas_call(
        kern,
        grid=(n_tiles,),
        in_specs=specs + [
            pl.BlockSpec((None, kh, d), lambda i: (layer, 0, 0)),
            pl.BlockSpec((None, kh, d), lambda i: (layer, 1, 0)),
        ] + h_specs + [
            pl.BlockSpec((None, N_MOD, d), lambda i: (rows.mod_row(i), 0, 0)),
        ],
        out_specs=pl.BlockSpec((tm, d), lambda i: (i, 0)),
        out_shape=jax.ShapeDtypeStruct((n_tiles * tm, d), F32),
        name="outproj",
        compiler_params=_params(("parallel",)),
    )(*args, w, w, *h_args, mods)


MLP_ROW_CHUNK = 512


def _mlp_kernel(h_ref, mod_ref, g_ref, w1_ref, w2_ref, fg_ref, o_ref, u_ref, *, nf, final_norm):
    j = pl.program_id(1)
    tm = o_ref.shape[0]
    rc = min(MLP_ROW_CHUNK, tm)

    def step(first):
        w1, w2 = w1_ref[...], w2_ref[...]
        for r in range(tm // rc):
            rs = slice(r * rc, (r + 1) * rc)
            if first:
                u = _norm_mod(h_ref[rs, :], g_ref[...], mod_ref[3:4, :], mod_ref[4:5, :]).astype(BF16)
                u_ref[rs, :] = u
            else:
                u = u_ref[rs, :]
            t = jnp.maximum(_dot(u, w1), 0.0)
            part = _dot((t * t).astype(BF16), w2)
            if first:
                o_ref[rs, :] = part
            else:
                o_ref[rs, :] += part

    @pl.when(j == 0)
    def _():
        step(True)

    @pl.when(j > 0)
    def _():
        step(False)

    @pl.when(j == nf - 1)
    def _():
        y = h_ref[...] + mod_ref[5:6, :] * o_ref[...]
        if final_norm:
            y = _rms(y) * fg_ref[...]
        o_ref[...] = y


def _mlp(h, mods, norm_g, w1, w2, layer, final_g, rows, n_tiles, tf, final_norm):
    _, d, ff = w1.shape
    nf = ff // tf
    tm = rows.tm
    kern = functools.partial(_mlp_kernel, nf=nf, final_norm=final_norm)
    return pl.pallas_call(
        kern,
        grid=(n_tiles, nf),
        in_specs=[
            pl.BlockSpec((tm, d), lambda i, j: (i, 0), pipeline_mode=pl.Buffered(1)),
            pl.BlockSpec((None, N_MOD, d), lambda i, j: (rows.mod_row(i), 0, 0)),
            pl.BlockSpec((1, d), lambda i, j: (0, 0)),
            pl.BlockSpec((None, d, tf), lambda i, j: (layer, 0, j)),
            pl.BlockSpec((None, tf, d), lambda i, j: (layer, j, 0)),
            pl.BlockSpec((1, d), lambda i, j: (0, 0)),
        ],
        out_specs=pl.BlockSpec((tm, d), lambda i, j: (i, 0)),
        out_shape=jax.ShapeDtypeStruct((n_tiles * tm, d), F32),
        scratch_shapes=[pltpu.VMEM((tm, d), BF16)],
        name="mlp",
        compiler_params=_params(("parallel", "arbitrary")),
    )(h, mods, norm_g.reshape(1, d), w1, w2, final_g.reshape(1, d))


def _rope_tables(seq, pad_rows):
    inv = ROPE_BASE ** (-jnp.arange(ROPE_FREQS, dtype=F32) / ROPE_FREQS)
    t = jnp.arange(seq, dtype=jnp.int32)
    pos_r = (t // GRID_W).astype(F32)[:, None]
    pos_c = (t % GRID_W).astype(F32)[:, None]
    lane = jnp.arange(LANES, dtype=jnp.int32)[None, :]
    second = (lane // A_HEAD_DIM) == 1
    is_col = ((lane % (2 * ROPE_FREQS)) // ROPE_FREQS) == 1
    ang = jnp.where(is_col, pos_c, pos_r) * inv[lane % ROPE_FREQS]
    cos = jnp.cos(ang)
    sin = jnp.where(second, jnp.sin(ang), -jnp.sin(ang))
    qs = A_HEAD_DIM ** -0.5 * math.log2(math.e)
    ones = jnp.ones((pad_rows, LANES), F32)
    zeros = jnp.zeros((pad_rows, LANES), F32)
    one_all = jnp.ones((seq + pad_rows, LANES), F32)
    zero_all = jnp.zeros((seq + pad_rows, LANES), F32)

    def table(rot, pad):
        body = jnp.concatenate([rot, pad])
        return jnp.concatenate([qs * body, body, one_all if pad is ones else zero_all], axis=1)

    return table(cos, ones), table(sin, zeros)


def _permute_qk_columns(w, a_width):
    lead = w.shape[:-1]
    qk = w[..., :2 * a_width].reshape(*lead, -1, 2, 2, 2, ROPE_FREQS)
    qk = jnp.swapaxes(jnp.swapaxes(qk, -2, -3), -3, -4)
    return jnp.concatenate([qk.reshape(*lead, 2 * a_width), w[..., 2 * a_width:]], axis=-1)


def kernel(x, c, ctx, c_ctx, ada_w, ada_b, norm1_g, norm2_g, mlp_w1, mlp_w2, ev_w_in, ev_w_out, ev_lambda_q1, ev_lambda_k1, ev_lambda_q2, ev_lambda_k2, ev_subln_g, ev_conv_w, ev_conv_b, ev_lru_wa, ev_lru_ba, ev_lru_wx, ev_lru_bx, ev_lru_lam, od_w_in, od_w_out, od_gate_w2, od_gate_b, od_norm_g, final_g):
    batch, seq, d = x.shape
    ctx_len = ctx.shape[1]
    depth = ada_w.shape[0]
    a_width = d // 2
    b_width = d - a_width
    a_heads = a_width // (2 * A_HEAD_DIM)
    c_heads = d // C_VAL_DIM
    key_w = c_heads * C_KEY_DIM

    tm = _pick_tile(ROW_TILE, seq, batch * ctx_len)
    rows = _Rows(batch, seq, ctx_len, tm)
    rows_out = _Rows(batch, seq, ctx_len, _pick_tile(OUT_ROW_TILE, seq, batch * ctx_len))
    tf = _pick_tile(FF_TILE, mlp_w1.shape[2])

    pad = (-(batch + 1)) % SUBLANES
    cc = jnp.concatenate([c, c_ctx[None, :], jnp.zeros((pad, d), F32)], axis=0)
    mods_all = _ada_mod(cc, ada_w, ada_b).reshape(depth, batch + 1 + pad, N_MOD, d)

    rope = _rope_tables(seq, tm)
    h = (x.reshape(batch * seq, d), ctx.reshape(batch * ctx_len, d))

    ev_w_in_b = _permute_qk_columns(ev_w_in, a_width).astype(BF16)
    od_w_in_b = od_w_in.astype(BF16)
    ev_w_out_b, od_w_out_b = ev_w_out.astype(BF16), od_w_out.astype(BF16)
    mlp_w1_b, mlp_w2_b = mlp_w1.astype(BF16), mlp_w2.astype(BF16)

    for i in range(depth):
        last = i == depth - 1
        mods = mods_all[i]
        j = i // 2
        if i % 2 == 0:
            lam_init = 0.8 - 0.6 * math.exp(-0.3 * i)
            lam = (jnp.exp(jnp.sum(ev_lambda_q1[j] * ev_lambda_k1[j]))
                   - jnp.exp(jnp.sum(ev_lambda_q2[j] * ev_lambda_k2[j])) + lam_init).reshape(1).astype(F32)
            p = _inproj_even(h, mods, norm1_g[i], ev_w_in_b, j, rope, rows, a_width)
            ya_lat, ya_ctx = _diff_attn(p, lam, ev_subln_g[j], 1.0 - lam_init, batch, seq, ctx_len, a_heads)
            yb_lat, yb_ctx = _rglru(p, ev_conv_w[j], ev_conv_b[j], ev_lru_wa[j], ev_lru_ba[j],
                                    ev_lru_wx[j], ev_lru_bx[j], ev_lru_lam[j],
                                    batch, seq, ctx_len, a_width, b_width)
            srcs = [(ya_lat, 0, ya_ctx, 0, 0), (yb_lat, 0, yb_ctx, 0, 0)]
            w_out = ev_w_out_b
        else:
            n_main = 2 * key_w + 2 * d
            p, z = _inproj_odd(h, mods, norm1_g[i], od_w_in_b, j, rows, key_w, n_main)
            w2p = jnp.zeros((2, LANES, key_w), F32)
            for dd in range(2):
                w2p = w2p.at[dd, dd * C_GATE_RANK:(dd + 1) * C_GATE_RANK, :].set(od_gate_w2[j, dd])
            a_lat, a_ctx = _gla(p, z, w2p.astype(BF16), od_gate_b[j], od_norm_g[j], batch, seq, ctx_len, c_heads)
            srcs = [(a_lat, 0, a_ctx, 0, 0), (a_lat, 1, a_ctx, 0, 1)]
            w_out = od_w_out_b

        n_out = rows_out.n_lat if last else rows_out.n_lat + rows_out.n_ctx
        h = _outproj(srcs, w_out, j, h, mods, rows_out, n_out)
        n_mlp = rows.n_lat if last else rows.n_lat + rows.n_ctx
        h = _mlp(h, mods, norm2_g[i], mlp_w1_b, mlp_w2_b, i, final_g, rows, n_mlp, tf, last)

    return h.reshape(batch, seq, d)
```

```python
import functools
import math

import jax
import jax.numpy as jnp
from jax import lax
from jax.experimental import pallas as pl
from jax.experimental.pallas import tpu as pltpu

F32 = jnp.float32
BF16 = jnp.bfloat16

NORM_EPS = 1e-6
N_MOD = 6
GRID_W = 64
A_HEAD_DIM = 64
ROPE_FREQS = A_HEAD_DIM // 4
ROPE_BASE = 10000.0
B_BLOCK_DIM = 128
CONV_W = 4
CONV_LEFT = 2
LRU_C = 8.0
C_KEY_DIM = 256
C_VAL_DIM = 512
C_GATE_RANK = 16
C_GATE_TAU = 16.0
C_CHUNK = 64

LANES = 128
SUBLANES = 8
VMEM_LIMIT = 56 * 1024 * 1024

ROW_TILE = 1024
OUT_ROW_TILE = 512
ADA_COL_TILE = 1024
EVEN_COL_TILE = 1024
ODD_COL_TILE = 2048
FF_TILE = 512
ATTN_Q_TILE = 256
ATTN_KEY_TILE = 256


def _params(sem, vmem=VMEM_LIMIT):
    return pltpu.CompilerParams(dimension_semantics=sem, vmem_limit_bytes=vmem)


def _dot(a, b):
    return jnp.dot(a, b, preferred_element_type=F32)


def _dot_nt(a, b):
    return lax.dot_general(a, b, (((1,), (1,)), ((), ())), preferred_element_type=F32)


def _dot_tn(a, b):
    return lax.dot_general(a, b, (((0,), (0,)), ((), ())), preferred_element_type=F32)


def _sigmoid(x):
    return 1.0 / (1.0 + jnp.exp(-x))


def _silu(x):
    return x * _sigmoid(x)


def _softplus(x):
    return jnp.maximum(x, 0.0) + jnp.log(1.0 + jnp.exp(-jnp.abs(x)))


def _log_sigmoid(x):
    return -_softplus(-x)


def _gelu_tanh(x):
    return 0.5 * x * (1.0 + jnp.tanh(math.sqrt(2.0 / math.pi) * (x + 0.044715 * (x * x * x))))


def _rms(x):
    return x * lax.rsqrt(jnp.mean(x * x, axis=-1, keepdims=True) + NORM_EPS)


def _norm_mod(x, g, shift, scale):
    return (_rms(x) * g) * (1.0 + scale) + shift


def _pick_tile(limit, *dims):
    t = limit
    while t > SUBLANES and any(d % t for d in dims):
        t //= 2
    assert all(d % t == 0 for d in dims), (limit, dims)
    return t


def _ada_kernel(c_ref, w_ref, b_ref, o_ref):
    s = _silu(c_ref[...]).astype(BF16)
    o_ref[...] = _dot(s, w_ref[...].astype(BF16)) + b_ref[...]


def _ada_mod(cc, ada_w, ada_b):
    depth, d, n = ada_w.shape
    rows = cc.shape[0]
    tn = _pick_tile(ADA_COL_TILE, n)
    return pl.pallas_call(
        _ada_kernel,
        grid=(depth, n // tn),
        in_specs=[
            pl.BlockSpec((rows, d), lambda l, j: (0, 0)),
            pl.BlockSpec((None, d, tn), lambda l, j: (l, 0, j)),
            pl.BlockSpec((None, 1, tn), lambda l, j: (l, 0, j)),
        ],
        out_specs=pl.BlockSpec((None, rows, tn), lambda l, j: (l, 0, j)),
        out_shape=jax.ShapeDtypeStruct((depth, rows, n), F32),
        name="ada_mod",
        compiler_params=_params(("parallel", "parallel")),
    )(cc, ada_w, ada_b.reshape(depth, 1, n))


class _Rows:
    def __init__(self, batch, seq, ctx_len, tm):
        self.batch, self.seq, self.ctx = batch, seq, ctx_len
        self.tm = tm
        self.n_lat = batch * seq // tm
        self.n_ctx = batch * ctx_len // tm
        self.per_batch = seq // tm
        self.rows_lat = batch * seq
        self.rows_all = batch * (seq + ctx_len)

    def mod_row(self, i):
        return jnp.where(i < self.n_lat, i // self.per_batch, self.batch)

    def rope_row(self, i):
        return jnp.where(i < self.n_lat, i % self.per_batch, self.per_batch)


INPROJ_ROW_CHUNK = 256


def _row_source(refs, n_lat):
    if len(refs) == 1:
        return lambda rs: refs[0][rs, :]
    is_lat = pl.program_id(0) < n_lat
    return lambda rs: jnp.where(is_lat, refs[0][rs, :], refs[1][rs, :])


def _row_specs(h, rows, width):
    tm, n_lat = rows.tm, rows.n_lat
    if not isinstance(h, tuple):
        return [pl.BlockSpec((tm, width), lambda i, *_: (i, 0))], [h]
    return [pl.BlockSpec((tm, width), lambda i, *_: (jnp.minimum(i, n_lat - 1), 0)),
            pl.BlockSpec((tm, width), lambda i, *_: (jnp.maximum(i - n_lat, 0), 0),
                         pipeline_mode=pl.Buffered(1))], list(h)


def _inproj_even_kernel(*refs, n_lat):
    *h_refs, mod_ref, g_ref, w_ref, cos_ref, sin_ref, o_ref, u_ref = refs
    h_rows = _row_source(h_refs, n_lat)
    j = pl.program_id(1)
    tm, tn = o_ref.shape
    rc = min(INPROJ_ROW_CHUNK, tm)

    def step(first):
        w = w_ref[...]
        for r in range(tm // rc):
            rs = slice(r * rc, (r + 1) * rc)
            if first:
                u = _norm_mod(h_rows(rs), g_ref[...], mod_ref[0:1, :], mod_ref[1:2, :]).astype(BF16)
                u_ref[rs, :] = u
            else:
                u = u_ref[rs, :]
            x = _dot(u, w)
            cos, sin = cos_ref[rs, :], sin_ref[rs, :]
            for g in range(tn // LANES):
                xg = x[:, g * LANES:(g + 1) * LANES]
                y = xg * cos + pltpu.roll(xg, LANES // 2, 1) * sin
                o_ref[rs, g * LANES:(g + 1) * LANES] = y.astype(o_ref.dtype)

    @pl.when(j == 0)
    def _():
        step(True)

    @pl.when(j > 0)
    def _():
        step(False)


def _inproj_even(h, mods, norm_g, w, layer, rope, rows, a_width):
    _, d, n = w.shape
    tm = rows.tm
    tn = _pick_tile(EVEN_COL_TILE, a_width, n)
    cos, sin = rope
    nq, nqk = a_width // tn, 2 * a_width // tn

    def col_type(j):
        return (j >= nq).astype(jnp.int32) + (j >= nqk).astype(jnp.int32)

    tab = pl.BlockSpec((tm, LANES), lambda i, j: (rows.rope_row(i), col_type(j)))
    h_specs, h_args = _row_specs(h, rows, d)
    return pl.pallas_call(
        functools.partial(_inproj_even_kernel, n_lat=rows.n_lat),
        grid=(rows.n_lat + rows.n_ctx, n // tn),
        in_specs=h_specs + [
            pl.BlockSpec((None, N_MOD, d), lambda i, j: (rows.mod_row(i), 0, 0)),
            pl.BlockSpec((1, d), lambda i, j: (0, 0)),
            pl.BlockSpec((None, d, tn), lambda i, j: (layer, 0, j)),
            tab, tab,
        ],
        out_specs=pl.BlockSpec((tm, tn), lambda i, j: (i, j)),
        out_shape=jax.ShapeDtypeStruct((rows.rows_all, n), BF16),
        scratch_shapes=[pltpu.VMEM((tm, d), BF16)],
        name="inproj_even",
        compiler_params=_params(("parallel", "arbitrary")),
    )(*h_args, mods, norm_g.reshape(1, d), w, cos, sin)


def _inproj_odd_kernel(h_ref, mod_ref, g_ref, w_ref, wz_ref, o_ref, z_ref, u_ref, *, key_w):
    j = pl.program_id(1)
    tm, tn = o_ref.shape
    rc = min(INPROJ_ROW_CHUNK, tm)
    col = j * tn + lax.broadcasted_iota(jnp.int32, (1, tn), 1)
    col_scale = jnp.where(col < key_w, C_KEY_DIM ** -0.5, 1.0).astype(F32)

    def step(first):
        w = w_ref[...]
        for r in range(tm // rc):
            rs = slice(r * rc, (r + 1) * rc)
            if first:
                u = _norm_mod(h_ref[rs, :], g_ref[...], mod_ref[0:1, :], mod_ref[1:2, :]).astype(BF16)
                u_ref[rs, :] = u
                zl = lax.broadcasted_iota(jnp.int32, (rc, LANES), 1)
                z_ref[rs, :] = jnp.where(zl < 2 * C_GATE_RANK, _dot(u, wz_ref[...]), 0.0)
            else:
                u = u_ref[rs, :]
            o_ref[rs, :] = (_dot(u, w) * col_scale).astype(o_ref.dtype)

    @pl.when(j == 0)
    def _():
        step(True)

    @pl.when(j > 0)
    def _():
        step(False)


def _inproj_odd(h, mods, norm_g, w, layer, rows, key_w, n):
    d = w.shape[1]
    assert n % LANES == 0 and w.shape[2] == n + 2 * C_GATE_RANK
    tm = rows.tm
    tn = _pick_tile(ODD_COL_TILE, n)
    return pl.pallas_call(
        functools.partial(_inproj_odd_kernel, key_w=key_w),
        grid=(rows.n_lat + rows.n_ctx, n // tn),
        in_specs=[
            pl.BlockSpec((tm, d), lambda i, j: (i, 0)),
            pl.BlockSpec((None, N_MOD, d), lambda i, j: (rows.mod_row(i), 0, 0)),
            pl.BlockSpec((1, d), lambda i, j: (0, 0)),
            pl.BlockSpec((None, d, tn), lambda i, j: (layer, 0, j)),
            pl.BlockSpec((None, d, LANES), lambda i, j: (layer, 0, n // LANES)),
        ],
        out_specs=[
            pl.BlockSpec((tm, tn), lambda i, j: (i, j)),
            pl.BlockSpec((tm, LANES), lambda i, j: (i, 0)),
        ],
        out_shape=[
            jax.ShapeDtypeStruct((rows.rows_all, n), BF16),
            jax.ShapeDtypeStruct((rows.rows_all, LANES), F32),
        ],
        scratch_shapes=[pltpu.VMEM((tm, d), BF16)],
        name="inproj_odd",
        compiler_params=_params(("parallel", "arbitrary")),
    )(h, mods, norm_g.reshape(1, d), w, w)


def _attn_kernel(lam_ref, ql_ref, qc_ref, kl_ref, vl_ref, kc_ref, vc_ref, g_ref, ol_ref, oc_ref,
                 k_s, v_s, s0_s, s1_s, p0_s, p1_s, *, tq, kt, post_scale):
    ctx_len, seq = kc_ref.shape[0], kl_ref.shape[0]
    total = ctx_len + seq
    nkt, nq, ng = total // kt, seq // tq, kt // LANES
    s_bufs, p_bufs = (s0_s, s1_s), (p0_s, p1_s)
    lam = lam_ref[0]
    gain = g_ref[...] * post_scale

    for src_ref, lo in ((kc_ref, 0), (kl_ref, ctx_len)):
        for c in range(src_ref.shape[0] // LANES):
            k_s[:, lo + c * LANES:lo + (c + 1) * LANES] = src_ref[c * LANES:(c + 1) * LANES, :].T
    v_s[0:ctx_len, 0:LANES] = vc_ref[...]
    v_s[ctx_len:total, 0:LANES] = vl_ref[...]
    v_s[:, LANES:2 * LANES] = jnp.ones((total, LANES), BF16)

    def map_mask(shape, mi):
        lane = lax.broadcasted_iota(jnp.int32, shape, 1)
        return ((lane // (A_HEAD_DIM // 2)) % 2) == mi

    def q_map(t, mi):
        q = ql_ref[pl.ds(pl.multiple_of(t * tq, tq), tq), :]
        return jnp.where(map_mask(q.shape, mi), q, jnp.zeros_like(q))

    def scores(qa, buf, j, m):
        s = _dot(qa, k_s[:, j * kt:(j + 1) * kt])
        s_bufs[buf][:, j * kt:(j + 1) * kt] = s
        for g in range(ng):
            m = jnp.maximum(m, s[:, g * LANES:(g + 1) * LANES])
        return m

    def exps(buf, j, mb):
        x = s_bufs[buf][:, j * kt:(j + 1) * kt]
        ps = [jnp.exp2(x[:, g * LANES:(g + 1) * LANES] - mb).astype(BF16) for g in range(ng)]
        p_bufs[buf][:, j * kt:(j + 1) * kt] = ps[0] if ng == 1 else jnp.concatenate(ps, axis=1)

    def values(buf):
        return _dot(p_bufs[buf][...], v_s[...])

    def row_max(m):
        return jnp.broadcast_to(jnp.max(m, axis=-1, keepdims=True), m.shape)

    def finish(t, r0, r1):
        o = r0[:, 0:LANES] / r0[:, LANES:2 * LANES] - lam * (r1[:, 0:LANES] / r1[:, LANES:2 * LANES])
        ol_ref[pl.ds(pl.multiple_of(t * tq, tq), tq), :] = (_rms(o) * gain).astype(ol_ref.dtype)

    neg = jnp.full((tq, LANES), -jnp.inf, F32)

    qa = q_map(0, 0)
    m = neg
    for j in range(nkt):
        m = scores(qa, 0, j, m)
    mx0 = row_max(m)
    qb = q_map(0, 1)
    m = neg
    for j in range(nkt):
        m = scores(qb, 1, j, m)
        exps(0, j, mx0)

    def tile_body(t, carry):
        m1, r0p, r1p = carry
        finish(jnp.maximum(t - 1, 0), r0p, r1p)
        mx1 = row_max(m1)
        tn = jnp.minimum(t + 1, nq - 1)
        qa = q_map(tn, 0)
        m = neg
        for j in range(nkt):
            m = scores(qa, 0, j, m)
        mx0n = row_max(m)
        for j in range(nkt):
            exps(1, j, mx1)
        r0 = values(0)
        r1 = values(1)
        qb = q_map(tn, 1)
        m = neg
        for j in range(nkt):
            m = scores(qb, 1, j, m)
            exps(0, j, mx0n)
        return m, r0, r1

    ones = jnp.ones((tq, 2 * LANES), F32)
    _, r0, r1 = lax.fori_loop(0, nq, tile_body, (m, ones, ones))
    finish(nq - 1, r0, r1)

    qc = qc_ref[...]
    outs = []
    for mi in range(2):
        qm = jnp.where(map_mask(qc.shape, mi), qc, jnp.zeros_like(qc))
        s = _dot_nt(qm, kc_ref[...])
        p = jnp.exp2(s - jnp.max(s, axis=-1, keepdims=True))
        outs.append(_dot(p.astype(BF16), vc_ref[...]) / jnp.sum(p, axis=-1, keepdims=True))
    oc_ref[...] = (_rms(outs[0] - lam * outs[1]) * gain).astype(oc_ref.dtype)


def _diff_attn(p, lam, subln_g, post_scale, batch, seq, ctx_len, heads):
    total = seq + ctx_len
    tq = _pick_tile(ATTN_Q_TILE, seq)
    kt = _pick_tile(ATTN_KEY_TILE, total)
    nkt = total // kt
    n_ctx_blk0 = batch * seq // ctx_len
    kern = functools.partial(_attn_kernel, tq=tq, kt=kt, post_scale=post_scale)

    def lat(off):
        return pl.BlockSpec((seq, LANES), lambda b, h: (b, off + h))

    def ctx(off):
        return pl.BlockSpec((ctx_len, LANES), lambda b, h: (n_ctx_blk0 + b, off + h))

    return pl.pallas_call(
        kern,
        grid=(batch, heads),
        in_specs=[
            pl.BlockSpec(memory_space=pltpu.SMEM),
            lat(0), ctx(0), lat(heads), lat(2 * heads), ctx(heads), ctx(2 * heads),
            pl.BlockSpec((1, LANES), lambda b, h: (0, 0)),
        ],
        out_specs=[
            pl.BlockSpec((seq, LANES), lambda b, h: (b, h)),
            pl.BlockSpec((ctx_len, LANES), lambda b, h: (b, h)),
        ],
        out_shape=[
            jax.ShapeDtypeStruct((batch * seq, heads * LANES), BF16),
            jax.ShapeDtypeStruct((batch * ctx_len, heads * LANES), BF16),
        ],
        scratch_shapes=[
            pltpu.VMEM((LANES, total), BF16),
            pltpu.VMEM((total, 2 * LANES), BF16),
            pltpu.VMEM((tq, total), F32),
            pltpu.VMEM((tq, total), F32),
            pltpu.VMEM((tq, total), BF16),
            pltpu.VMEM((tq, total), BF16),
        ],
        name="diff_attn",
        compiler_params=_params(("parallel", "parallel")),
    )(lam, p, p, p, p, p, p, subln_g.reshape(1, LANES))


def _dwconv(x, w, b):
    n = x.shape[0]
    row = lax.broadcasted_iota(jnp.int32, x.shape, 0)
    y = b + x * w[CONV_LEFT:CONV_LEFT + 1, :]
    for j in range(CONV_W):
        off = j - CONV_LEFT
        if off == 0:
            continue
        xs = pltpu.roll(x, (-off) % n, 0)
        ok = jnp.logical_and(row + off >= 0, row + off < n)
        y = y + jnp.where(ok, xs, 0.0) * w[j:j + 1, :]
    return y


def _lru_kernel(recl_ref, recc_ref, gtl_ref, gtc_ref, cw_ref, cb_ref, wa_ref, ba_ref, wx_ref, bx_ref,
                lam_ref, ol_ref, oc_ref, a_s, b_s, h_s, *, ctx_len, seq):
    total = ctx_len + seq
    nt = total // SUBLANES
    cw = cw_ref[...]
    cb = cb_ref[...]
    region_off = ((0, seq), (ctx_len, 0))
    decay_rate = [LRU_C * _softplus(-lam_ref[d]) for d in range(2)]
    for ri, rec_ref in enumerate((recc_ref, recl_ref)):
        n = rec_ref.shape[0]
        xc = _dwconv(rec_ref[...].astype(F32), cw, cb)
        xcb = xc.astype(BF16)
        pre = [(_dot(xcb, wa_ref[d].astype(BF16)), _dot(xcb, wx_ref[d].astype(BF16))) for d in range(2)]
        for d in range(2):
            r = _sigmoid(pre[d][0] + ba_ref[d])
            gi = _sigmoid(pre[d][1] + bx_ref[d])
            a = jnp.exp(-(r * decay_rate[d]))
            lo = region_off[ri][d]
            a_s[d, lo:lo + n, :] = a
            b_s[d, lo:lo + n, :] = jnp.sqrt(1.0 - a * a) * gi * xc

    row = lax.broadcasted_iota(jnp.int32, (SUBLANES, LANES), 0)

    def scan_tile(a, b, reverse):
        s = 1
        while s < SUBLANES:
            sh = (SUBLANES - s) if reverse else s
            a_sh = pltpu.roll(a, sh, 0)
            b_sh = pltpu.roll(b, sh, 0)
            ok = (row < SUBLANES - s) if reverse else (row >= s)
            b = jnp.where(ok, a * b_sh + b, b)
            a = jnp.where(ok, a * a_sh, a)
            s *= 2
        return a, b

    def body(p, carry):
        cf, cbk = carry
        tf = pl.multiple_of(p * SUBLANES, SUBLANES)
        a, b = scan_tile(a_s[0, pl.ds(tf, SUBLANES), :], b_s[0, pl.ds(tf, SUBLANES), :], False)
        hf = a * cf + b
        h_s[0, pl.ds(tf, SUBLANES), :] = hf
        tb = pl.multiple_of((nt - 1 - p) * SUBLANES, SUBLANES)
        a, b = scan_tile(a_s[1, pl.ds(tb, SUBLANES), :], b_s[1, pl.ds(tb, SUBLANES), :], True)
        hb = a * cbk + b
        h_s[1, pl.ds(tb, SUBLANES), :] = hb
        return hf[SUBLANES - 1:SUBLANES, :], hb[0:1, :]

    zero = jnp.zeros((1, LANES), F32)
    lax.fori_loop(0, nt, body, (zero, zero), unroll=4)

    oc_ref[...] = (_gelu_tanh(gtc_ref[...].astype(F32))
                   * (h_s[0, 0:ctx_len, :] + h_s[1, seq:total, :])).astype(oc_ref.dtype)
    ol_ref[...] = (_gelu_tanh(gtl_ref[...].astype(F32))
                   * (h_s[0, ctx_len:total, :] + h_s[1, 0:seq, :])).astype(ol_ref.dtype)


def _rglru(p, conv_w, conv_b, wa, ba, wx, bx, lru_lam, batch, seq, ctx_len, a_width, b_width):
    nb = b_width // B_BLOCK_DIM
    gt_off = 3 * a_width // LANES
    rec_off = (3 * a_width + b_width) // LANES
    n_ctx_blk0 = batch * seq // ctx_len
    total = seq + ctx_len
    kern = functools.partial(_lru_kernel, ctx_len=ctx_len, seq=seq)
    vec = pl.BlockSpec((2, 1, LANES), lambda b, n: (0, 0, n))
    mat = pl.BlockSpec((2, None, B_BLOCK_DIM, B_BLOCK_DIM), lambda b, n: (0, n, 0, 0))
    return pl.pallas_call(
        kern,
        grid=(batch, nb),
        in_specs=[
            pl.BlockSpec((seq, LANES), lambda b, n: (b, rec_off + n)),
            pl.BlockSpec((ctx_len, LANES), lambda b, n: (n_ctx_blk0 + b, rec_off + n)),
            pl.BlockSpec((seq, LANES), lambda b, n: (b, gt_off + n)),
            pl.BlockSpec((ctx_len, LANES), lambda b, n: (n_ctx_blk0 + b, gt_off + n)),
            pl.BlockSpec((CONV_W, LANES), lambda b, n: (0, n)),
            pl.BlockSpec((1, LANES), lambda b, n: (0, n)),
            mat, vec, mat, vec, vec,
        ],
        out_specs=[
            pl.BlockSpec((seq, LANES), lambda b, n: (b, n)),
            pl.BlockSpec((ctx_len, LANES), lambda b, n: (b, n)),
        ],
        out_shape=[
            jax.ShapeDtypeStruct((batch * seq, b_width), BF16),
            jax.ShapeDtypeStruct((batch * ctx_len, b_width), BF16),
        ],
        scratch_shapes=[
            pltpu.VMEM((2, total, LANES), F32),
            pltpu.VMEM((2, total, LANES), F32),
            pltpu.VMEM((2, total, LANES), F32),
        ],
        name="rglru",
        compiler_params=_params(("parallel", "parallel")),
    )(p, p, p, p, conv_w, conv_b.reshape(1, b_width), wa, ba.reshape(2, 1, b_width),
      wx, bx.reshape(2, 1, b_width), lru_lam.reshape(2, 1, b_width))


GLA_UNROLL = 4


def _gla_kernel(ql_ref, kl_ref, vl_ref, rl_ref, zl_ref, qc_ref, kc_ref, vc_ref, rc_ref, zc_ref,
                w2_ref, gb_ref, g_ref, al_ref, ac_ref, st_s, ol_s, oc_s, *, ctx_len, seq):
    c = C_CHUNK
    ri = lax.broadcasted_iota(jnp.int32, (c, c), 0)
    ci = lax.broadcasted_iota(jnp.int32, (c, c), 1)
    keep = (ri >= ci, ri <= ci)
    tri = tuple(jnp.where(k, 1.0, 0.0).astype(BF16) for k in keep)
    end_row = (c - 1, 0)

    def step(q_ref, k_ref, v_ref, z_ref, o_ref, chains, accumulate):
        rows = [pl.ds(t0, c) for t0, _ in chains]
        dirs = [d for _, d in chains]
        xs = [_dot(z_ref[r, :].astype(BF16), w2_ref[d]) + gb_ref[d] for r, d in zip(rows, dirs)]
        gs = [_log_sigmoid(x) * (1.0 / C_GATE_TAU) for x in xs]
        his = [g.astype(BF16) for g in gs]
        los = [(g - hi.astype(F32)).astype(BF16) for g, hi in zip(gs, his)]
        bcums = [_dot(tri[d], hi) + _dot(tri[d], lo) for d, hi, lo in zip(dirs, his, los)]
        qes, kes, kds, decays = [], [], [], []
        for r, d, bcum in zip(rows, dirs, bcums):
            b_end = bcum[end_row[d]:end_row[d] + 1, :]
            k = k_ref[r, :].astype(F32)
            qes.append((q_ref[r, :].astype(F32) * jnp.exp(bcum)).astype(BF16))
            kes.append((k * jnp.exp(-bcum)).astype(BF16))
            kds.append((k * jnp.exp(b_end - bcum)).astype(BF16))
            decays.append(jnp.exp(b_end))
        raw = [_dot_nt(qe, ke) for qe, ke in zip(qes, kes)]
        ups = [_dot_tn(v_ref[r, :], kd) for r, kd in zip(rows, kds)]
        atts = [jnp.where(keep[d], a, 0.0).astype(BF16) for d, a in zip(dirs, raw)]
        intra = [_dot(att, v_ref[r, :]) for att, r in zip(atts, rows)]
        for r, d, qe, up, decay, oi in zip(rows, dirs, qes, ups, decays, intra):
            st = st_s[d]
            o = oi + _dot_nt(qe, st.astype(BF16))
            st_s[d] = st * decay + up
            if accumulate:
                o_ref[r, :] += o
            else:
                o_ref[r, :] = o

    def sweep(q_ref, k_ref, v_ref, z_ref, o_ref, n):
        half = n // 2
        unroll = math.gcd(GLA_UNROLL, half)

        def run(lo, accumulate):
            def body(i, carry):
                chains = []
                for u in range(unroll):
                    ci_ = i * unroll + u
                    chains.append((pl.multiple_of(ci_ * c, c), 0))
                    chains.append((pl.multiple_of((n - 1 - ci_) * c, c), 1))
                step(q_ref, k_ref, v_ref, z_ref, o_ref, chains, accumulate)
                return carry
            lax.fori_loop(lo // unroll, (lo + half) // unroll, body, 0)

        run(0, False)
        run(half, True)

    st_s[...] = jnp.zeros_like(st_s)
    sweep(qc_ref, kc_ref, vc_ref, zc_ref, oc_s, ctx_len // c)
    sweep(ql_ref, kl_ref, vl_ref, zl_ref, ol_s, seq // c)

    g = g_ref[...]
    ac_ref[...] = (_rms(oc_s[...]) * g * _silu(rc_ref[...].astype(F32))).astype(ac_ref.dtype)
    al_ref[...] = (_rms(ol_s[...]) * g * _silu(rl_ref[...].astype(F32))).astype(al_ref.dtype)


def _gla(p, z, w2p, gate_b, norm_g, batch, seq, ctx_len, heads):
    d_model = heads * C_VAL_DIM
    key_w = heads * C_KEY_DIM
    n_ctx_blk0 = batch * seq // ctx_len
    kern = functools.partial(_gla_kernel, ctx_len=ctx_len, seq=seq)

    def lat(width, off):
        return pl.BlockSpec((seq, width), lambda b, h: (b, off + h))

    def ctx(width, off):
        return pl.BlockSpec((ctx_len, width), lambda b, h: (n_ctx_blk0 + b, off + h))

    specs = [
        lat(C_KEY_DIM, 0), lat(C_KEY_DIM, heads), lat(C_VAL_DIM, heads), lat(C_VAL_DIM, 2 * heads),
        pl.BlockSpec((seq, LANES), lambda b, h: (b, 0)),
        ctx(C_KEY_DIM, 0), ctx(C_KEY_DIM, heads), ctx(C_VAL_DIM, heads), ctx(C_VAL_DIM, 2 * heads),
        pl.BlockSpec((ctx_len, LANES), lambda b, h: (n_ctx_blk0 + b, 0)),
    ]
    specs += [
        pl.BlockSpec((2, LANES, C_KEY_DIM), lambda b, h: (0, 0, h)),
        pl.BlockSpec((2, 1, C_KEY_DIM), lambda b, h: (0, 0, h)),
        pl.BlockSpec((1, C_VAL_DIM), lambda b, h: (0, 0)),
    ]
    return pl.pallas_call(
        kern,
        grid=(batch, heads),
        in_specs=specs,
        out_specs=[
            pl.BlockSpec((seq, C_VAL_DIM), lambda b, h: (b, h)),
            pl.BlockSpec((ctx_len, C_VAL_DIM), lambda b, h: (b, h)),
        ],
        out_shape=[
            jax.ShapeDtypeStruct((batch * seq, d_model), BF16),
            jax.ShapeDtypeStruct((batch * ctx_len, d_model), BF16),
        ],
        scratch_shapes=[
            pltpu.VMEM((2, C_VAL_DIM, C_KEY_DIM), F32),
            pltpu.VMEM((seq, C_VAL_DIM), F32),
            pltpu.VMEM((ctx_len, C_VAL_DIM), F32),
        ],
        name="gla",
        compiler_params=_params(("parallel", "parallel")),
    )(p, p, p, p, z, p, p, p, p, z, w2p, gate_b.reshape(2, 1, key_w), norm_g.reshape(1, C_VAL_DIM))


def _outproj_kernel(al0_ref, ac0_ref, al1_ref, ac1_ref, w0_ref, w1_ref, *refs, n_lat):
    *h_refs, mod_ref, o_ref = refs
    is_lat = pl.program_id(0) < n_lat
    a0 = jnp.where(is_lat, al0_ref[...], ac0_ref[...])
    a1 = jnp.where(is_lat, al1_ref[...], ac1_ref[...])
    acc = _dot(a0, w0_ref[...]) + _dot(a1, w1_ref[...])
    o_ref[...] = _row_source(h_refs, n_lat)(slice(None)) + mod_ref[2:3, :] * acc


def _outproj(srcs, w, layer, h, mods, rows, n_tiles):
    d = w.shape[2]
    kh = w.shape[1] // 2
    tm = rows.tm
    n_lat = rows.n_lat
    specs, args = [], []
    for lat_arr, lat_col, ctx_arr, ctx_row0, ctx_col in srcs:
        specs.append(pl.BlockSpec((tm, kh), lambda i, c=lat_col: (jnp.minimum(i, n_lat - 1), c)))
        specs.append(pl.BlockSpec((tm, kh), lambda i, r=ctx_row0, c=ctx_col: (r + jnp.maximum(i - n_lat, 0), c)))
        args += [lat_arr, ctx_arr]
    kern = functools.partial(_outproj_kernel, n_lat=n_lat)
    h_specs, h_args = _row_specs(h, rows, d)
    return pl.pallas_call(
        kern,
        grid=(n_tiles,),
        in_specs=specs + [
            pl.BlockSpec((None, kh, d), lambda i: (layer, 0, 0)),
            pl.BlockSpec((None, kh, d), lambda i: (layer, 1, 0)),
        ] + h_specs + [
            pl.BlockSpec((None, N_MOD, d), lambda i: (rows.mod_row(i), 0, 0)),
        ],
        out_specs=pl.BlockSpec((tm, d), lambda i: (i, 0)),
        out_shape=jax.ShapeDtypeStruct((n_tiles * tm, d), F32),
        name="outproj",
        compiler_params=_params(("parallel",)),
    )(*args, w, w, *h_args, mods)


MLP_ROW_CHUNK = 512


WEIGHT_SLOTS = 3


def _mlp_kernel(h_ref, mod_ref, g_ref, w1_hbm, w2_hbm, fg_ref, o_ref, u_ref, w1_buf, w2_buf, sem,
                *, nf, n_steps, layer, tf, final_norm):
    j = pl.program_id(1)
    s = pl.program_id(0) * nf + j
    tm = o_ref.shape[0]
    rc = min(MLP_ROW_CHUNK, tm)

    def weight_copies(step):
        jj, slot = step % nf, step % WEIGHT_SLOTS
        c0 = jj * tf if isinstance(jj, int) else pl.multiple_of(jj * tf, tf)
        return (pltpu.make_async_copy(w1_hbm.at[layer, :, pl.ds(c0, tf)], w1_buf.at[slot], sem.at[0, slot]),
                pltpu.make_async_copy(w2_hbm.at[layer, pl.ds(c0, tf), :], w2_buf.at[slot], sem.at[1, slot]))

    @pl.when(s == 0)
    def _():
        for st in range(min(WEIGHT_SLOTS - 1, n_steps)):
            for cp in weight_copies(st):
                cp.start()

    @pl.when(s + (WEIGHT_SLOTS - 1) < n_steps)
    def _():
        for cp in weight_copies(s + (WEIGHT_SLOTS - 1)):
            cp.start()

    for cp in weight_copies(s):
        cp.wait()
    w1_ref, w2_ref = w1_buf.at[s % WEIGHT_SLOTS], w2_buf.at[s % WEIGHT_SLOTS]

    def step(first):
        w1, w2 = w1_ref[...], w2_ref[...]
        for r in range(tm // rc):
            rs = slice(r * rc, (r + 1) * rc)
            if first:
                u = _norm_mod(h_ref[rs, :], g_ref[...], mod_ref[3:4, :], mod_ref[4:5, :]).astype(BF16)
                u_ref[rs, :] = u
            else:
                u = u_ref[rs, :]
            t = jnp.maximum(_dot(u, w1), 0.0)
            part = _dot((t * t).astype(BF16), w2)
            if first:
                o_ref[rs, :] = part
            else:
                o_ref[rs, :] += part

    @pl.when(j == 0)
    def _():
        step(True)

    @pl.when(j > 0)
    def _():
        step(False)

    @pl.when(j == nf - 1)
    def _():
        y = h_ref[...] + mod_ref[5:6, :] * o_ref[...]
        if final_norm:
            y = _rms(y) * fg_ref[...]
        o_ref[...] = y


def _mlp(h, mods, norm_g, w1, w2, layer, final_g, rows, n_tiles, tf, final_norm):
    _, d, ff = w1.shape
    nf = ff // tf
    tm = rows.tm
    kern = functools.partial(_mlp_kernel, nf=nf, n_steps=n_tiles * nf, layer=layer, tf=tf, final_norm=final_norm)
    return pl.pallas_call(
        kern,
        grid=(n_tiles, nf),
        in_specs=[
            pl.BlockSpec((tm, d), lambda i, j: (i, 0)),
            pl.BlockSpec((None, N_MOD, d), lambda i, j: (rows.mod_row(i), 0, 0)),
            pl.BlockSpec((1, d), lambda i, j: (0, 0)),
            pl.BlockSpec(memory_space=pl.ANY),
            pl.BlockSpec(memory_space=pl.ANY),
            pl.BlockSpec((1, d), lambda i, j: (0, 0)),
        ],
        out_specs=pl.BlockSpec((tm, d), lambda i, j: (i, 0)),
        out_shape=jax.ShapeDtypeStruct((n_tiles * tm, d), F32),
        scratch_shapes=[pltpu.VMEM((tm, d), BF16),
                        pltpu.VMEM((WEIGHT_SLOTS, d, tf), BF16),
                        pltpu.VMEM((WEIGHT_SLOTS, tf, d), BF16),
                        pltpu.SemaphoreType.DMA((2, WEIGHT_SLOTS))],
        name="mlp",
        compiler_params=_params(("arbitrary", "arbitrary"), vmem=VMEM_LIMIT + 4 * 1024 * 1024),
    )(h, mods, norm_g.reshape(1, d), w1, w2, final_g.reshape(1, d))


def _rope_tables(seq, pad_rows):
    inv = ROPE_BASE ** (-jnp.arange(ROPE_FREQS, dtype=F32) / ROPE_FREQS)
    t = jnp.arange(seq, dtype=jnp.int32)
    pos_r = (t // GRID_W).astype(F32)[:, None]
    pos_c = (t % GRID_W).astype(F32)[:, None]
    lane = jnp.arange(LANES, dtype=jnp.int32)[None, :]
    second = (lane // A_HEAD_DIM) == 1
    is_col = ((lane % (2 * ROPE_FREQS)) // ROPE_FREQS) == 1
    ang = jnp.where(is_col, pos_c, pos_r) * inv[lane % ROPE_FREQS]
    cos = jnp.cos(ang)
    sin = jnp.where(second, jnp.sin(ang), -jnp.sin(ang))
    qs = A_HEAD_DIM ** -0.5 * math.log2(math.e)
    ones = jnp.ones((pad_rows, LANES), F32)
    zeros = jnp.zeros((pad_rows, LANES), F32)
    one_all = jnp.ones((seq + pad_rows, LANES), F32)
    zero_all = jnp.zeros((seq + pad_rows, LANES), F32)

    def table(rot, pad):
        body = jnp.concatenate([rot, pad])
        return jnp.concatenate([qs * body, body, one_all if pad is ones else zero_all], axis=1)

    return table(cos, ones), table(sin, zeros)


def _permute_qk_columns(w, a_width):
    lead = w.shape[:-1]
    qk = w[..., :2 * a_width].reshape(*lead, -1, 2, 2, 2, ROPE_FREQS)
    qk = jnp.swapaxes(jnp.swapaxes(qk, -2, -3), -3, -4)
    return jnp.concatenate([qk.reshape(*lead, 2 * a_width), w[..., 2 * a_width:]], axis=-1)


def kernel(x, c, ctx, c_ctx, ada_w, ada_b, norm1_g, norm2_g, mlp_w1, mlp_w2, ev_w_in, ev_w_out, ev_lambda_q1, ev_lambda_k1, ev_lambda_q2, ev_lambda_k2, ev_subln_g, ev_conv_w, ev_conv_b, ev_lru_wa, ev_lru_ba, ev_lru_wx, ev_lru_bx, ev_lru_lam, od_w_in, od_w_out, od_gate_w2, od_gate_b, od_norm_g, final_g):
    batch, seq, d = x.shape
    ctx_len = ctx.shape[1]
    depth = ada_w.shape[0]
    a_width = d // 2
    b_width = d - a_width
    a_heads = a_width // (2 * A_HEAD_DIM)
    c_heads = d // C_VAL_DIM
    key_w = c_heads * C_KEY_DIM

    tm = _pick_tile(ROW_TILE, seq, batch * ctx_len)
    rows = _Rows(batch, seq, ctx_len, tm)
    rows_out = _Rows(batch, seq, ctx_len, _pick_tile(OUT_ROW_TILE, seq, batch * ctx_len))
    tf = _pick_tile(FF_TILE, mlp_w1.shape[2])

    pad = (-(batch + 1)) % SUBLANES
    cc = jnp.concatenate([c, c_ctx[None, :], jnp.zeros((pad, d), F32)], axis=0)
    mods_all = _ada_mod(cc, ada_w, ada_b).reshape(depth, batch + 1 + pad, N_MOD, d)

    rope = _rope_tables(seq, tm)
    h = (x.reshape(batch * seq, d), ctx.reshape(batch * ctx_len, d))

    ev_w_in_b = _permute_qk_columns(ev_w_in, a_width).astype(BF16)
    od_w_in_b = od_w_in.astype(BF16)
    ev_w_out_b, od_w_out_b = ev_w_out.astype(BF16), od_w_out.astype(BF16)
    mlp_w1_b, mlp_w2_b = mlp_w1.astype(BF16), mlp_w2.astype(BF16)

    for i in range(depth):
        last = i == depth - 1
        mods = mods_all[i]
        j = i // 2
        if i % 2 == 0:
            lam_init = 0.8 - 0.6 * math.exp(-0.3 * i)
            lam = (jnp.exp(jnp.sum(ev_lambda_q1[j] * ev_lambda_k1[j]))
                   - jnp.exp(jnp.sum(ev_lambda_q2[j] * ev_lambda_k2[j])) + lam_init).reshape(1).astype(F32)
            p = _inproj_even(h, mods, norm1_g[i], ev_w_in_b, j, rope, rows, a_width)
            ya_lat, ya_ctx = _diff_attn(p, lam, ev_subln_g[j], 1.0 - lam_init, batch, seq, ctx_len, a_heads)
            yb_lat, yb_ctx = _rglru(p, ev_conv_w[j], ev_conv_b[j], ev_lru_wa[j], ev_lru_ba[j],
                                    ev_lru_wx[j], ev_lru_bx[j], ev_lru_lam[j],
                                    batch, seq, ctx_len, a_width, b_width)
            srcs = [(ya_lat, 0, ya_ctx, 0, 0), (yb_lat, 0, yb_ctx, 0, 0)]
            w_out = ev_w_out_b
        else:
            n_main = 2 * key_w + 2 * d
            p, z = _inproj_odd(h, mods, norm1_g[i], od_w_in_b, j, rows, key_w, n_main)
            w2p = jnp.zeros((2, LANES, key_w), F32)
            for dd in range(2):
                w2p = w2p.at[dd, dd * C_GATE_RANK:(dd + 1) * C_GATE_RANK, :].set(od_gate_w2[j, dd])
            a_lat, a_ctx = _gla(p, z, w2p.astype(BF16), od_gate_b[j], od_norm_g[j], batch, seq, ctx_len, c_heads)
            srcs = [(a_lat, 0, a_ctx, 0, 0), (a_lat, 1, a_ctx, 0, 1)]
            w_out = od_w_out_b

        n_out = rows_out.n_lat if last else rows_out.n_lat + rows_out.n_ctx
        h = _outproj(srcs, w_out, j, h, mods, rows_out, n_out)
        n_mlp = rows.n_lat if last else rows.n_lat + rows.n_ctx
        h = _mlp(h, mods, norm2_g[i], mlp_w1_b, mlp_w2_b, i, final_g, rows, n_mlp, tf, last)

    return h.reshape(batch, seq, d)
```
